```python
import math
import jax, jax.numpy as jnp
from jax import lax
import numpy as np

D_MODEL = 2048
BATCH = 4
SEQ = 2048
DEPTH = 4
DEC_BATCH = 128
DEC_SEQ = 4
PAST_LEN = 16384
PAGE_SIZE = 128

N_MIXERS = 3
N_HGRN = (DEPTH + 2) // 3
N_MLSTM = (DEPTH + 1) // 3
N_GMLP = DEPTH // 3

HG_DK = 128
HG_HEADS = D_MODEL // HG_DK
HG_DV = D_MODEL // HG_HEADS
HG_CHUNK = 64

ML_HEADS = 8
ML_DQK = D_MODEL // (2 * ML_HEADS)
ML_DV = D_MODEL // ML_HEADS
ML_CHUNK = 64
GATE_CAP = 15.0

GM_DIM = D_MODEL
GM_CHUNK = 128
GM_GROUPS = 16
GM_GDIM = GM_DIM // GM_GROUPS

FFN_DIM = 256 * ((8 * D_MODEL // 3 + 255) // 256)
CONV_W = 3
EPS = 1e-6

kernel_name = 'hybrid_hgrn2_mlstm_gmlp_convffn_step'


def _rmsnorm(x, g):
    xf = x.astype(jnp.float32)
    y = xf * lax.rsqrt(jnp.mean(xf * xf, axis=-1, keepdims=True) + EPS)
    return (y * g.astype(jnp.float32)).astype(x.dtype)


def _layernorm(x, g, b):
    xf = x.astype(jnp.float32)
    mu = jnp.mean(xf, axis=-1, keepdims=True)
    xc = xf - mu
    y = xc * lax.rsqrt(jnp.mean(xc * xc, axis=-1, keepdims=True) + EPS)
    return (y * g.astype(jnp.float32) + b.astype(jnp.float32)).astype(x.dtype)


def _chunk_len(T, c):
    return T if T <= c else c


def _to_chunks(a, L):
    B, T, H, d = a.shape
    return a.reshape(B, T // L, L, H, d).transpose(1, 0, 3, 2, 4)


def _gates_to_chunks(a, L):
    B, T, H = a.shape
    return a.reshape(B, T // L, L, H).transpose(1, 0, 3, 2)


def _from_chunks(a):
    n, B, H, L, d = a.shape
    return a.transpose(1, 0, 3, 2, 4).reshape(B, n * L, H, d)


def _hgrn_mixer(h, S0, w_q, w_f, w_i, w_g, lb, onorm, w_o):
    B, T, _ = h.shape
    f32 = jnp.float32
    q = jax.nn.silu(h @ w_q).astype(f32).reshape(B, T, HG_HEADS, HG_DK)
    fpre = (h @ w_f).astype(f32).reshape(B, T, HG_HEADS, HG_DK)
    v = (h @ w_i).astype(f32).reshape(B, T, HG_HEADS, HG_DV)
    lbh = jnp.maximum(lb.astype(f32), 0.0).reshape(HG_HEADS, HG_DK)
    logf = jnp.logaddexp(jnp.log(lbh), jnp.log1p(-lbh) + jax.nn.log_sigmoid(fpre))
    k = (1.0 - lbh) * jax.nn.sigmoid(-fpre)
    L = _chunk_len(T, HG_CHUNK)
    tri = jnp.tril(jnp.ones((L, L), dtype=bool))

    def step(S, xs):
        qc, kc, vc, lfc = xs
        b = jnp.cumsum(lfc, axis=2)
        o_inter = jnp.einsum('bhtk,bhkv->bhtv', qc * jnp.exp(b), S)
        diff = b[:, :, :, None, :] - b[:, :, None, :, :]
        decay = jnp.exp(jnp.where(tri[:, :, None], diff, -jnp.inf))
        A = jnp.einsum('bhtk,bhsk,bhtsk->bhts', qc, kc, decay)
        o = o_inter + jnp.einsum('bhts,bhsv->bhtv', A, vc)
        bL = b[:, :, -1:, :]
        S = jnp.exp(bL[:, :, 0])[..., None] * S + jnp.einsum('bhsk,bhsv->bhkv', kc * jnp.exp(bL - b), vc)
        return S, o

    S, o = lax.scan(step, S0.astype(f32), (_to_chunks(q, L), _to_chunks(k, L), _to_chunks(v, L), _to_chunks(logf, L)))
    o = _from_chunks(o).reshape(B, T, HG_HEADS * HG_DV)
    o = _rmsnorm(o, onorm).astype(h.dtype) * jax.nn.sigmoid(h @ w_g)
    return o @ w_o, S


def _mlstm_mixer(h, C0, n0, m0, w_q, w_k, w_v, w_og, w_if, b_if, hnorm, w_out):
    B, T, _ = h.shape
    f32 = jnp.float32
    q = (h @ w_q).astype(f32).reshape(B, T, ML_HEADS, ML_DQK)
    k = (h @ w_k).astype(f32).reshape(B, T, ML_HEADS, ML_DQK) * (ML_DQK ** -0.5)
    v = (h @ w_v).astype(f32).reshape(B, T, ML_HEADS, ML_DV)
    gates = GATE_CAP * jnp.tanh((h @ w_if + b_if).astype(f32) / GATE_CAP)
    ig = gates[..., :ML_HEADS]
    lf = jax.nn.log_sigmoid(gates[..., ML_HEADS:])
    L = _chunk_len(T, ML_CHUNK)
    tri = jnp.tril(jnp.ones((L, L), dtype=bool))

    def step(carry, xs):
        C, n, m = carry
        qc, kc, vc, igc, lfc = xs
        b = jnp.cumsum(lfc, axis=-1)
        dlog = jnp.where(tri, b[..., :, None] - b[..., None, :] + igc[..., None, :], -jnp.inf)
        inter = b + m[..., None]
        mt = jnp.maximum(inter, jnp.max(dlog, axis=-1))
        wts = jnp.exp(dlog - mt[..., None]) * jnp.einsum('bhtd,bhsd->bhts', qc, kc)
        sc = jnp.exp(inter - mt)
        num = sc[..., None] * jnp.einsum('bhtd,bhdv->bhtv', qc, C) + jnp.einsum('bhts,bhsv->bhtv', wts, vc)
        den = sc * jnp.einsum('bhtd,bhd->bht', qc, n) + jnp.sum(wts, axis=-1)
        out = num / jnp.maximum(jnp.abs(den), jnp.exp(-mt))[..., None]
        mL = mt[..., -1]
        sc_state = jnp.exp(b[..., -1] + m - mL)
        wk = jnp.exp(b[..., -1:] - b + igc - mL[..., None])
        C = sc_state[..., None, None] * C + jnp.einsum('bhs,bhsd,bhsv->bhdv', wk, kc, vc)
        n = sc_state[..., None] * n + jnp.einsum('bhs,bhsd->bhd', wk, kc)
        return (C, n, mL), out

    (C, n, m), o = lax.scan(step, (C0.astype(f32), n0.astype(f32), m0.astype(f32)),
                            (_to_chunks(q, L), _to_chunks(k, L), _to_chunks(v, L),
                             _gates_to_chunks(ig, L), _gates_to_chunks(lf, L)))
    o = _from_chunks(o)
    o = _rmsnorm(o, hnorm.reshape(ML_HEADS, ML_DV)).reshape(B, T, ML_HEADS * ML_DV)
    o = o.astype(h.dtype) * jax.nn.sigmoid(h @ w_og)
    return o @ w_out, (C, n, m)


def _gmlp_mixer(h, w_in, b_in, vg, vb, w_s, b_s, w_out):
    B, T, _ = h.shape
    z = jax.nn.gelu(h @ w_in + b_in, approximate=False)
    u, v = jnp.split(z, 2, axis=-1)
    v = _layernorm(v, vg, vb)
    L = _chunk_len(T, GM_CHUNK)
    tri = jnp.tril(jnp.ones((L, L), dtype=bool))
    ws = jnp.where(tri, w_s[:, :L, :L], 0.0)
    vc = v.reshape(B, T // L, L, GM_GROUPS, GM_GDIM)
    mix = jnp.einsum('gts,bnsgd->bntgd', ws, vc) + b_s[:, :L].T[None, None, :, :, None]
    o = u * mix.reshape(B, T, GM_DIM)
    return o @ w_out, v


def _conv_ffn(h, buf, w_up, cw, cb, w_down):
    T = h.shape[1]
    up = h @ w_up
    hp = jnp.concatenate([buf.astype(up.dtype), up], axis=1)
    y = cb
    for j in range(CONV_W):
        y = y + cw[j] * hp[:, j:j + T]
    a, g = jnp.split(y, 2, axis=-1)
    return (a * jax.nn.silu(g)) @ w_down, hp[:, T:]


def _trunk(x, S_in, C_in, n_in, m_in, conv_in, p):
    lb = jax.nn.softmax(p['hgrn_lb'].astype(jnp.float32), axis=0)
    lbs = jnp.cumsum(lb, axis=0) - lb[0]
    new_S, new_C, new_n, new_m, new_v, new_conv = [], [], [], [], [], []
    for i in range(DEPTH):
        j = i // N_MIXERS
        kind = i % N_MIXERS
        h = _rmsnorm(x, p['norm_mix'][i])
        if kind == 0:
            o, S = _hgrn_mixer(h, S_in[j], p['hgrn_w_q'][j], p['hgrn_w_f'][j], p['hgrn_w_i'][j],
                               p['hgrn_w_g'][j], lbs[j], p['hgrn_onorm'][j], p['hgrn_w_o'][j])
            new_S.append(S)
        elif kind == 1:
            o, (C, n, m) = _mlstm_mixer(h, C_in[j], n_in[j], m_in[j], p['mlstm_w_q'][j], p['mlstm_w_k'][j],
                                        p['mlstm_w_v'][j], p['mlstm_w_og'][j], p['mlstm_w_if'][j],
                                        p['mlstm_b_if'][j], p['mlstm_hnorm'][j], p['mlstm_w_out'][j])
            new_C.append(C)
            new_n.append(n)
            new_m.append(m)
        else:
            o, v = _gmlp_mixer(h, p['gmlp_w_in'][j], p['gmlp_b_in'][j], p['gmlp_vnorm_g'][j],
                               p['gmlp_vnorm_b'][j], p['gmlp_w_s'][j], p['gmlp_b_s'][j], p['gmlp_w_out'][j])
            new_v.append(v)
        x = x + o
        h = _rmsnorm(x, p['norm_ffn'][i])
        o, buf = _conv_ffn(h, conv_in[i], p['ffn_w_up'][i], p['ffn_conv_w'][i], p['ffn_conv_b'][i], p['ffn_w_down'][i])
        new_conv.append(buf)
        x = x + o
    y = _rmsnorm(x, p['norm_final'])
    return (y, jnp.stack(new_S), jnp.stack(new_C), jnp.stack(new_n), jnp.stack(new_m),
            jnp.stack(new_v), jnp.stack(new_conv))


def setup_inputs(seed: int = 0) -> dict:
    key = jax.random.key(seed)
    ks = iter(jax.random.split(key, 48))

    def nrm(shape, scale):
        return scale * jax.random.normal(next(ks), shape, jnp.float32)

    F2 = 2 * FFN_DIM
    res = (2 * DEPTH) ** -0.5
    dn = D_MODEL ** -0.5
    return {
        'x_prompt': nrm((BATCH, SEQ, D_MODEL), 1.0),
        'x_sample': nrm((DEC_BATCH, DEC_SEQ, D_MODEL), 1.0),
        'state_hgrn_S': nrm((N_HGRN, DEC_BATCH, HG_HEADS, HG_DK, HG_DV), 0.5),
        'state_mlstm_C': nrm((N_MLSTM, DEC_BATCH, ML_HEADS, ML_DQK, ML_DV), 0.3),
        'state_mlstm_n': nrm((N_MLSTM, DEC_BATCH, ML_HEADS, ML_DQK), 0.3),
        'state_mlstm_m': nrm((N_MLSTM, DEC_BATCH, ML_HEADS), 1.0),
        'state_ffn_conv': nrm((DEPTH, DEC_BATCH, CONV_W - 1, F2), 1.0),
        'norm_mix': 1.0 + nrm((DEPTH, D_MODEL), 0.02),
        'norm_ffn': 1.0 + nrm((DEPTH, D_MODEL), 0.02),
        'norm_final': 1.0 + nrm((D_MODEL,), 0.02),
        'hgrn_w_q': nrm((N_HGRN, D_MODEL, HG_HEADS * HG_DK), dn),
        'hgrn_w_f': nrm((N_HGRN, D_MODEL, HG_HEADS * HG_DK), dn),
        'hgrn_w_i': nrm((N_HGRN, D_MODEL, HG_HEADS * HG_DV), dn),
        'hgrn_w_g': nrm((N_HGRN, D_MODEL, HG_HEADS * HG_DV), dn),
        'hgrn_lb': nrm((N_HGRN, HG_HEADS * HG_DK), 0.5),
        'hgrn_onorm': 1.0 + nrm((N_HGRN, HG_HEADS * HG_DV), 0.02),
        'hgrn_w_o': nrm((N_HGRN, HG_HEADS * HG_DV, D_MODEL), (HG_HEADS * HG_DV) ** -0.5 * res),
        'mlstm_w_q': nrm((N_MLSTM, D_MODEL, ML_HEADS * ML_DQK), dn),
        'mlstm_w_k': nrm((N_MLSTM, D_MODEL, ML_HEADS * ML_DQK), dn),
        'mlstm_w_v': nrm((N_MLSTM, D_MODEL, ML_HEADS * ML_DV), dn),
        'mlstm_w_og': nrm((N_MLSTM, D_MODEL, ML_HEADS * ML_DV), dn),
        'mlstm_w_if': nrm((N_MLSTM, D_MODEL, 2 * ML_HEADS), dn),
        'mlstm_b_if': jnp.concatenate([nrm((N_MLSTM, ML_HEADS), 0.1) - 1.0,
                                       3.0 + nrm((N_MLSTM, ML_HEADS), 0.5)], axis=-1),
        'mlstm_hnorm': 1.0 + nrm((N_MLSTM, ML_HEADS * ML_DV), 0.02),
        'mlstm_w_out': nrm((N_MLSTM, ML_HEADS * ML_DV, D_MODEL), (ML_HEADS * ML_DV) ** -0.5 * res),
        'gmlp_w_in': nrm((N_GMLP, D_MODEL, 2 * GM_DIM), dn),
        'gmlp_b_in': nrm((N_GMLP, 2 * GM_DIM), 0.02),
        'gmlp_vnorm_g': 1.0 + nrm((N_GMLP, GM_DIM), 0.02),
        'gmlp_vnorm_b': nrm((N_GMLP, GM_DIM), 0.02),
        'gmlp_w_s': nrm((N_GMLP, GM_GROUPS, GM_CHUNK, GM_CHUNK), GM_CHUNK ** -0.5),
        'gmlp_b_s': 1.0 + nrm((N_GMLP, GM_GROUPS, GM_CHUNK), 0.1),
        'gmlp_w_out': nrm((N_GMLP, GM_DIM, D_MODEL), GM_DIM ** -0.5 * res),
        'ffn_w_up': nrm((DEPTH, D_MODEL, F2), dn),
        'ffn_conv_w': nrm((DEPTH, CONV_W, F2), 0.3).at[:, CONV_W - 1].add(1.0),
        'ffn_conv_b': nrm((DEPTH, F2), 0.02),
        'ffn_w_down': nrm((DEPTH, FFN_DIM, D_MODEL), FFN_DIM ** -0.5 * res),
    }


def reference(x_prompt, x_sample, state_hgrn_S, state_mlstm_C, state_mlstm_n, state_mlstm_m, state_ffn_conv,
              norm_mix, norm_ffn, norm_final,
              hgrn_w_q, hgrn_w_f, hgrn_w_i, hgrn_w_g, hgrn_lb, hgrn_onorm, hgrn_w_o,
              mlstm_w_q, mlstm_w_k, mlstm_w_v, mlstm_w_og, mlstm_w_if, mlstm_b_if, mlstm_hnorm, mlstm_w_out,
              gmlp_w_in, gmlp_b_in, gmlp_vnorm_g, gmlp_vnorm_b, gmlp_w_s, gmlp_b_s, gmlp_w_out,
              ffn_w_up, ffn_conv_w, ffn_conv_b, ffn_w_down):
    p = dict(norm_mix=norm_mix, norm_ffn=norm_ffn, norm_final=norm_final,
             hgrn_w_q=hgrn_w_q, hgrn_w_f=hgrn_w_f, hgrn_w_i=hgrn_w_i, hgrn_w_g=hgrn_w_g,
             hgrn_lb=hgrn_lb, hgrn_onorm=hgrn_onorm, hgrn_w_o=hgrn_w_o,
             mlstm_w_q=mlstm_w_q, mlstm_w_k=mlstm_w_k, mlstm_w_v=mlstm_w_v, mlstm_w_og=mlstm_w_og,
             mlstm_w_if=mlstm_w_if, mlstm_b_if=mlstm_b_if, mlstm_hnorm=mlstm_hnorm, mlstm_w_out=mlstm_w_out,
             gmlp_w_in=gmlp_w_in, gmlp_b_in=gmlp_b_in, gmlp_vnorm_g=gmlp_vnorm_g, gmlp_vnorm_b=gmlp_vnorm_b,
             gmlp_w_s=gmlp_w_s, gmlp_b_s=gmlp_b_s, gmlp_w_out=gmlp_w_out,
             ffn_w_up=ffn_w_up, ffn_conv_w=ffn_conv_w, ffn_conv_b=ffn_conv_b, ffn_w_down=ffn_w_down)
    f32 = jnp.float32
    bp = x_prompt.shape[0]
    zS = jnp.zeros((N_HGRN, bp, HG_HEADS, HG_DK, HG_DV), f32)
    zC = jnp.zeros((N_MLSTM, bp, ML_HEADS, ML_DQK, ML_DV), f32)
    zn = jnp.zeros((N_MLSTM, bp, ML_HEADS, ML_DQK), f32)
    zm = jnp.zeros((N_MLSTM, bp, ML_HEADS), f32)
    zconv = jnp.zeros((DEPTH, bp, CONV_W - 1, 2 * FFN_DIM), x_prompt.dtype)
    y_prompt, S_p, C_p, n_p, m_p, _, conv_p = _trunk(x_prompt, zS, zC, zn, zm, zconv, p)
    y_sample, S_s, C_s, n_s, m_s, v_s, conv_s = _trunk(x_sample, state_hgrn_S, state_mlstm_C, state_mlstm_n,
                                                       state_mlstm_m, state_ffn_conv, p)
    return (y_prompt, y_sample, S_p, S_s, C_p, C_s, n_p, n_s, m_p, m_s, v_s, conv_p, conv_s)
```

```python
import functools
import math

import numpy as np
import jax
import jax.numpy as jnp
from jax import lax
from jax.experimental import pallas as pl
from jax.experimental.pallas import tpu as pltpu

F32 = jnp.float32
BF16 = jnp.bfloat16

EPS = 1e-6
GATE_CAP = 15.0
N_MIXERS = 3
HGRN_CHUNK = 64
MLSTM_CHUNK = 128
GMLP_CHUNK = 128
SAMPLE_PAD_T = 16
NEG_BIG = -1e30

LANE = 128
VMEM_PHYSICAL_V7X = 64 * 1024 * 1024
VMEM_BUDGET = VMEM_PHYSICAL_V7X - 8 * 1024 * 1024


def _nbytes(shape, dtype):
    return int(np.prod(shape)) * jnp.dtype(dtype).itemsize


def _cparams(semantics, block_bytes, temp_bytes=0):
    need = 2 * block_bytes + temp_bytes + (4 << 20)
    return pltpu.CompilerParams(dimension_semantics=semantics,
                                vmem_limit_bytes=int(min(max(need, 16 << 20), VMEM_BUDGET)))


def _pick_tile(n, target, align):
    best = None
    for d in range(align, min(n, target) + 1, align):
        if n % d == 0:
            best = d
    return best if best is not None else n


def _dot(a, b):
    return jnp.dot(a, b, preferred_element_type=F32)


def _dot_nt(a, b):
    return lax.dot_general(a, b, (((1,), (1,)), ((), ())), preferred_element_type=F32)


def _dot_tn(a, b):
    return lax.dot_general(a, b, (((0,), (0,)), ((), ())), preferred_element_type=F32)


def _log_sigmoid(x):
    return -(jnp.maximum(-x, 0.0) + jnp.log1p(jnp.exp(-jnp.abs(x))))


def _split2(x):
    hi = x.astype(BF16)
    lo = (x - hi.astype(F32)).astype(BF16)
    return hi, lo


def _split3(x):
    hi = x.astype(BF16)
    r = x - hi.astype(F32)
    mid = r.astype(BF16)
    lo = (r - mid.astype(F32)).astype(BF16)
    return hi, mid, lo


def _rmsnorm_kernel(x_ref, g_ref, o_ref):
    x = x_ref[...]
    y = x * lax.rsqrt(jnp.mean(x * x, axis=-1, keepdims=True) + EPS)
    o_ref[...] = (y * g_ref[...]).astype(o_ref.dtype)


def _rmsnorm(x, g, out_dtype, row_start=0, n_rows=None):
    m_all, d = x.shape
    n_rows = m_all - row_start if n_rows is None else n_rows
    tm = _pick_tile(math.gcd(n_rows, row_start) if row_start else n_rows, 544, 16)
    off = row_start // tm
    blk = _nbytes((tm, d), F32) + _nbytes((tm, d), out_dtype)
    return pl.pallas_call(
        _rmsnorm_kernel,
        grid=(n_rows // tm,),
        in_specs=[pl.BlockSpec((tm, d), lambda i: (i + off, 0)),
                  pl.BlockSpec((1, d), lambda i: (0, 0))],
        out_specs=pl.BlockSpec((tm, d), lambda i: (i, 0)),
        out_shape=jax.ShapeDtypeStruct((n_rows, d), out_dtype),
        compiler_params=_cparams(("parallel",), blk, _nbytes((tm, d), F32)),
        name="rmsnorm",
    )(x, g.reshape(1, d).astype(F32))


def _mm_kernel(*refs, epi, n_extra, n_out):
    x_ref, w_ref = refs[0], refs[1]
    extras = refs[2:2 + n_extra]
    outs = refs[2 + n_extra:2 + n_extra + n_out]
    acc = _dot(x_ref[...], w_ref[...])
    vals = epi(acc, *(e[...] for e in extras))
    for o_ref, val in zip(outs, vals):
        o_ref[...] = val.astype(o_ref.dtype)


def _matmul(x, w, epi, extras=(), out_dtypes=(F32,), name="matmul"):
    m, k = x.shape
    n = w.shape[1]
    tn = _pick_tile(n, 512, LANE)
    tm = _pick_tile(m, max(16, (6 << 20) // (2 * k)), 16)
    in_specs = [pl.BlockSpec((tm, k), lambda j, i: (i, 0)),
                pl.BlockSpec((k, tn), lambda j, i: (0, j))]
    args = [x, w]
    blk = _nbytes((tm, k), x.dtype) + _nbytes((k, tn), w.dtype)
    for kind, arr in extras:
        if kind == "row":
            in_specs.append(pl.BlockSpec((1, tn), lambda j, i: (0, j)))
            args.append(arr.reshape(1, n).astype(F32))
            blk += _nbytes((8, tn), F32)
        else:
            in_specs.append(pl.BlockSpec((tm, tn), lambda j, i: (i, j)))
            args.append(arr)
            blk += _nbytes((tm, tn), arr.dtype)
    out_specs = [pl.BlockSpec((tm, tn), lambda j, i: (i, j)) for _ in out_dtypes]
    out_shape = [jax.ShapeDtypeStruct((m, n), dt) for dt in out_dtypes]
    blk += sum(_nbytes((tm, tn), dt) for dt in out_dtypes)
    outs = pl.pallas_call(
        functools.partial(_mm_kernel, epi=epi, n_extra=len(extras), n_out=len(out_dtypes)),
        grid=(n // tn, m // tm),
        in_specs=in_specs, out_specs=out_specs, out_shape=out_shape,
        compiler_params=_cparams(("parallel", "parallel"), blk, 3 * _nbytes((tm, tn), F32)),
        name=name,
    )(*args)
    return outs


def _epi_id(acc):
    return (acc,)


def _epi_silu(acc):
    return (acc * jax.nn.sigmoid(acc),)


def _epi_sigmoid(acc):
    return (jax.nn.sigmoid(acc),)


def _epi_scale(acc, *, scale):
    return (acc * scale,)


def _epi_residual(acc, res):
    return (res + acc,)


def _epi_hgrn_forget(acc, lb):
    lbh = jnp.maximum(lb, 0.0)
    a = jnp.log(lbh)
    c = jnp.log1p(-lbh) + _log_sigmoid(acc)
    logf = jnp.maximum(a, c) + jnp.log1p(jnp.exp(-jnp.abs(a - c)))
    kk = (1.0 - lbh) * jax.nn.sigmoid(-acc)
    return logf, kk


def _epi_bias_gelu(acc, b):
    x = acc + b
    return (0.5 * x * (1.0 + lax.erf(x * (2.0 ** -0.5))),)


def _epi_mlstm_gates(acc, b, *, heads):
    gates = GATE_CAP * jnp.tanh((acc + b) / GATE_CAP)
    col = lax.broadcasted_iota(jnp.int32, acc.shape, 1)
    return (jnp.where(col < heads, gates, _log_sigmoid(gates)),)


def _hgrn_levels(chunk, t_real):
    return tuple(m for m in (2 ** p for p in range(int(math.log2(chunk)) - 1, -1, -1)) if m < t_real)


def _hgrn_sum_matrix(chunk, levels):
    t = np.arange(chunk)[:, None]
    r = np.arange(chunk)[None, :]
    mats = [r <= t, r > t]
    for m in levels:
        mid = (t // (2 * m)) * (2 * m) + m - 1
        second = (t % (2 * m)) >= m
        mats.append(np.where(second, (r > mid) & (r <= t), (r > t) & (r <= mid)))
    return np.concatenate(mats, axis=0).astype(np.float32)


def _hgrn_kernel(q_ref, lf_ref, k_ref, v_ref, g_ref, onorm_ref, d_ref, s0_ref,
                 y_ref, s_ref, st_scr, o_scr, *, heads, chunk, levels, dk):
    c = pl.program_id(1)

    @pl.when(c == 0)
    def _():
        for h in range(heads):
            st_scr[h] = s0_ref[0, h].T

    row = lax.broadcasted_iota(jnp.int32, (chunk, 1), 0)
    r2 = lax.broadcasted_iota(jnp.int32, (chunk, chunk), 0)
    c2 = lax.broadcasted_iota(jnp.int32, (chunk, chunk), 1)
    dmat = d_ref[...]
    for h in range(heads):
        cols = slice(h * dk, (h + 1) * dk)
        lf_hi, lf_lo = _split2(lf_ref[:, cols])
        xs = _dot(dmat, lf_hi) + _dot(dmat, lf_lo)
        q = q_ref[:, cols]
        k = k_ref[:, cols]
        vb = v_ref[:, cols].astype(BF16)
        st = st_scr[h]
        ex_b = jnp.exp(xs[0:chunk])
        ex_rev = jnp.exp(xs[chunk:2 * chunk])
        o = _dot_nt((q * ex_b).astype(BF16), st.astype(BF16))
        a = jnp.where(r2 == c2, _dot_nt(q.astype(BF16), k.astype(BF16)), 0.0)
        for li, m in enumerate(levels):
            ex = jnp.exp(xs[(2 + li) * chunk:(3 + li) * chunk])
            second = (row & m) != 0
            qs = jnp.where(second, q * ex, 0.0).astype(BF16)
            ks = jnp.where(second, 0.0, k * ex).astype(BF16)
            al = _dot_nt(qs, ks)
            if 2 * m < chunk:
                shift = int(math.log2(2 * m))
                al = jnp.where((r2 >> shift) == (c2 >> shift), al, 0.0)
            a = a + al
        o = o + _dot(a.astype(BF16), vb)
        st_scr[h] = st * ex_b[chunk - 1:chunk, :] + _dot_tn(vb, (k * ex_rev).astype(BF16))
        o_scr[:, cols] = o

    o = o_scr[...]
    y = o * lax.rsqrt(jnp.mean(o * o, axis=-1, keepdims=True) + EPS) * onorm_ref[...]
    y_ref[...] = (y * g_ref[...]).astype(y_ref.dtype)

    @pl.when(c == pl.num_programs(1) - 1)
    def _():
        for h in range(heads):
            s_ref[0, h] = st_scr[h].T


def _hgrn_scan(q, lf, k, v, g, onorm, s0, *, batch, seq, chunk, t_real, out_rows, out_dtype):
    d = q.shape[1]
    heads, dk, dv = s0.shape[1], s0.shape[2], s0.shape[3]
    n_chunks = seq // chunk
    levels = _hgrn_levels(chunk, t_real)
    dmat = jnp.asarray(_hgrn_sum_matrix(chunk, levels), dtype=BF16)
    tok = pl.BlockSpec((chunk, d), lambda b, c: (b * n_chunks + c, 0))
    blk = 5 * _nbytes((chunk, d), F32) + _nbytes((chunk, d), out_dtype) + 2 * _nbytes(s0.shape[1:], F32)
    y, s = pl.pallas_call(
        functools.partial(_hgrn_kernel, heads=heads, chunk=chunk, levels=levels, dk=dk),
        grid=(batch, n_chunks),
        in_specs=[tok, tok, tok, tok, tok,
                  pl.BlockSpec((1, d), lambda b, c: (0, 0)),
                  pl.BlockSpec(dmat.shape, lambda b, c: (0, 0)),
                  pl.BlockSpec((1, heads, dk, dv), lambda b, c: (b, 0, 0, 0))],
        out_specs=[tok, pl.BlockSpec((1, heads, dk, dv), lambda b, c: (b, 0, 0, 0))],
        out_shape=[jax.ShapeDtypeStruct((out_rows, d), out_dtype),
                   jax.ShapeDtypeStruct((batch, heads, dk, dv), F32)],
        scratch_shapes=[pltpu.VMEM((heads, dv, dk), F32), pltpu.VMEM((chunk, d), F32)],
        compiler_params=_cparams(("parallel", "arbitrary"), blk,
                                 _nbytes((heads, dv, dk), F32) + 4 * _nbytes((chunk, d), F32)),
        name="hgrn_scan",
    )(q, lf, k, v, g, onorm.reshape(1, d).astype(F32), dmat, s0)
    return y, s


def _mlstm_kernel(q_ref, k_ref, v_ref, og_ref, gt_ref, hn_ref, tri_ref, sel_ref, c0_ref, n0_ref, m0_ref,
                  y_ref, c_ref, n_ref, m_ref, *, heads, chunk, dqk, dv):
    c = pl.program_id(1)

    @pl.when(c == 0)
    def _():
        c_ref[...] = c0_ref[...]
        n_ref[...] = n0_ref[...]
        m_ref[...] = m0_ref[...]

    gates = gt_ref[...]
    tri = tri_ref[...]
    sel = sel_ref[...]
    g3 = _split3(gates)
    cum = sum(_dot(tri, p) for p in g3)
    gates_t = sum(_dot_nt(sel, p) for p in g3)
    cum_t = sum(_dot_nt(sel, p) for p in _split3(cum))
    r2 = lax.broadcasted_iota(jnp.int32, (chunk, chunk), 0)
    c2 = lax.broadcasted_iota(jnp.int32, (chunk, chunk), 1)
    causal = c2 <= r2
    lane = lax.broadcasted_iota(jnp.int32, (1, LANE), 1)
    m_row = m_ref[0]
    m_new = m_row
    for h in range(heads):
        qc = slice(h * dqk, (h + 1) * dqk)
        vc = slice(h * dv, (h + 1) * dv)
        qh = q_ref[:, qc]
        kh = k_ref[:, qc]
        vb = v_ref[:, vc].astype(BF16)
        qb = qh.astype(BF16)
        b_c = cum[:, heads + h:heads + h + 1]
        ig_c = gates[:, h:h + 1]
        b_r = cum_t[heads + h:heads + h + 1, :]
        ig_r = gates_t[h:h + 1, :]
        m_h = m_row[:, h:h + 1]
        dlog = jnp.where(causal, b_c - b_r + ig_r, -jnp.inf)
        inter = b_c + m_h
        mt = jnp.maximum(inter, jnp.max(dlog, axis=-1, keepdims=True))
        wts = jnp.exp(dlog - mt) * _dot_nt(qb, kh.astype(BF16))
        sc = jnp.exp(inter - mt)
        c_h = c_ref[0, h]
        n_h = n_ref[0, h:h + 1, :]
        num = sc * _dot(qb, c_h.astype(BF16)) + _dot(wts.astype(BF16), vb)
        den = sc * jnp.sum(qh * n_h, axis=-1, keepdims=True) + jnp.sum(wts, axis=-1, keepdims=True)
        out = num / jnp.maximum(jnp.abs(den), jnp.exp(-mt))
        m_last = mt[chunk - 1:chunk, :]
        b_last = b_c[chunk - 1:chunk, :]
        sc_state = jnp.exp(b_last + m_h - m_last)
        kw = jnp.exp(b_last - b_c + ig_c - m_last) * kh
        c_ref[0, h] = sc_state * c_h + _dot_tn(kw.astype(BF16), vb)
        n_ref[0, h:h + 1, :] = sc_state * n_h + jnp.sum(kw, axis=0, keepdims=True)
        m_new = jnp.where(lane == h, m_last, m_new)
        y = out * lax.rsqrt(jnp.mean(out * out, axis=-1, keepdims=True) + EPS) * hn_ref[:, vc]
        y_ref[:, vc] = (y * og_ref[:, vc]).astype(y_ref.dtype)
    m_ref[0] = m_new


def _mlstm_scan(q, k, v, og, gates, hnorm, c0, n0, m0, *, batch, seq, chunk, out_rows, out_dtype):
    heads, dqk, dv = c0.shape[1], c0.shape[2], c0.shape[3]
    dq_all, dv_all = q.shape[1], v.shape[1]
    n_chunks = seq // chunk
    tri = jnp.asarray(np.tril(np.ones((chunk, chunk), np.float32)), dtype=BF16)
    sel = jnp.asarray(np.eye(16, LANE, dtype=np.float32), dtype=BF16)
    m0p = jnp.pad(m0, ((0, 0), (0, LANE - heads))).reshape(batch, 1, LANE)
    tq = pl.BlockSpec((chunk, dq_all), lambda b, c: (b * n_chunks + c, 0))
    tv = pl.BlockSpec((chunk, dv_all), lambda b, c: (b * n_chunks + c, 0))
    tg = pl.BlockSpec((chunk, LANE), lambda b, c: (b * n_chunks + c, 0))
    sc_ = pl.BlockSpec((1, heads, dqk, dv), lambda b, c: (b, 0, 0, 0))
    sn_ = pl.BlockSpec((1, heads, dqk), lambda b, c: (b, 0, 0))
    sm_ = pl.BlockSpec((1, 1, LANE), lambda b, c: (b, 0, 0))
    blk = (2 * _nbytes((chunk, dq_all), F32) + 2 * _nbytes((chunk, dv_all), F32) + _nbytes((chunk, dv_all), out_dtype)
           + 2 * _nbytes(c0.shape[1:], F32))
    y, c_out, n_out, m_out = pl.pallas_call(
        functools.partial(_mlstm_kernel, heads=heads, chunk=chunk, dqk=dqk, dv=dv),
        grid=(batch, n_chunks),
        in_specs=[tq, tq, tv, tv, tg,
                  pl.BlockSpec((1, dv_all), lambda b, c: (0, 0)),
                  pl.BlockSpec(tri.shape, lambda b, c: (0, 0)),
                  pl.BlockSpec(sel.shape, lambda b, c: (0, 0)),
                  sc_, sn_, sm_],
        out_specs=[tv, sc_, sn_, sm_],
        out_shape=[jax.ShapeDtypeStruct((out_rows, dv_all), out_dtype),
                   jax.ShapeDtypeStruct(c0.shape, F32),
                   jax.ShapeDtypeStruct(n0.shape, F32),
                   jax.ShapeDtypeStruct((batch, 1, LANE), F32)],
        compiler_params=_cparams(("parallel", "arbitrary"), blk, 8 * _nbytes((chunk, dv_all), F32)),
        name="mlstm_scan",
    )(q, k, v, og, gates, hnorm.reshape(1, dv_all).astype(F32), tri, sel, c0, n0, m0p)
    return y, c_out, n_out, m_out[:, 0, :heads]


def _gmlp_kernel(u_ref, v_ref, vg_ref, vb_ref, w_ref, bs_ref, o_ref, vn_ref, *, groups, gd):
    v = v_ref[...]
    mu = jnp.mean(v, axis=-1, keepdims=True)
    xc = v - mu
    vn = xc * lax.rsqrt(jnp.mean(xc * xc, axis=-1, keepdims=True) + EPS) * vg_ref[...] + vb_ref[...]
    vn_ref[...] = vn
    n = v.shape[0]
    causal = (lax.broadcasted_iota(jnp.int32, (n, n), 1) <= lax.broadcasted_iota(jnp.int32, (n, n), 0))
    bs = bs_ref[0]
    for g in range(groups):
        cols = slice(g * gd, (g + 1) * gd)
        wg = jnp.where(causal, w_ref[0, g], 0.0).astype(BF16)
        mix = _dot(wg, vn[:, cols].astype(BF16)) + bs[:, g:g + 1]
        o_ref[:, cols] = (u_ref[:, cols] * mix).astype(o_ref.dtype)


def _gmlp_gate(z, vg, vb, w_stack, bs_stack, *, n_prompt_chunks):
    m, d2 = z.shape
    d = d2 // 2
    groups, chunk = w_stack.shape[1], w_stack.shape[2]
    gd = d // groups

    def which(i):
        return jnp.minimum(i // n_prompt_chunks, 1)

    blk = (2 * _nbytes((chunk, d), F32) + _nbytes((chunk, d), BF16) + _nbytes((chunk, d), F32)
           + _nbytes((groups, chunk, chunk), F32))
    o, vn = pl.pallas_call(
        functools.partial(_gmlp_kernel, groups=groups, gd=gd),
        grid=(m // chunk,),
        in_specs=[pl.BlockSpec((chunk, d), lambda i: (i, 0)),
                  pl.BlockSpec((chunk, d), lambda i: (i, 1)),
                  pl.BlockSpec((1, d), lambda i: (0, 0)),
                  pl.BlockSpec((1, d), lambda i: (0, 0)),
                  pl.BlockSpec((1, groups, chunk, chunk), lambda i: (which(i), 0, 0, 0)),
                  pl.BlockSpec((1, chunk, groups), lambda i: (which(i), 0, 0))],
        out_specs=[pl.BlockSpec((chunk, d), lambda i: (i, 0)),
                   pl.BlockSpec((chunk, d), lambda i: (i, 0))],
        out_shape=[jax.ShapeDtypeStruct((m, d), BF16), jax.ShapeDtypeStruct((m, d), F32)],
        compiler_params=_cparams(("parallel",), blk, 4 * _nbytes((chunk, d), F32)),
        name="gmlp_gate",
    )(z, z, vg.reshape(1, d).astype(F32), vb.reshape(1, d).astype(F32), w_stack, bs_stack)
    return o, vn


def _conv_prompt_kernel(a_ref, g_ref, ha_ref, hg_ref, cwa_ref, cwg_ref, cba_ref, cbg_ref, o_ref,
                        *, tiles_per_seq, conv_w):
    first = (pl.program_id(0) % tiles_per_seq) == 0
    tm = a_ref.shape[0]
    row = lax.broadcasted_iota(jnp.int32, (tm, 1), 0)

    def conv(up_ref, halo_ref, cw_ref, cb_ref):
        up = up_ref[...]
        y = cb_ref[...] + cw_ref[conv_w - 1:conv_w, :] * up
        for back in range(1, conv_w):
            sh = pltpu.roll(up, back, axis=0)
            for r in range(back):
                hrow = halo_ref[8 - back + r:8 - back + r + 1, :]
                sh = jnp.where(row == r, jnp.where(first, 0.0, hrow), sh)
            y = y + cw_ref[conv_w - 1 - back:conv_w - back, :] * sh
        return y

    ya = conv(a_ref, ha_ref, cwa_ref, cba_ref)
    yg = conv(g_ref, hg_ref, cwg_ref, cbg_ref)
    o_ref[...] = (ya * (yg * jax.nn.sigmoid(yg))).astype(o_ref.dtype)


def _conv_prompt(up, cw, cb, *, rows, seq, out_rows):
    f2 = up.shape[1]
    f = f2 // 2
    conv_w = cw.shape[0]
    tm = _pick_tile(seq, 512, 16)
    tf = _pick_tile(f, 512, LANE)
    nf = f // tf
    hb = tm // 8

    def halo(i, j, off):
        return (jnp.maximum(i * hb - 1, 0), j + off)

    blk = 2 * _nbytes((tm, tf), F32) + _nbytes((tm, tf), BF16) + 2 * _nbytes((8, tf), F32) + 4 * _nbytes((8, tf), F32)
    return pl.pallas_call(
        functools.partial(_conv_prompt_kernel, tiles_per_seq=seq // tm, conv_w=conv_w),
        grid=(rows // tm, nf),
        in_specs=[pl.BlockSpec((tm, tf), lambda i, j: (i, j)),
                  pl.BlockSpec((tm, tf), lambda i, j: (i, j + nf)),
                  pl.BlockSpec((8, tf), lambda i, j: halo(i, j, 0)),
                  pl.BlockSpec((8, tf), lambda i, j: halo(i, j, nf)),
                  pl.BlockSpec((conv_w, tf), lambda i, j: (0, j)),
                  pl.BlockSpec((conv_w, tf), lambda i, j: (0, j + nf)),
                  pl.BlockSpec((1, tf), lambda i, j: (0, j)),
                  pl.BlockSpec((1, tf), lambda i, j: (0, j + nf))],
        out_specs=pl.BlockSpec((tm, tf), lambda i, j: (i, j)),
        out_shape=jax.ShapeDtypeStruct((out_rows, f), BF16),
        compiler_params=_cparams(("parallel", "parallel"), blk, 6 * _nbytes((tm, tf), F32)),
        name="conv_gate_prompt",
    )(up, up, up, up, cw, cw, cb.reshape(1, f2), cb.reshape(1, f2))


def _conv_sample_kernel(a_ref, g_ref, ba_ref, bg_ref, cwa_ref, cwg_ref, cba_ref, cbg_ref, o_ref, *, conv_w):
    steps = a_ref.shape[0]

    def conv(up_ref, buf_ref, cw_ref, cb_ref):
        hp = [buf_ref[j] for j in range(conv_w - 1)] + [up_ref[t] for t in range(steps)]
        ys = []
        for t in range(steps):
            y = cb_ref[...]
            for j in range(conv_w):
                y = y + cw_ref[j:j + 1, :] * hp[t + j]
            ys.append(y)
        return ys

    ya = conv(a_ref, ba_ref, cwa_ref, cba_ref)
    yg = conv(g_ref, bg_ref, cwg_ref, cbg_ref)
    for t in range(steps):
        o_ref[t] = (ya[t] * (yg[t] * jax.nn.sigmoid(yg[t]))).astype(o_ref.dtype)


def _conv_sample(up_t, buf_t, cw, cb):
    steps, b, f2 = up_t.shape
    f = f2 // 2
    conv_w = cw.shape[0]
    tf = _pick_tile(f, 512, LANE)
    nf = f // tf
    blk = (2 * _nbytes((steps, b, tf), F32) + 2 * _nbytes((conv_w - 1, b, tf), F32) + _nbytes((steps, b, tf), BF16)
           + 4 * _nbytes((8, tf), F32))
    return pl.pallas_call(
        functools.partial(_conv_sample_kernel, conv_w=conv_w),
        grid=(nf,),
        in_specs=[pl.BlockSpec((steps, b, tf), lambda j: (0, 0, j)),
                  pl.BlockSpec((steps, b, tf), lambda j: (0, 0, j + nf)),
                  pl.BlockSpec((conv_w - 1, b, tf), lambda j: (0, 0, j)),
                  pl.BlockSpec((conv_w - 1, b, tf), lambda j: (0, 0, j + nf)),
                  pl.BlockSpec((conv_w, tf), lambda j: (0, j)),
                  pl.BlockSpec((conv_w, tf), lambda j: (0, j + nf)),
                  pl.BlockSpec((1, tf), lambda j: (0, j)),
                  pl.BlockSpec((1, tf), lambda j: (0, j + nf))],
        out_specs=pl.BlockSpec((steps, b, tf), lambda j: (0, 0, j)),
        out_shape=jax.ShapeDtypeStruct((steps, b, f), BF16),
        compiler_params=_cparams(("parallel",), blk, 8 * _nbytes((steps, b, tf), F32)),
        name="conv_gate_sample",
    )(up_t, up_t, buf_t, buf_t, cw, cw, cb.reshape(1, f2), cb.reshape(1, f2))


def _pad_sample(a, mp, bs, ts, fill=0.0):
    n = a.shape[1]
    s = a[mp:].reshape(bs, ts, n)
    s = jnp.pad(s, ((0, 0), (0, SAMPLE_PAD_T - ts), (0, 0)), constant_values=fill)
    return s.reshape(bs * SAMPLE_PAD_T, n)


def _merge_sample(y_full, y_pad, mp, bs, ts):
    n = y_pad.shape[1]
    ys = y_pad.reshape(bs, SAMPLE_PAD_T, n)[:, :ts].reshape(bs * ts, n).astype(y_full.dtype)
    return lax.dynamic_update_slice(y_full, ys, (mp, 0))


def _forward(x_prompt, x_sample, state_hgrn_S, state_mlstm_C, state_mlstm_n, state_mlstm_m, state_ffn_conv,
             norm_mix, norm_ffn, norm_final,
             hgrn_w_q, hgrn_w_f, hgrn_w_i, hgrn_w_g, hgrn_lb, hgrn_onorm, hgrn_w_o,
             mlstm_w_q, mlstm_w_k, mlstm_w_v, mlstm_w_og, mlstm_w_if, mlstm_b_if, mlstm_hnorm, mlstm_w_out,
             gmlp_w_in, gmlp_b_in, gmlp_vnorm_g, gmlp_vnorm_b, gmlp_w_s, gmlp_b_s, gmlp_w_out,
             ffn_w_up, ffn_conv_w, ffn_conv_b, ffn_w_down):
    bp, tp, d = x_prompt.shape
    bs, ts, _ = x_sample.shape
    mp, ms = bp * tp, bs * ts
    m = mp + ms
    depth = norm_mix.shape[0]
    hg_heads, hg_dk, hg_dv = state_hgrn_S.shape[2:]
    ml_heads, ml_dqk, ml_dv = state_mlstm_C.shape[2:]
    conv_w = ffn_conv_w.shape[1]
    f2 = ffn_w_up.shape[2]
    bf = lambda w: w.astype(BF16)

    x = jnp.concatenate([x_prompt.reshape(mp, d), x_sample.reshape(ms, d)], axis=0)

    lb = jax.nn.softmax(hgrn_lb.astype(F32), axis=0)
    lbs = jnp.cumsum(lb, axis=0) - lb[0]

    out_s, out_c, out_n, out_m, out_v, out_conv = [], [], [], [], [], []
    for i in range(depth):
        j = i // N_MIXERS
        kind = i % N_MIXERS
        h = _rmsnorm(x, norm_mix[i], BF16)
        if kind == 0:
            (q,) = _matmul(h, bf(hgrn_w_q[j]), _epi_silu, name="hgrn_q")
            lf, k = _matmul(h, bf(hgrn_w_f[j]), _epi_hgrn_forget, extras=[("row", lbs[j])],
                            out_dtypes=(F32, F32), name="hgrn_f")
            (v,) = _matmul(h, bf(hgrn_w_i[j]), _epi_id, name="hgrn_i")
            (g,) = _matmul(h, bf(hgrn_w_g[j]), _epi_sigmoid, name="hgrn_g")
            chunk = min(HGRN_CHUNK, tp)
            y, s_p = _hgrn_scan(q, lf, k, v, g, hgrn_onorm[j],
                                jnp.zeros((bp, hg_heads, hg_dk, hg_dv), F32),
                                batch=bp, seq=tp, chunk=chunk, t_real=tp, out_rows=m, out_dtype=BF16)
            pads = [_pad_sample(a, mp, bs, ts) for a in (q, lf, k, v, g)]
            y_s, s_s = _hgrn_scan(*pads, hgrn_onorm[j], state_hgrn_S[j].astype(F32),
                                  batch=bs, seq=SAMPLE_PAD_T, chunk=SAMPLE_PAD_T, t_real=ts,
                                  out_rows=bs * SAMPLE_PAD_T, out_dtype=BF16)
            y = _merge_sample(y, y_s, mp, bs, ts)
            out_s.append((s_p, s_s))
            (x,) = _matmul(y, bf(hgrn_w_o[j]), _epi_residual, extras=[("tile", x)], name="hgrn_o")
        elif kind == 1:
            (q,) = _matmul(h, bf(mlstm_w_q[j]), _epi_id, name="mlstm_q")
            (k,) = _matmul(h, bf(mlstm_w_k[j]), functools.partial(_epi_scale, scale=ml_dqk ** -0.5), name="mlstm_k")
            (v,) = _matmul(h, bf(mlstm_w_v[j]), _epi_id, name="mlstm_v")
            (og,) = _matmul(h, bf(mlstm_w_og[j]), _epi_sigmoid, name="mlstm_og")
            w_if = jnp.pad(mlstm_w_if[j], ((0, 0), (0, LANE - 2 * ml_heads)))
            b_if = jnp.pad(mlstm_b_if[j], (0, LANE - 2 * ml_heads))
            (gates,) = _matmul(h, bf(w_if), functools.partial(_epi_mlstm_gates, heads=ml_heads),
                               extras=[("row", b_if)], name="mlstm_if")
            chunk = min(MLSTM_CHUNK, tp)
            y, c_p, n_p, m_p = _mlstm_scan(
                q, k, v, og, gates, mlstm_hnorm[j],
                jnp.zeros((bp, ml_heads, ml_dqk, ml_dv), F32), jnp.zeros((bp, ml_heads, ml_dqk), F32),
                jnp.zeros((bp, ml_heads), F32),
                batch=bp, seq=tp, chunk=chunk, out_rows=m, out_dtype=BF16)
            pads = [_pad_sample(a, mp, bs, ts) for a in (q, k, v, og)]
            gate_fill = jnp.where(jnp.arange(LANE) < ml_heads, NEG_BIG, 0.0).astype(F32)
            gs = gates[mp:].reshape(bs, ts, LANE)
            gs = jnp.concatenate([gs, jnp.broadcast_to(gate_fill, (bs, SAMPLE_PAD_T - ts, LANE))], axis=1)
            y_s, c_s, n_s, m_s = _mlstm_scan(
                *pads, gs.reshape(bs * SAMPLE_PAD_T, LANE), mlstm_hnorm[j],
                state_mlstm_C[j].astype(F32), state_mlstm_n[j].astype(F32), state_mlstm_m[j].astype(F32),
                batch=bs, seq=SAMPLE_PAD_T, chunk=SAMPLE_PAD_T, out_rows=bs * SAMPLE_PAD_T, out_dtype=BF16)
            y = _merge_sample(y, y_s, mp, bs, ts)
            out_c.append((c_p, c_s))
            out_n.append((n_p, n_s))
            out_m.append((m_p, m_s))
            (x,) = _matmul(y, bf(mlstm_w_out[j]), _epi_residual, extras=[("tile", x)], name="mlstm_out")
        else:
            (z,) = _matmul(h, bf(gmlp_w_in[j]), _epi_bias_gelu, extras=[("row", gmlp_b_in[j])], name="gmlp_in")
            groups = gmlp_w_s.shape[1]
            lp = min(GMLP_CHUNK, tp)
            ls = min(GMLP_CHUNK, ts)
            reps = GMLP_CHUNK // ls
            w_s_blk = jnp.einsum("ab,gts->gatbs", jnp.eye(reps, dtype=F32), gmlp_w_s[j][:, :ls, :ls])
            w_s_blk = w_s_blk.reshape(groups, GMLP_CHUNK, GMLP_CHUNK)
            w_p = gmlp_w_s[j][:, :lp, :lp]
            w_stack = jnp.stack([w_p, w_s_blk])
            bs_stack = jnp.stack([gmlp_b_s[j][:, :lp].T, jnp.tile(gmlp_b_s[j][:, :ls].T, (reps, 1))])
            y, vn = _gmlp_gate(z, gmlp_vnorm_g[j], gmlp_vnorm_b[j], w_stack, bs_stack,
                               n_prompt_chunks=mp // GMLP_CHUNK)
            out_v.append(vn[mp:].reshape(bs, ts, d))
            (x,) = _matmul(y, bf(gmlp_w_out[j]), _epi_residual, extras=[("tile", x)], name="gmlp_out")

        h = _rmsnorm(x, norm_ffn[i], BF16)
        (up,) = _matmul(h, bf(ffn_w_up[i]), _epi_id, name="ffn_up")
        act = _conv_prompt(up, ffn_conv_w[i], ffn_conv_b[i], rows=mp, seq=tp, out_rows=m)
        up_s = up[mp:].reshape(bs, ts, f2)
        act_s = _conv_sample(up_s.transpose(1, 0, 2), state_ffn_conv[i].astype(F32).transpose(1, 0, 2),
                             ffn_conv_w[i], ffn_conv_b[i])
        act = lax.dynamic_update_slice(act, act_s.transpose(1, 0, 2).reshape(ms, f2 // 2), (mp, 0))
        tail_p = up[mp - tp * bp:mp].reshape(bp, tp, f2)[:, max(tp - (conv_w - 1), 0):]
        tail_s = up_s[:, max(ts - (conv_w - 1), 0):]
        if tp < conv_w - 1:
            tail_p = jnp.concatenate([jnp.zeros((bp, conv_w - 1 - tp, f2), F32), tail_p], axis=1)
        if ts < conv_w - 1:
            tail_s = jnp.concatenate([state_ffn_conv[i].astype(F32)[:, ts:], tail_s], axis=1)
        out_conv.append((tail_p, tail_s))
        (x,) = _matmul(act, bf(ffn_w_down[i]), _epi_residual, extras=[("tile", x)], name="ffn_down")

    y_p = _rmsnorm(x, norm_final, F32, row_start=0, n_rows=mp).reshape(bp, tp, d)
    y_s = _rmsnorm(x, norm_final, F32, row_start=mp, n_rows=ms).reshape(bs, ts, d)
    stack = lambda pairs, idx: jnp.stack([p[idx] for p in pairs])
    return (y_p, y_s,
            stack(out_s, 0), stack(out_s, 1), stack(out_c, 0), stack(out_c, 1),
            stack(out_n, 0), stack(out_n, 1), stack(out_m, 0), stack(out_m, 1),
            jnp.stack(out_v), stack(out_conv, 0), stack(out_conv, 1))


_forward_jit = jax.jit(_forward)


def kernel(x_prompt, x_sample, state_hgrn_S, state_mlstm_C, state_mlstm_n, state_mlstm_m, state_ffn_conv, norm_mix, norm_ffn, norm_final, hgrn_w_q, hgrn_w_f, hgrn_w_i, hgrn_w_g, hgrn_lb, hgrn_onorm, hgrn_w_o, mlstm_w_q, mlstm_w_k, mlstm_w_v, mlstm_w_og, mlstm_w_if, mlstm_b_if, mlstm_hnorm, mlstm_w_out, gmlp_w_in, gmlp_b_in, gmlp_vnorm_g, gmlp_vnorm_b, gmlp_w_s, gmlp_b_s, gmlp_w_out, ffn_w_up, ffn_conv_w, ffn_conv_b, ffn_w_down):
    return _forward_jit(x_prompt, x_sample, state_hgrn_S, state_mlstm_C, state_mlstm_n, state_mlstm_m, state_ffn_conv,
                        norm_mix, norm_ffn, norm_final,
                        hgrn_w_q, hgrn_w_f, hgrn_w_i, hgrn_w_g, hgrn_lb, hgrn_onorm, hgrn_w_o,
                        mlstm_w_q, mlstm_w_k, mlstm_w_v, mlstm_w_og, mlstm_w_if, mlstm_b_if, mlstm_hnorm, mlstm_w_out,
                        gmlp_w_in, gmlp_b_in, gmlp_vnorm_g, gmlp_vnorm_b, gmlp_w_s, gmlp_b_s, gmlp_w_out,
                        ffn_w_up, ffn_conv_w, ffn_conv_b, ffn_w_down)
```

```python
import functools
import math

import numpy as np
import jax
import jax.numpy as jnp
from jax import lax
from jax.experimental import pallas as pl
from jax.experimental.pallas import tpu as pltpu

F32 = jnp.float32
BF16 = jnp.bfloat16

EPS = 1e-6
GATE_CAP = 15.0
N_MIXERS = 3
HGRN_CHUNK = 64
MLSTM_CHUNK = 128
GMLP_CHUNK = 128
SAMPLE_PAD_T = 16
NEG_BIG = -1e30

LANE = 128
VMEM_PHYSICAL_V7X = 64 * 1024 * 1024
VMEM_BUDGET = VMEM_PHYSICAL_V7X - 8 * 1024 * 1024


def _nbytes(shape, dtype):
    return int(np.prod(shape)) * jnp.dtype(dtype).itemsize


def _cparams(semantics, block_bytes, temp_bytes=0):
    need = 2 * block_bytes + temp_bytes + (4 << 20)
    return pltpu.CompilerParams(dimension_semantics=semantics,
                                vmem_limit_bytes=int(min(max(need, 16 << 20), VMEM_BUDGET)))


def _pick_tile(n, target, align):
    best = None
    for d in range(align, min(n, target) + 1, align):
        if n % d == 0:
            best = d
    return best if best is not None else n


def _dot(a, b):
    return jnp.dot(a, b, preferred_element_type=F32)


def _dot_nt(a, b):
    return lax.dot_general(a, b, (((1,), (1,)), ((), ())), preferred_element_type=F32)


def _dot_tn(a, b):
    return lax.dot_general(a, b, (((0,), (0,)), ((), ())), preferred_element_type=F32)


def _log_sigmoid(x):
    return -(jnp.maximum(-x, 0.0) + jnp.log1p(jnp.exp(-jnp.abs(x))))


def _split2(x):
    hi = x.astype(BF16)
    lo = (x - hi.astype(F32)).astype(BF16)
    return hi, lo


def _split3(x):
    hi = x.astype(BF16)
    r = x - hi.astype(F32)
    mid = r.astype(BF16)
    lo = (r - mid.astype(F32)).astype(BF16)
    return hi, mid, lo


def _rmsnorm_kernel(x_ref, g_ref, o_ref):
    x = x_ref[...]
    y = x * lax.rsqrt(jnp.mean(x * x, axis=-1, keepdims=True) + EPS)
    o_ref[...] = (y * g_ref[...]).astype(o_ref.dtype)


def _rmsnorm(x, g, out_dtype, row_start=0, n_rows=None):
    m_all, d = x.shape
    n_rows = m_all - row_start if n_rows is None else n_rows
    tm = _pick_tile(math.gcd(n_rows, row_start) if row_start else n_rows, 544, 16)
    off = row_start // tm
    blk = _nbytes((tm, d), F32) + _nbytes((tm, d), out_dtype)
    return pl.pallas_call(
        _rmsnorm_kernel,
        grid=(n_rows // tm,),
        in_specs=[pl.BlockSpec((tm, d), lambda i: (i + off, 0)),
                  pl.BlockSpec((1, d), lambda i: (0, 0))],
        out_specs=pl.BlockSpec((tm, d), lambda i: (i, 0)),
        out_shape=jax.ShapeDtypeStruct((n_rows, d), out_dtype),
        compiler_params=_cparams(("parallel",), blk, _nbytes((tm, d), F32)),
        name="rmsnorm",
    )(x, g.reshape(1, d).astype(F32))


def _mm_kernel(*refs, epi, n_extra, n_out):
    x_ref, w_ref = refs[0], refs[1]
    extras = refs[2:2 + n_extra]
    outs = refs[2 + n_extra:2 + n_extra + n_out]
    wb_scr = refs[2 + n_extra + n_out]

    @pl.when(pl.program_id(1) == 0)
    def _():
        wb_scr[...] = w_ref[...].astype(BF16)

    acc = _dot(x_ref[...], wb_scr[...])
    vals = epi(acc, *(e[...] for e in extras))
    for o_ref, val in zip(outs, vals):
        o_ref[...] = val.astype(o_ref.dtype)


def _matmul(x, w, epi, extras=(), out_dtypes=(F32,), name="matmul"):
    m, k = x.shape
    n = w.shape[1]
    tile_bytes = 6 << 20
    tn_cap = tile_bytes // (jnp.dtype(w.dtype).itemsize * k) // LANE * LANE
    tn = _pick_tile(n, max(LANE, min(512, tn_cap)), LANE)
    tm = _pick_tile(m, max(16, tile_bytes // (jnp.dtype(x.dtype).itemsize * k)), 16)
    in_specs = [pl.BlockSpec((tm, k), lambda j, i: (i, 0)),
                pl.BlockSpec((k, tn), lambda j, i: (0, j))]
    args = [x, w]
    blk = _nbytes((tm, k), x.dtype) + _nbytes((k, tn), w.dtype)
    for kind, arr in extras:
        if kind == "row":
            in_specs.append(pl.BlockSpec((1, tn), lambda j, i: (0, j)))
            args.append(arr.reshape(1, n).astype(F32))
            blk += _nbytes((8, tn), F32)
        else:
            in_specs.append(pl.BlockSpec((tm, tn), lambda j, i: (i, j)))
            args.append(arr)
            blk += _nbytes((tm, tn), arr.dtype)
    out_specs = [pl.BlockSpec((tm, tn), lambda j, i: (i, j)) for _ in out_dtypes]
    out_shape = [jax.ShapeDtypeStruct((m, n), dt) for dt in out_dtypes]
    blk += sum(_nbytes((tm, tn), dt) for dt in out_dtypes)
    outs = pl.pallas_call(
        functools.partial(_mm_kernel, epi=epi, n_extra=len(extras), n_out=len(out_dtypes)),
        grid=(n // tn, m // tm),
        in_specs=in_specs, out_specs=out_specs, out_shape=out_shape,
        scratch_shapes=[pltpu.VMEM((k, tn), BF16)],
        compiler_params=_cparams(("parallel", "arbitrary"), blk,
                                 _nbytes((k, tn), BF16) + 3 * _nbytes((tm, tn), F32)),
        name=name,
    )(*args)
    return outs


def _epi_id(acc):
    return (acc,)


def _epi_silu(acc):
    return (acc * jax.nn.sigmoid(acc),)


def _epi_sigmoid(acc):
    return (jax.nn.sigmoid(acc),)


def _epi_scale(acc, *, scale):
    return (acc * scale,)


def _epi_residual(acc, res):
    return (res + acc,)


def _epi_hgrn_forget(acc, lb):
    lbh = jnp.maximum(lb, 0.0)
    a = jnp.log(lbh)
    c = jnp.log1p(-lbh) + _log_sigmoid(acc)
    logf = jnp.maximum(a, c) + jnp.log1p(jnp.exp(-jnp.abs(a - c)))
    kk = (1.0 - lbh) * jax.nn.sigmoid(-acc)
    return logf, kk


def _epi_bias_gelu(acc, b):
    x = acc + b
    return (0.5 * x * (1.0 + lax.erf(x * (2.0 ** -0.5))),)


def _epi_mlstm_gates(acc, b, *, heads):
    gates = GATE_CAP * jnp.tanh((acc + b) / GATE_CAP)
    col = lax.broadcasted_iota(jnp.int32, acc.shape, 1)
    return (jnp.where(col < heads, gates, _log_sigmoid(gates)),)


def _hgrn_levels(chunk, t_real):
    return tuple(m for m in (2 ** p for p in range(int(math.log2(chunk)) - 1, -1, -1)) if m < t_real)


def _hgrn_sum_matrix(chunk, levels):
    t = np.arange(chunk)[:, None]
    r = np.arange(chunk)[None, :]
    mats = [r <= t, r > t]
    for m in levels:
        mid = (t // (2 * m)) * (2 * m) + m - 1
        second = (t % (2 * m)) >= m
        mats.append(np.where(second, (r > mid) & (r <= t), (r > t) & (r <= mid)))
    return np.concatenate(mats, axis=0).astype(np.float32)


def _hgrn_kernel(*refs, heads, chunk, levels, dk, aliased):
    q_ref, lf_ref, k_ref, v_ref, g_ref, onorm_ref, d_ref, s0_ref = refs[:8]
    y_ref, s_ref, st_scr, o_scr = refs[8 + aliased:]
    c = pl.program_id(1)

    @pl.when(c == 0)
    def _():
        for h in range(heads):
            st_scr[h] = s0_ref[0, 0, h].T

    row = lax.broadcasted_iota(jnp.int32, (chunk, 1), 0)
    r2 = lax.broadcasted_iota(jnp.int32, (chunk, chunk), 0)
    c2 = lax.broadcasted_iota(jnp.int32, (chunk, chunk), 1)

    lf_hi, lf_lo = _split2(lf_ref[...])
    dmat = d_ref[...]
    xs = _dot(dmat, lf_hi) + _dot(dmat, lf_lo)
    q = q_ref[...]
    k = k_ref[...]
    vb = v_ref[...].astype(BF16)
    ex_b = jnp.exp(xs[0:chunk])
    q_in = (q * ex_b).astype(BF16)
    k_out = (k * jnp.exp(xs[chunk:2 * chunk])).astype(BF16)
    g_last = ex_b[chunk - 1:chunk, :]
    qs = [q.astype(BF16)]
    ks = [k.astype(BF16)]
    masks = [r2 == c2]
    for li, m in enumerate(levels):
        ex = jnp.exp(xs[(2 + li) * chunk:(3 + li) * chunk])
        second = (row & m) != 0
        qs.append(jnp.where(second, q * ex, 0.0).astype(BF16))
        ks.append(jnp.where(second, 0.0, k * ex).astype(BF16))
        shift = int(math.log2(2 * m))
        masks.append(None if 2 * m == chunk else (r2 >> shift) == (c2 >> shift))

    scores = []
    for h in range(heads):
        cols = slice(h * dk, (h + 1) * dk)
        a = None
        for qs_l, ks_l, mask in zip(qs, ks, masks):
            al = _dot_nt(qs_l[:, cols], ks_l[:, cols])
            if mask is not None:
                al = jnp.where(mask, al, 0.0)
            a = al if a is None else a + al
        scores.append(a.astype(BF16))

    for h in range(heads):
        cols = slice(h * dk, (h + 1) * dk)
        st = st_scr[h]
        o_scr[:, cols] = _dot_nt(q_in[:, cols], st.astype(BF16)) + _dot(scores[h], vb[:, cols])
        st_scr[h] = st * g_last[:, cols] + _dot_tn(vb[:, cols], k_out[:, cols])

    o = o_scr[...]
    y = o * lax.rsqrt(jnp.mean(o * o, axis=-1, keepdims=True) + EPS) * onorm_ref[...]
    y_ref[...] = (y * g_ref[...]).astype(y_ref.dtype)

    @pl.when(c == pl.num_programs(1) - 1)
    def _():
        for h in range(heads):
            s_ref[0, 0, h] = st_scr[h].T


def _hgrn_scan(q, lf, k, v, g, onorm, s0_all, layer, s_out_prev, *, batch, seq, chunk, t_real, out_rows, out_dtype):
    d = q.shape[1]
    heads, dk, dv = s0_all.shape[2:]
    n_chunks = seq // chunk
    levels = _hgrn_levels(chunk, t_real)
    dmat = jnp.asarray(_hgrn_sum_matrix(chunk, levels), dtype=BF16)
    tok = pl.BlockSpec((chunk, d), lambda b, c: (b * n_chunks + c, 0))
    in_layer = min(layer, s0_all.shape[0] - 1)
    st_in = pl.BlockSpec((1, 1, heads, dk, dv), lambda b, c: (in_layer, b, 0, 0, 0))
    st_out = pl.BlockSpec((1, 1, heads, dk, dv), lambda b, c: (layer, b, 0, 0, 0))
    aliased = not isinstance(s_out_prev, int)
    n_layers = s_out_prev.shape[0] if aliased else s_out_prev
    in_specs = [tok, tok, tok, tok, tok,
                pl.BlockSpec((1, d), lambda b, c: (0, 0)),
                pl.BlockSpec(dmat.shape, lambda b, c: (0, 0)),
                st_in]
    args = [q, lf, k, v, g, onorm.reshape(1, d).astype(F32), dmat, s0_all]
    if aliased:
        in_specs.append(pl.BlockSpec(memory_space=pl.ANY))
        args.append(s_out_prev)
    blk = (5 * _nbytes((chunk, d), F32) + _nbytes((chunk, d), out_dtype) + 2 * _nbytes((heads, dk, dv), F32)
           + _nbytes(dmat.shape, BF16))
    n_exp = 2 + len(levels)
    y, s = pl.pallas_call(
        functools.partial(_hgrn_kernel, heads=heads, chunk=chunk, levels=levels, dk=dk, aliased=int(aliased)),
        grid=(batch, n_chunks),
        in_specs=in_specs,
        out_specs=[tok, st_out],
        out_shape=[jax.ShapeDtypeStruct((out_rows, d), out_dtype),
                   jax.ShapeDtypeStruct((n_layers, batch, heads, dk, dv), F32)],
        scratch_shapes=[pltpu.VMEM((heads, dv, dk), F32), pltpu.VMEM((chunk, d), F32)],
        input_output_aliases={len(args) - 1: 1} if aliased else {},
        compiler_params=_cparams(("parallel", "arbitrary"), blk,
                                 _nbytes((heads, dv, dk), F32) + (3 * n_exp + 8) * _nbytes((chunk, d), F32)),
        name="hgrn_scan",
    )(*args)
    return y, s


def _mlstm_kernel(q_ref, k_ref, v_ref, og_ref, gt_ref, hn_ref, tri_ref, sel_ref, c0_ref, n0_ref, m0_ref,
                  y_ref, c_ref, n_ref, m_ref, *, heads, chunk, dqk, dv):
    c = pl.program_id(1)

    @pl.when(c == 0)
    def _():
        c_ref[...] = c0_ref[...]
        n_ref[...] = n0_ref[...]
        m_ref[...] = m0_ref[...]

    gates = gt_ref[...]
    tri = tri_ref[...]
    sel = sel_ref[...]
    g3 = _split3(gates)
    cum = sum(_dot(tri, p) for p in g3)
    gates_t = sum(_dot_nt(sel, p) for p in g3)
    cum_t = sum(_dot_nt(sel, p) for p in _split3(cum))
    r2 = lax.broadcasted_iota(jnp.int32, (chunk, chunk), 0)
    c2 = lax.broadcasted_iota(jnp.int32, (chunk, chunk), 1)
    causal = c2 <= r2
    lane = lax.broadcasted_iota(jnp.int32, (1, LANE), 1)
    m_row = m_ref[0]
    m_new = m_row
    for h in range(heads):
        qc = slice(h * dqk, (h + 1) * dqk)
        vc = slice(h * dv, (h + 1) * dv)
        qh = q_ref[:, qc]
        kh = k_ref[:, qc]
        vb = v_ref[:, vc].astype(BF16)
        qb = qh.astype(BF16)
        b_c = cum[:, heads + h:heads + h + 1]
        ig_c = gates[:, h:h + 1]
        b_r = cum_t[heads + h:heads + h + 1, :]
        ig_r = gates_t[h:h + 1, :]
        m_h = m_row[:, h:h + 1]
        dlog = jnp.where(causal, b_c - b_r + ig_r, -jnp.inf)
        inter = b_c + m_h
        mt = jnp.maximum(inter, jnp.max(dlog, axis=-1, keepdims=True))
        wts = jnp.exp(dlog - mt) * _dot_nt(qb, kh.astype(BF16))
        sc = jnp.exp(inter - mt)
        c_h = c_ref[0, h]
        n_h = n_ref[0, h:h + 1, :]
        num = sc * _dot(qb, c_h.astype(BF16)) + _dot(wts.astype(BF16), vb)
        den = sc * jnp.sum(qh * n_h, axis=-1, keepdims=True) + jnp.sum(wts, axis=-1, keepdims=True)
        out = num / jnp.maximum(jnp.abs(den), jnp.exp(-mt))
        m_last = mt[chunk - 1:chunk, :]
        b_last = b_c[chunk - 1:chunk, :]
        sc_state = jnp.exp(b_last + m_h - m_last)
        kw = jnp.exp(b_last - b_c + ig_c - m_last) * kh
        c_ref[0, h] = sc_state * c_h + _dot_tn(kw.astype(BF16), vb)
        n_ref[0, h:h + 1, :] = sc_state * n_h + jnp.sum(kw, axis=0, keepdims=True)
        m_new = jnp.where(lane == h, m_last, m_new)
        y = out * lax.rsqrt(jnp.mean(out * out, axis=-1, keepdims=True) + EPS) * hn_ref[:, vc]
        y_ref[:, vc] = (y * og_ref[:, vc]).astype(y_ref.dtype)
    m_ref[0] = m_new


def _mlstm_scan(q, k, v, og, gates, hnorm, c0, n0, m0, *, batch, seq, chunk, out_rows, out_dtype):
    heads, dqk, dv = c0.shape[1], c0.shape[2], c0.shape[3]
    dq_all, dv_all = q.shape[1], v.shape[1]
    n_chunks = seq // chunk
    assert 2 * heads <= 16
    tri = jnp.asarray(np.tril(np.ones((chunk, chunk), np.float32)), dtype=BF16)
    sel = jnp.asarray(np.eye(16, LANE, dtype=np.float32), dtype=BF16)
    m0p = jnp.pad(m0, ((0, 0), (0, LANE - heads))).reshape(batch, 1, LANE)
    tq = pl.BlockSpec((chunk, dq_all), lambda b, c: (b * n_chunks + c, 0))
    tv = pl.BlockSpec((chunk, dv_all), lambda b, c: (b * n_chunks + c, 0))
    tg = pl.BlockSpec((chunk, LANE), lambda b, c: (b * n_chunks + c, 0))
    sc_ = pl.BlockSpec((1, heads, dqk, dv), lambda b, c: (b, 0, 0, 0))
    sn_ = pl.BlockSpec((1, heads, dqk), lambda b, c: (b, 0, 0))
    sm_ = pl.BlockSpec((1, 1, LANE), lambda b, c: (b, 0, 0))
    blk = (2 * _nbytes((chunk, dq_all), F32) + 2 * _nbytes((chunk, dv_all), F32) + _nbytes((chunk, dv_all), out_dtype)
           + 2 * _nbytes(c0.shape[1:], F32))
    y, c_out, n_out, m_out = pl.pallas_call(
        functools.partial(_mlstm_kernel, heads=heads, chunk=chunk, dqk=dqk, dv=dv),
        grid=(batch, n_chunks),
        in_specs=[tq, tq, tv, tv, tg,
                  pl.BlockSpec((1, dv_all), lambda b, c: (0, 0)),
                  pl.BlockSpec(tri.shape, lambda b, c: (0, 0)),
                  pl.BlockSpec(sel.shape, lambda b, c: (0, 0)),
                  sc_, sn_, sm_],
        out_specs=[tv, sc_, sn_, sm_],
        out_shape=[jax.ShapeDtypeStruct((out_rows, dv_all), out_dtype),
                   jax.ShapeDtypeStruct(c0.shape, F32),
                   jax.ShapeDtypeStruct(n0.shape, F32),
                   jax.ShapeDtypeStruct((batch, 1, LANE), F32)],
        compiler_params=_cparams(("parallel", "arbitrary"), blk, 8 * _nbytes((chunk, dv_all), F32)),
        name="mlstm_scan",
    )(q, k, v, og, gates, hnorm.reshape(1, dv_all).astype(F32), tri, sel, c0, n0, m0p)
    return y, c_out, n_out, m_out[:, 0, :heads]


def _gmlp_kernel(u_ref, v_ref, vg_ref, vb_ref, w_ref, bs_ref, o_ref, vn_ref, *, groups, gd):
    v = v_ref[...]
    mu = jnp.mean(v, axis=-1, keepdims=True)
    xc = v - mu
    vn = xc * lax.rsqrt(jnp.mean(xc * xc, axis=-1, keepdims=True) + EPS) * vg_ref[...] + vb_ref[...]
    vn_ref[...] = vn
    n = v.shape[0]
    causal = (lax.broadcasted_iota(jnp.int32, (n, n), 1) <= lax.broadcasted_iota(jnp.int32, (n, n), 0))
    bs = bs_ref[0]
    for g in range(groups):
        cols = slice(g * gd, (g + 1) * gd)
        wg = jnp.where(causal, w_ref[0, g], 0.0).astype(BF16)
        mix = _dot(wg, vn[:, cols].astype(BF16)) + bs[:, g:g + 1]
        o_ref[:, cols] = (u_ref[:, cols] * mix).astype(o_ref.dtype)


def _gmlp_gate(z, vg, vb, w_stack, bs_stack, *, n_prompt_chunks):
    m, d2 = z.shape
    d = d2 // 2
    groups, chunk = w_stack.shape[1], w_stack.shape[2]
    gd = d // groups
    assert m % chunk == 0

    def which(i):
        return jnp.minimum(i // n_prompt_chunks, 1)

    blk = (2 * _nbytes((chunk, d), F32) + _nbytes((chunk, d), BF16) + _nbytes((chunk, d), F32)
           + _nbytes((groups, chunk, chunk), F32))
    o, vn = pl.pallas_call(
        functools.partial(_gmlp_kernel, groups=groups, gd=gd),
        grid=(m // chunk,),
        in_specs=[pl.BlockSpec((chunk, d), lambda i: (i, 0)),
                  pl.BlockSpec((chunk, d), lambda i: (i, 1)),
                  pl.BlockSpec((1, d), lambda i: (0, 0)),
                  pl.BlockSpec((1, d), lambda i: (0, 0)),
                  pl.BlockSpec((1, groups, chunk, chunk), lambda i: (which(i), 0, 0, 0)),
                  pl.BlockSpec((1, chunk, groups), lambda i: (which(i), 0, 0))],
        out_specs=[pl.BlockSpec((chunk, d), lambda i: (i, 0)),
                   pl.BlockSpec((chunk, d), lambda i: (i, 0))],
        out_shape=[jax.ShapeDtypeStruct((m, d), BF16), jax.ShapeDtypeStruct((m, d), F32)],
        compiler_params=_cparams(("parallel",), blk, 4 * _nbytes((chunk, d), F32)),
        name="gmlp_gate",
    )(z, z, vg.reshape(1, d).astype(F32), vb.reshape(1, d).astype(F32), w_stack, bs_stack)
    return o, vn


def _causal_conv(up, shifted, cw_ref, cb_ref, conv_w):
    y = cb_ref[...] + cw_ref[conv_w - 1:conv_w, :] * up
    for back in range(1, conv_w):
        y = y + cw_ref[conv_w - 1 - back:conv_w - back, :] * shifted(back)
    return y


def _ffn_up_prompt_kernel(h_ref, wa_ref, wg_ref, cwa_ref, cwg_ref, cba_ref, cbg_ref,
                          act_ref, tail_a_ref, tail_g_ref, wb_scr, carry_scr, *, tiles_per_seq, conv_w):
    i = pl.program_id(1)

    @pl.when(i == 0)
    def _():
        wb_scr[0] = wa_ref[...].astype(BF16)
        wb_scr[1] = wg_ref[...].astype(BF16)
        carry_scr[...] = jnp.zeros_like(carry_scr)

    first = (i % tiles_per_seq) == 0
    tm = h_ref.shape[0]
    row = lax.broadcasted_iota(jnp.int32, (tm, 1), 0)
    hb = h_ref[...]

    def branch(idx, cw_ref, cb_ref, tail_ref):
        up = _dot(hb, wb_scr[idx])
        prev = carry_scr[idx]

        def shifted(back):
            sh = pltpu.roll(up, back, axis=0)
            for r in range(back):
                hrow = prev[8 - back + r:8 - back + r + 1, :]
                sh = jnp.where(row == r, jnp.where(first, 0.0, hrow), sh)
            return sh

        y = _causal_conv(up, shifted, cw_ref, cb_ref, conv_w)
        last = up[tm - 8:tm, :]
        carry_scr[idx] = last
        tail_ref[0] = last
        return y

    ya = branch(0, cwa_ref, cba_ref, tail_a_ref)
    yg = branch(1, cwg_ref, cbg_ref, tail_g_ref)
    act_ref[...] = (ya * (yg * jax.nn.sigmoid(yg))).astype(act_ref.dtype)


def _ffn_up_prompt(h, w_up, cw, cb, *, rows, seq, out_rows):
    k, f2 = w_up.shape
    f = f2 // 2
    conv_w = cw.shape[0]
    tm = _pick_tile(seq, 1024, 16)
    tf = _pick_tile(f, 256, LANE)
    nf = f // tf
    n_tiles = rows // tm
    blk = (_nbytes((tm, k), BF16) + 2 * _nbytes((k, tf), F32) + _nbytes((tm, tf), BF16)
           + 2 * _nbytes((8, tf), F32) + 6 * _nbytes((8, tf), F32))
    tail = jax.ShapeDtypeStruct((n_tiles, 8, f), F32)
    act, tail_a, tail_g = pl.pallas_call(
        functools.partial(_ffn_up_prompt_kernel, tiles_per_seq=seq // tm, conv_w=conv_w),
        grid=(nf, n_tiles),
        in_specs=[pl.BlockSpec((tm, k), lambda j, i: (i, 0)),
                  pl.BlockSpec((k, tf), lambda j, i: (0, j)),
                  pl.BlockSpec((k, tf), lambda j, i: (0, j + nf)),
                  pl.BlockSpec((conv_w, tf), lambda j, i: (0, j)),
                  pl.BlockSpec((conv_w, tf), lambda j, i: (0, j + nf)),
                  pl.BlockSpec((1, tf), lambda j, i: (0, j)),
                  pl.BlockSpec((1, tf), lambda j, i: (0, j + nf))],
        out_specs=[pl.BlockSpec((tm, tf), lambda j, i: (i, j)),
                   pl.BlockSpec((1, 8, tf), lambda j, i: (i, 0, j)),
                   pl.BlockSpec((1, 8, tf), lambda j, i: (i, 0, j))],
        out_shape=[jax.ShapeDtypeStruct((out_rows, f), BF16), tail, tail],
        scratch_shapes=[pltpu.VMEM((2, k, tf), BF16), pltpu.VMEM((2, 8, tf), F32)],
        compiler_params=_cparams(("parallel", "arbitrary"), blk,
                                 2 * _nbytes((k, tf), BF16) + 8 * _nbytes((tm, tf), F32)),
        name="ffn_up_conv_prompt",
    )(h, w_up, w_up, cw, cw, cb.reshape(1, f2), cb.reshape(1, f2))
    tiles_per_seq = seq // tm
    keep = slice(8 - (conv_w - 1), 8)
    tails = jnp.concatenate([tail_a[tiles_per_seq - 1::tiles_per_seq, keep], tail_g[tiles_per_seq - 1::tiles_per_seq, keep]],
                            axis=-1)
    return act, tails


def _ffn_up_sample_kernel(h_ref, wa_ref, wg_ref, ea_ref, eg_ref, cwa_ref, cwg_ref, cba_ref, cbg_ref,
                          act_ref, upa_ref, upg_ref, *, steps, conv_w):
    ms = h_ref.shape[0]
    t = lax.rem(lax.broadcasted_iota(jnp.int32, (ms, 1), 0), steps)
    hb = h_ref[...]

    def branch(w_ref, e_ref, cw_ref, cb_ref, up_ref):
        up = _dot(hb, w_ref[...].astype(BF16))
        up_ref[...] = up

        def shifted(back):
            return jnp.where(t < back, e_ref[back - 1], pltpu.roll(up, back, axis=0))

        return _causal_conv(up, shifted, cw_ref, cb_ref, conv_w)

    ya = branch(wa_ref, ea_ref, cwa_ref, cba_ref, upa_ref)
    yg = branch(wg_ref, eg_ref, cwg_ref, cbg_ref, upg_ref)
    act_ref[...] = (ya * (yg * jax.nn.sigmoid(yg))).astype(act_ref.dtype)


def _ffn_up_sample(h, w_up, cw, cb, buf, *, row_start, steps):
    k, f2 = w_up.shape
    f = f2 // 2
    conv_w = cw.shape[0]
    bs = buf.shape[0]
    ms = bs * steps
    assert row_start % ms == 0
    tf = _pick_tile(f, 256, LANE)
    nf = f // tf
    fill = jnp.stack([
        jnp.stack([buf[:, conv_w - 1 + t - back] if t < back else jnp.zeros((bs, f2), F32) for t in range(steps)],
                  axis=1).reshape(ms, f2)
        for back in range(1, conv_w)])
    blk = (_nbytes((ms, k), BF16) + 2 * _nbytes((k, tf), F32) + 2 * _nbytes((conv_w - 1, ms, tf), F32)
           + _nbytes((ms, tf), BF16) + 2 * _nbytes((ms, tf), F32) + 6 * _nbytes((8, tf), F32))
    up_shape = jax.ShapeDtypeStruct((ms, f), F32)
    act, up_a, up_g = pl.pallas_call(
        functools.partial(_ffn_up_sample_kernel, steps=steps, conv_w=conv_w),
        grid=(nf,),
        in_specs=[pl.BlockSpec((ms, k), lambda j: (row_start // ms, 0)),
                  pl.BlockSpec((k, tf), lambda j: (0, j)),
                  pl.BlockSpec((k, tf), lambda j: (0, j + nf)),
                  pl.BlockSpec((conv_w - 1, ms, tf), lambda j: (0, 0, j)),
                  pl.BlockSpec((conv_w - 1, ms, tf), lambda j: (0, 0, j + nf)),
                  pl.BlockSpec((conv_w, tf), lambda j: (0, j)),
                  pl.BlockSpec((conv_w, tf), lambda j: (0, j + nf)),
                  pl.BlockSpec((1, tf), lambda j: (0, j)),
                  pl.BlockSpec((1, tf), lambda j: (0, j + nf))],
        out_specs=[pl.BlockSpec((ms, tf), lambda j: (0, j))] * 3,
        out_shape=[jax.ShapeDtypeStruct((ms, f), BF16), up_shape, up_shape],
        compiler_params=_cparams(("parallel",), blk, 2 * _nbytes((k, tf), BF16) + 8 * _nbytes((ms, tf), F32)),
        name="ffn_up_conv_sample",
    )(h, w_up, w_up, fill, fill, cw, cw, cb.reshape(1, f2), cb.reshape(1, f2))
    up_tail = jnp.concatenate([up_a.reshape(bs, steps, f), up_g.reshape(bs, steps, f)], axis=-1)
    return act, up_tail


def _pad_sample(a, mp, bs, ts, fill=0.0):
    n = a.shape[1]
    s = a[mp:].reshape(bs, ts, n)
    s = jnp.pad(s, ((0, 0), (0, SAMPLE_PAD_T - ts), (0, 0)), constant_values=fill)
    return s.reshape(bs * SAMPLE_PAD_T, n)


def _merge_sample(y_full, y_pad, mp, bs, ts):
    n = y_pad.shape[1]
    ys = y_pad.reshape(bs, SAMPLE_PAD_T, n)[:, :ts].reshape(bs * ts, n).astype(y_full.dtype)
    return lax.dynamic_update_slice(y_full, ys, (mp, 0))


def _forward(x_prompt, x_sample, state_hgrn_S, state_mlstm_C, state_mlstm_n, state_mlstm_m, state_ffn_conv,
             norm_mix, norm_ffn, norm_final,
             hgrn_w_q, hgrn_w_f, hgrn_w_i, hgrn_w_g, hgrn_lb, hgrn_onorm, hgrn_w_o,
             mlstm_w_q, mlstm_w_k, mlstm_w_v, mlstm_w_og, mlstm_w_if, mlstm_b_if, mlstm_hnorm, mlstm_w_out,
             gmlp_w_in, gmlp_b_in, gmlp_vnorm_g, gmlp_vnorm_b, gmlp_w_s, gmlp_b_s, gmlp_w_out,
             ffn_w_up, ffn_conv_w, ffn_conv_b, ffn_w_down):
    bp, tp, d = x_prompt.shape
    bs, ts, _ = x_sample.shape
    mp, ms = bp * tp, bs * ts
    m = mp + ms
    depth = norm_mix.shape[0]
    hg_heads, hg_dk, hg_dv = state_hgrn_S.shape[2:]
    ml_heads, ml_dqk, ml_dv = state_mlstm_C.shape[2:]
    conv_w = ffn_conv_w.shape[1]

    x = jnp.concatenate([x_prompt.reshape(mp, d), x_sample.reshape(ms, d)], axis=0)

    lb = jax.nn.softmax(hgrn_lb.astype(F32), axis=0)
    lbs = jnp.cumsum(lb, axis=0) - lb[0]

    out_c, out_n, out_m, out_v, out_conv = [], [], [], [], []
    s_p_all = s_s_all = state_hgrn_S.shape[0]
    for i in range(depth):
        j = i // N_MIXERS
        kind = i % N_MIXERS
        h = _rmsnorm(x, norm_mix[i], BF16)
        if kind == 0:
            (q,) = _matmul(h, hgrn_w_q[j], _epi_silu, name="hgrn_q")
            lf, k = _matmul(h, hgrn_w_f[j], _epi_hgrn_forget, extras=[("row", lbs[j])],
                            out_dtypes=(F32, F32), name="hgrn_f")
            (v,) = _matmul(h, hgrn_w_i[j], _epi_id, name="hgrn_i")
            (g,) = _matmul(h, hgrn_w_g[j], _epi_sigmoid, name="hgrn_g")
            chunk = min(HGRN_CHUNK, tp)
            y, s_p_all = _hgrn_scan(q, lf, k, v, g, hgrn_onorm[j],
                                    jnp.zeros((1, bp, hg_heads, hg_dk, hg_dv), F32), j, s_p_all,
                                    batch=bp, seq=tp, chunk=chunk, t_real=tp, out_rows=m, out_dtype=BF16)
            pads = [_pad_sample(a, mp, bs, ts) for a in (q, lf, k, v, g)]
            y_s, s_s_all = _hgrn_scan(*pads, hgrn_onorm[j], state_hgrn_S.astype(F32), j, s_s_all,
                                      batch=bs, seq=SAMPLE_PAD_T, chunk=SAMPLE_PAD_T, t_real=ts,
                                      out_rows=bs * SAMPLE_PAD_T, out_dtype=BF16)
            y = _merge_sample(y, y_s, mp, bs, ts)
            (x,) = _matmul(y, hgrn_w_o[j], _epi_residual, extras=[("tile", x)], name="hgrn_o")
        elif kind == 1:
            (q,) = _matmul(h, mlstm_w_q[j], _epi_id, name="mlstm_q")
            (k,) = _matmul(h, mlstm_w_k[j], functools.partial(_epi_scale, scale=ml_dqk ** -0.5), name="mlstm_k")
            (v,) = _matmul(h, mlstm_w_v[j], _epi_id, name="mlstm_v")
            (og,) = _matmul(h, mlstm_w_og[j], _epi_sigmoid, name="mlstm_og")
            w_if = jnp.pad(mlstm_w_if[j], ((0, 0), (0, LANE - 2 * ml_heads)))
            b_if = jnp.pad(mlstm_b_if[j], (0, LANE - 2 * ml_heads))
            (gates,) = _matmul(h, w_if, functools.partial(_epi_mlstm_gates, heads=ml_heads),
                               extras=[("row", b_if)], name="mlstm_if")
            chunk = min(MLSTM_CHUNK, tp)
            y, c_p, n_p, m_p = _mlstm_scan(
                q, k, v, og, gates, mlstm_hnorm[j],
                jnp.zeros((bp, ml_heads, ml_dqk, ml_dv), F32), jnp.zeros((bp, ml_heads, ml_dqk), F32),
                jnp.zeros((bp, ml_heads), F32),
                batch=bp, seq=tp, chunk=chunk, out_rows=m, out_dtype=BF16)
            pads = [_pad_sample(a, mp, bs, ts) for a in (q, k, v, og)]
            gate_fill = jnp.where(jnp.arange(LANE) < ml_heads, NEG_BIG, 0.0).astype(F32)
            gs = gates[mp:].reshape(bs, ts, LANE)
            gs = jnp.concatenate([gs, jnp.broadcast_to(gate_fill, (bs, SAMPLE_PAD_T - ts, LANE))], axis=1)
            y_s, c_s, n_s, m_s = _mlstm_scan(
                *pads, gs.reshape(bs * SAMPLE_PAD_T, LANE), mlstm_hnorm[j],
                state_mlstm_C[j].astype(F32), state_mlstm_n[j].astype(F32), state_mlstm_m[j].astype(F32),
                batch=bs, seq=SAMPLE_PAD_T, chunk=SAMPLE_PAD_T, out_rows=bs * SAMPLE_PAD_T, out_dtype=BF16)
            y = _merge_sample(y, y_s, mp, bs, ts)
            out_c.append((c_p, c_s))
            out_n.append((n_p, n_s))
            out_m.append((m_p, m_s))
            (x,) = _matmul(y, mlstm_w_out[j], _epi_residual, extras=[("tile", x)], name="mlstm_out")
        else:
            (z,) = _matmul(h, gmlp_w_in[j], _epi_bias_gelu, extras=[("row", gmlp_b_in[j])], name="gmlp_in")
            groups = gmlp_w_s.shape[1]
            lp = min(GMLP_CHUNK, tp)
            ls = min(GMLP_CHUNK, ts)
            reps = GMLP_CHUNK // ls
            w_s_blk = jnp.einsum("ab,gts->gatbs", jnp.eye(reps, dtype=F32), gmlp_w_s[j][:, :ls, :ls])
            w_s_blk = w_s_blk.reshape(groups, GMLP_CHUNK, GMLP_CHUNK)
            w_p = gmlp_w_s[j][:, :lp, :lp]
            w_stack = jnp.stack([w_p, w_s_blk])
            bs_stack = jnp.stack([gmlp_b_s[j][:, :lp].T, jnp.tile(gmlp_b_s[j][:, :ls].T, (reps, 1))])
            y, vn = _gmlp_gate(z, gmlp_vnorm_g[j], gmlp_vnorm_b[j], w_stack, bs_stack,
                               n_prompt_chunks=mp // GMLP_CHUNK)
            out_v.append(vn[mp:].reshape(bs, ts, d))
            (x,) = _matmul(y, gmlp_w_out[j], _epi_residual, extras=[("tile", x)], name="gmlp_out")

        h = _rmsnorm(x, norm_ffn[i], BF16)
        act, tail_p = _ffn_up_prompt(h, ffn_w_up[i], ffn_conv_w[i], ffn_conv_b[i], rows=mp, seq=tp, out_rows=m)
        buf = state_ffn_conv[i].astype(F32)
        act_s, up_s = _ffn_up_sample(h, ffn_w_up[i], ffn_conv_w[i], ffn_conv_b[i], buf, row_start=mp, steps=ts)
        act = lax.dynamic_update_slice(act, act_s, (mp, 0))
        tail_s = up_s[:, ts - (conv_w - 1):] if ts >= conv_w - 1 else jnp.concatenate([buf, up_s], axis=1)[:, -(conv_w - 1):]
        out_conv.append((tail_p, tail_s))
        (x,) = _matmul(act, ffn_w_down[i].astype(BF16), _epi_residual, extras=[("tile", x)], name="ffn_down")

    y_p = _rmsnorm(x, norm_final, F32, row_start=0, n_rows=mp).reshape(bp, tp, d)
    y_s = _rmsnorm(x, norm_final, F32, row_start=mp, n_rows=ms).reshape(bs, ts, d)
    stack = lambda pairs, idx: jnp.stack([p[idx] for p in pairs])
    return (y_p, y_s,
            s_p_all, s_s_all, stack(out_c, 0), stack(out_c, 1),
            stack(out_n, 0), stack(out_n, 1), stack(out_m, 0), stack(out_m, 1),
            jnp.stack(out_v), stack(out_conv, 0), stack(out_conv, 1))


_forward_jit = jax.jit(_forward)


def kernel(x_prompt, x_sample, state_hgrn_S, state_mlstm_C, state_mlstm_n, state_mlstm_m, state_ffn_conv, norm_mix, norm_ffn, norm_final, hgrn_w_q, hgrn_w_f, hgrn_w_i, hgrn_w_g, hgrn_lb, hgrn_onorm, hgrn_w_o, mlstm_w_q, mlstm_w_k, mlstm_w_v, mlstm_w_og, mlstm_w_if, mlstm_b_if, mlstm_hnorm, mlstm_w_out, gmlp_w_in, gmlp_b_in, gmlp_vnorm_g, gmlp_vnorm_b, gmlp_w_s, gmlp_b_s, gmlp_w_out, ffn_w_up, ffn_conv_w, ffn_conv_b, ffn_w_down):
    return _forward_jit(x_prompt, x_sample, state_hgrn_S, state_mlstm_C, state_mlstm_n, state_mlstm_m, state_ffn_conv,
                        norm_mix, norm_ffn, norm_final,
                        hgrn_w_q, hgrn_w_f, hgrn_w_i, hgrn_w_g, hgrn_lb, hgrn_onorm, hgrn_w_o,
                        mlstm_w_q, mlstm_w_k, mlstm_w_v, mlstm_w_og, mlstm_w_if, mlstm_b_if, mlstm_hnorm, mlstm_w_out,
                        gmlp_w_in, gmlp_b_in, gmlp_vnorm_g, gmlp_vnorm_b, gmlp_w_s, gmlp_b_s, gmlp_w_out,
                        ffn_w_up, ffn_conv_w, ffn_conv_b, ffn_w_down)
```

```python
import functools
import math

import numpy as np
import jax
import jax.numpy as jnp
from jax import lax
from jax.experimental import pallas as pl
from jax.experimental.pallas import tpu as pltpu

F32 = jnp.float32
BF16 = jnp.bfloat16

EPS = 1e-6
GATE_CAP = 15.0
N_MIXERS = 3
HGRN_CHUNK = 64
MLSTM_CHUNK = 128
GMLP_CHUNK = 128
FFN_ROW_CHUNK = 256
SAMPLE_PAD_T = 16
SAMPLE_ROWS = 16
NEG_BIG = -1e30

LANE = 128
VMEM_PHYSICAL_V7X = 64 * 1024 * 1024
VMEM_BUDGET = VMEM_PHYSICAL_V7X - 8 * 1024 * 1024


def _nbytes(shape, dtype):
    return int(np.prod(shape)) * jnp.dtype(dtype).itemsize


def _cparams(semantics, block_bytes, temp_bytes=0):
    need = 2 * block_bytes + temp_bytes + (4 << 20)
    return pltpu.CompilerParams(dimension_semantics=semantics,
                                vmem_limit_bytes=int(min(max(need, 16 << 20), VMEM_BUDGET)))


def _pick_tile(n, target, align):
    best = None
    for d in range(align, min(n, target) + 1, align):
        if n % d == 0:
            best = d
    return best if best is not None else n


def _dot(a, b):
    return jnp.dot(a, b, preferred_element_type=F32)


def _dot_nt(a, b):
    return lax.dot_general(a, b, (((1,), (1,)), ((), ())), preferred_element_type=F32)


def _dot_tn(a, b):
    return lax.dot_general(a, b, (((0,), (0,)), ((), ())), preferred_element_type=F32)


def _log_sigmoid(x):
    return -(jnp.maximum(-x, 0.0) + jnp.log1p(jnp.exp(-jnp.abs(x))))


def _split2(x):
    hi = x.astype(BF16)
    lo = (x - hi.astype(F32)).astype(BF16)
    return hi, lo


def _split3(x):
    hi = x.astype(BF16)
    r = x - hi.astype(F32)
    mid = r.astype(BF16)
    lo = (r - mid.astype(F32)).astype(BF16)
    return hi, mid, lo


def _rmsnorm_kernel(x_ref, g_ref, o_ref):
    x = x_ref[...]
    y = x * lax.rsqrt(jnp.mean(x * x, axis=-1, keepdims=True) + EPS)
    o_ref[...] = (y * g_ref[...]).astype(o_ref.dtype)


def _rmsnorm(x, g, out_dtype, row_start=0, n_rows=None):
    m_all, d = x.shape
    n_rows = m_all - row_start if n_rows is None else n_rows
    tm = _pick_tile(math.gcd(n_rows, row_start) if row_start else n_rows, 544, 16)
    off = row_start // tm
    blk = _nbytes((tm, d), F32) + _nbytes((tm, d), out_dtype)
    return pl.pallas_call(
        _rmsnorm_kernel,
        grid=(n_rows // tm,),
        in_specs=[pl.BlockSpec((tm, d), lambda i: (i + off, 0)),
                  pl.BlockSpec((1, d), lambda i: (0, 0))],
        out_specs=pl.BlockSpec((tm, d), lambda i: (i, 0)),
        out_shape=jax.ShapeDtypeStruct((n_rows, d), out_dtype),
        compiler_params=_cparams(("parallel",), blk, _nbytes((tm, d), F32)),
        name="rmsnorm",
    )(x, g.reshape(1, d).astype(F32))


def _mm_kernel(*refs, epi, n_extra, n_out):
    x_ref, w_ref = refs[0], refs[1]
    extras = refs[2:2 + n_extra]
    outs = refs[2 + n_extra:2 + n_extra + n_out]
    wb_scr = refs[2 + n_extra + n_out]

    @pl.when(pl.program_id(1) == 0)
    def _():
        wb_scr[...] = w_ref[...].astype(BF16)

    acc = _dot(x_ref[...], wb_scr[...])
    vals = epi(acc, *(e[...] for e in extras))
    for o_ref, val in zip(outs, vals):
        o_ref[...] = val.astype(o_ref.dtype)


def _matmul(x, w, epi, extras=(), out_dtypes=(F32,), name="matmul", layer=None):
    m, k = x.shape
    n = w.shape[-1]
    tile_bytes = 6 << 20
    tn_cap = tile_bytes // (jnp.dtype(w.dtype).itemsize * k) // LANE * LANE
    tn = _pick_tile(n, max(LANE, min(512, tn_cap)), LANE)
    tm = _pick_tile(m, max(16, tile_bytes // (jnp.dtype(x.dtype).itemsize * k)), 16)
    w_spec = (pl.BlockSpec((k, tn), lambda j, i: (0, j)) if layer is None else
              pl.BlockSpec((None, k, tn), lambda j, i: (layer, 0, j)))
    in_specs = [pl.BlockSpec((tm, k), lambda j, i: (i, 0)), w_spec]
    args = [x, w]
    blk = _nbytes((tm, k), x.dtype) + _nbytes((k, tn), w.dtype)
    for kind, arr in extras:
        if kind == "row":
            in_specs.append(pl.BlockSpec((1, tn), lambda j, i: (0, j)))
            args.append(arr.reshape(1, n).astype(F32))
            blk += _nbytes((8, tn), F32)
        else:
            in_specs.append(pl.BlockSpec((tm, tn), lambda j, i: (i, j)))
            args.append(arr)
            blk += _nbytes((tm, tn), arr.dtype)
    out_specs = [pl.BlockSpec((tm, tn), lambda j, i: (i, j)) for _ in out_dtypes]
    out_shape = [jax.ShapeDtypeStruct((m, n), dt) for dt in out_dtypes]
    blk += sum(_nbytes((tm, tn), dt) for dt in out_dtypes)
    outs = pl.pallas_call(
        functools.partial(_mm_kernel, epi=epi, n_extra=len(extras), n_out=len(out_dtypes)),
        grid=(n // tn, m // tm),
        in_specs=in_specs, out_specs=out_specs, out_shape=out_shape,
        scratch_shapes=[pltpu.VMEM((k, tn), BF16)],
        compiler_params=_cparams(("parallel", "arbitrary"), blk,
                                 _nbytes((k, tn), BF16) + 3 * _nbytes((tm, tn), F32)),
        name=name,
    )(*args)
    return outs


def _epi_id(acc):
    return (acc,)


def _epi_silu(acc):
    return (acc * jax.nn.sigmoid(acc),)


def _epi_sigmoid(acc):
    return (jax.nn.sigmoid(acc),)


def _epi_scale(acc, *, scale):
    return (acc * scale,)


def _epi_residual(acc, res):
    return (res + acc,)


def _epi_hgrn_forget(acc, lb):
    lbh = jnp.maximum(lb, 0.0)
    a = jnp.log(lbh)
    c = jnp.log1p(-lbh) + _log_sigmoid(acc)
    logf = jnp.maximum(a, c) + jnp.log1p(jnp.exp(-jnp.abs(a - c)))
    kk = (1.0 - lbh) * jax.nn.sigmoid(-acc)
    return logf, kk


def _epi_bias_gelu(acc, b):
    x = acc + b
    return (0.5 * x * (1.0 + lax.erf(x * (2.0 ** -0.5))),)


def _epi_mlstm_gates(acc, b, *, heads):
    gates = GATE_CAP * jnp.tanh((acc + b) / GATE_CAP)
    col = lax.broadcasted_iota(jnp.int32, acc.shape, 1)
    return (jnp.where(col < heads, gates, _log_sigmoid(gates)),)


def _hgrn_levels(chunk, t_real):
    return tuple(m for m in (2 ** p for p in range(int(math.log2(chunk)) - 1, -1, -1)) if m < t_real)


def _hgrn_sum_matrix(chunk, levels, t_len):
    t = np.arange(chunk)[:, None]
    r = np.arange(chunk)[None, :]
    same = (t // t_len) == (r // t_len)
    mats = [(r <= t) & same, (r > t) & same]
    for m in levels:
        mid = (t // (2 * m)) * (2 * m) + m - 1
        second = (t % (2 * m)) >= m
        mats.append(np.where(second, (r > mid) & (r <= t), (r > t) & (r <= mid)))
    return np.concatenate(mats, axis=0).astype(np.float32)


def _hgrn_kernel(*refs, heads, chunk, levels, dk, group, n_aliased):
    q_ref, lf_ref, k_ref, v_ref, g_ref, onorm_ref, d_ref, s0_ref = refs[:8]
    y_ref, s_ref, o_scr = refs[8 + n_aliased:]
    c = pl.program_id(1)
    t_len = chunk // group
    dv = s_ref.shape[-1]

    @pl.when(c == 0)
    def _():
        s_ref[...] = s0_ref[...]

    row = lax.broadcasted_iota(jnp.int32, (chunk, 1), 0)
    in_seq = [None] if group == 1 else [(row >> int(math.log2(t_len))) == s for s in range(group)]
    seq_ones = [jnp.ones((chunk, dv), BF16) if mask is None else
                jnp.where(mask, jnp.ones((chunk, dv), F32), 0.0).astype(BF16) for mask in in_seq]
    r2 = lax.broadcasted_iota(jnp.int32, (chunk, chunk), 0)
    c2 = lax.broadcasted_iota(jnp.int32, (chunk, chunk), 1)

    lf_hi, lf_lo = _split2(lf_ref[...])
    dmat = d_ref[...]
    xs = _dot(dmat, lf_hi) + _dot(dmat, lf_lo)
    q = q_ref[...]
    k = k_ref[...]
    vb = v_ref[...].astype(BF16)
    ex_b = jnp.exp(xs[0:chunk])
    q_in = q * ex_b
    k_out = k * jnp.exp(xs[chunk:2 * chunk])
    q_in = [(q_in if mask is None else jnp.where(mask, q_in, 0.0)).astype(BF16) for mask in in_seq]
    k_out = [(k_out if mask is None else jnp.where(mask, k_out, 0.0)).astype(BF16) for mask in in_seq]
    qs = [q.astype(BF16)]
    ks = [k.astype(BF16)]
    masks = [r2 == c2]
    for li, m in enumerate(levels):
        ex = jnp.exp(xs[(2 + li) * chunk:(3 + li) * chunk])
        second = (row & m) != 0
        qs.append(jnp.where(second, q * ex, 0.0).astype(BF16))
        ks.append(jnp.where(second, 0.0, k * ex).astype(BF16))
        shift = int(math.log2(2 * m))
        masks.append(None if 2 * m == chunk else (r2 >> shift) == (c2 >> shift))

    scores = []
    for h in range(heads):
        cols = slice(h * dk, (h + 1) * dk)
        a = None
        for qs_l, ks_l, mask in zip(qs, ks, masks):
            al = _dot_nt(qs_l[:, cols], ks_l[:, cols])
            if mask is not None:
                al = jnp.where(mask, al, 0.0)
            a = al if a is None else a + al
        scores.append(a.astype(BF16))

    for h in range(heads):
        cols = slice(h * dk, (h + 1) * dk)
        o = _dot(scores[h], vb[:, cols])
        for s in range(group):
            st = s_ref[0, s, h]
            o = o + _dot(q_in[s][:, cols], st.astype(BF16))
            decay = jnp.exp(_dot_tn(lf_hi[:, cols], seq_ones[s]) + _dot_tn(lf_lo[:, cols], seq_ones[s]))
            s_ref[0, s, h] = st * decay + _dot_tn(k_out[s][:, cols], vb[:, cols])
        o_scr[:, cols] = o

    o = o_scr[...]
    y = o * lax.rsqrt(jnp.mean(o * o, axis=-1, keepdims=True) + EPS) * onorm_ref[...]
    y_ref[...] = (y * g_ref[...]).astype(y_ref.dtype)


def _hgrn_scan(q, lf, k, v, g, onorm, s0_all, layer, s_out_prev, y_prev, *, row_start, batch, seq, chunk, group):
    d = q.shape[1]
    heads, dk, dv = s0_all.shape[2:]
    t_chunk = chunk // group
    n_chunks = seq // t_chunk
    assert group == 1 or n_chunks == 1
    assert row_start % chunk == 0 and batch % group == 0
    levels = _hgrn_levels(chunk, t_chunk)
    dmat = jnp.asarray(_hgrn_sum_matrix(chunk, levels, t_chunk), dtype=BF16)
    first_blk = row_start // chunk
    tok = pl.BlockSpec((chunk, d), lambda b, c: (first_blk + b * n_chunks + c, 0))
    in_layer = min(layer, s0_all.shape[0] - 1)
    st_in = pl.BlockSpec((1, group, heads, dk, dv), lambda b, c: (in_layer, b, 0, 0, 0))
    st_out = pl.BlockSpec((1, group, heads, dk, dv), lambda b, c: (layer, b, 0, 0, 0))
    in_specs = [tok, tok, tok, tok, tok,
                pl.BlockSpec((1, d), lambda b, c: (0, 0)),
                pl.BlockSpec(dmat.shape, lambda b, c: (0, 0)),
                st_in]
    args = [q, lf, k, v, g, onorm.reshape(1, d).astype(F32), dmat, s0_all]
    aliases = {}
    for out_idx, prev in ((0, y_prev), (1, s_out_prev)):
        if not isinstance(prev, int):
            aliases[len(args)] = out_idx
            in_specs.append(pl.BlockSpec(memory_space=pl.ANY))
            args.append(prev)
    y_rows = y_prev if isinstance(y_prev, int) else y_prev.shape[0]
    n_layers = s_out_prev if isinstance(s_out_prev, int) else s_out_prev.shape[0]
    blk = (5 * _nbytes((chunk, d), F32) + _nbytes((chunk, d), BF16) + 2 * _nbytes((group, heads, dk, dv), F32)
           + _nbytes(dmat.shape, BF16))
    n_exp = 2 + len(levels)
    y, s = pl.pallas_call(
        functools.partial(_hgrn_kernel, heads=heads, chunk=chunk, levels=levels, dk=dk, group=group,
                          n_aliased=len(aliases)),
        grid=(batch // group, n_chunks),
        in_specs=in_specs,
        out_specs=[tok, st_out],
        out_shape=[jax.ShapeDtypeStruct((y_rows, d), BF16),
                   jax.ShapeDtypeStruct((n_layers, batch, heads, dk, dv), F32)],
        scratch_shapes=[pltpu.VMEM((chunk, d), F32)],
        input_output_aliases=aliases,
        compiler_params=_cparams(("parallel", "arbitrary"), blk, (3 * n_exp + 8) * _nbytes((chunk, d), F32)),
        name="hgrn_scan",
    )(*args)
    return y, s


def _mlstm_kernel(q_ref, k_ref, v_ref, og_ref, gt_ref, hn_ref, tri_ref, sel_ref, c0_ref, n0_ref, m0_ref,
                  y_ref, c_ref, n_ref, m_ref, *, heads, chunk, dqk, dv):
    c = pl.program_id(1)

    @pl.when(c == 0)
    def _():
        c_ref[...] = c0_ref[...]
        n_ref[...] = n0_ref[...]
        m_ref[...] = m0_ref[...]

    gates = gt_ref[...]
    tri = tri_ref[...]
    sel = sel_ref[...]
    g3 = _split3(gates)
    cum = sum(_dot(tri, p) for p in g3)
    gates_t = sum(_dot_nt(sel, p) for p in g3)
    cum_t = sum(_dot_nt(sel, p) for p in _split3(cum))
    r2 = lax.broadcasted_iota(jnp.int32, (chunk, chunk), 0)
    c2 = lax.broadcasted_iota(jnp.int32, (chunk, chunk), 1)
    causal = c2 <= r2
    lane = lax.broadcasted_iota(jnp.int32, (1, LANE), 1)
    m_row = m_ref[0]
    m_new = m_row
    for h in range(heads):
        qc = slice(h * dqk, (h + 1) * dqk)
        vc = slice(h * dv, (h + 1) * dv)
        qh = q_ref[:, qc]
        kh = k_ref[:, qc]
        vb = v_ref[:, vc].astype(BF16)
        qb = qh.astype(BF16)
        b_c = cum[:, heads + h:heads + h + 1]
        ig_c = gates[:, h:h + 1]
        b_r = cum_t[heads + h:heads + h + 1, :]
        ig_r = gates_t[h:h + 1, :]
        m_h = m_row[:, h:h + 1]
        dlog = jnp.where(causal, b_c - b_r + ig_r, -jnp.inf)
        inter = b_c + m_h
        mt = jnp.maximum(inter, jnp.max(dlog, axis=-1, keepdims=True))
        wts = jnp.exp(dlog - mt) * _dot_nt(qb, kh.astype(BF16))
        sc = jnp.exp(inter - mt)
        c_h = c_ref[0, h]
        n_h = n_ref[0, h:h + 1, :]
        num = sc * _dot(qb, c_h.astype(BF16)) + _dot(wts.astype(BF16), vb)
        den = sc * jnp.sum(qh * n_h, axis=-1, keepdims=True) + jnp.sum(wts, axis=-1, keepdims=True)
        out = num / jnp.maximum(jnp.abs(den), jnp.exp(-mt))
        m_last = mt[chunk - 1:chunk, :]
        b_last = b_c[chunk - 1:chunk, :]
        sc_state = jnp.exp(b_last + m_h - m_last)
        kw = jnp.exp(b_last - b_c + ig_c - m_last) * kh
        c_ref[0, h] = sc_state * c_h + _dot_tn(kw.astype(BF16), vb)
        n_ref[0, h:h + 1, :] = sc_state * n_h + jnp.sum(kw, axis=0, keepdims=True)
        m_new = jnp.where(lane == h, m_last, m_new)
        y = out * lax.rsqrt(jnp.mean(out * out, axis=-1, keepdims=True) + EPS) * hn_ref[:, vc]
        y_ref[:, vc] = (y * og_ref[:, vc]).astype(y_ref.dtype)
    m_ref[0] = m_new


def _mlstm_scan(q, k, v, og, gates, hnorm, c0, n0, m0, *, batch, seq, chunk, out_rows, out_dtype):
    heads, dqk, dv = c0.shape[1], c0.shape[2], c0.shape[3]
    dq_all, dv_all = q.shape[1], v.shape[1]
    n_chunks = seq // chunk
    assert 2 * heads <= 16
    tri = jnp.asarray(np.tril(np.ones((chunk, chunk), np.float32)), dtype=BF16)
    sel = jnp.asarray(np.eye(16, LANE, dtype=np.float32), dtype=BF16)
    m0p = jnp.pad(m0, ((0, 0), (0, LANE - heads))).reshape(batch, 1, LANE)
    tq = pl.BlockSpec((chunk, dq_all), lambda b, c: (b * n_chunks + c, 0))
    tv = pl.BlockSpec((chunk, dv_all), lambda b, c: (b * n_chunks + c, 0))
    tg = pl.BlockSpec((chunk, LANE), lambda b, c: (b * n_chunks + c, 0))
    sc_ = pl.BlockSpec((1, heads, dqk, dv), lambda b, c: (b, 0, 0, 0))
    sn_ = pl.BlockSpec((1, heads, dqk), lambda b, c: (b, 0, 0))
    sm_ = pl.BlockSpec((1, 1, LANE), lambda b, c: (b, 0, 0))
    blk = (2 * _nbytes((chunk, dq_all), F32) + 2 * _nbytes((chunk, dv_all), F32) + _nbytes((chunk, dv_all), out_dtype)
           + 2 * _nbytes(c0.shape[1:], F32))
    y, c_out, n_out, m_out = pl.pallas_call(
        functools.partial(_mlstm_kernel, heads=heads, chunk=chunk, dqk=dqk, dv=dv),
        grid=(batch, n_chunks),
        in_specs=[tq, tq, tv, tv, tg,
                  pl.BlockSpec((1, dv_all), lambda b, c: (0, 0)),
                  pl.BlockSpec(tri.shape, lambda b, c: (0, 0)),
                  pl.BlockSpec(sel.shape, lambda b, c: (0, 0)),
                  sc_, sn_, sm_],
        out_specs=[tv, sc_, sn_, sm_],
        out_shape=[jax.ShapeDtypeStruct((out_rows, dv_all), out_dtype),
                   jax.ShapeDtypeStruct(c0.shape, F32),
                   jax.ShapeDtypeStruct(n0.shape, F32),
                   jax.ShapeDtypeStruct((batch, 1, LANE), F32)],
        compiler_params=_cparams(("parallel", "arbitrary"), blk, 8 * _nbytes((chunk, dv_all), F32)),
        name="mlstm_scan",
    )(q, k, v, og, gates, hnorm.reshape(1, dv_all).astype(F32), tri, sel, c0, n0, m0p)
    return y, c_out, n_out, m_out[:, 0, :heads]


def _gmlp_kernel(u_ref, v_ref, vg_ref, vb_ref, w_ref, bs_ref, o_ref, vn_ref, *, groups, gd):
    v = v_ref[...]
    mu = jnp.mean(v, axis=-1, keepdims=True)
    xc = v - mu
    vn = xc * lax.rsqrt(jnp.mean(xc * xc, axis=-1, keepdims=True) + EPS) * vg_ref[...] + vb_ref[...]
    vn_ref[...] = vn
    n = v.shape[0]
    causal = (lax.broadcasted_iota(jnp.int32, (n, n), 1) <= lax.broadcasted_iota(jnp.int32, (n, n), 0))
    bs = bs_ref[0]
    for g in range(groups):
        cols = slice(g * gd, (g + 1) * gd)
        wg = jnp.where(causal, w_ref[0, g], 0.0).astype(BF16)
        mix = _dot(wg, vn[:, cols].astype(BF16)) + bs[:, g:g + 1]
        o_ref[:, cols] = (u_ref[:, cols] * mix).astype(o_ref.dtype)


def _gmlp_gate(z, vg, vb, w_stack, bs_stack, *, n_prompt_chunks):
    m, d2 = z.shape
    d = d2 // 2
    groups, chunk = w_stack.shape[1], w_stack.shape[2]
    gd = d // groups
    assert m % chunk == 0

    def which(i):
        return jnp.minimum(i // n_prompt_chunks, 1)

    blk = (2 * _nbytes((chunk, d), F32) + _nbytes((chunk, d), BF16) + _nbytes((chunk, d), F32)
           + _nbytes((groups, chunk, chunk), F32))
    o, vn = pl.pallas_call(
        functools.partial(_gmlp_kernel, groups=groups, gd=gd),
        grid=(m // chunk,),
        in_specs=[pl.BlockSpec((chunk, d), lambda i: (i, 0)),
                  pl.BlockSpec((chunk, d), lambda i: (i, 1)),
                  pl.BlockSpec((1, d), lambda i: (0, 0)),
                  pl.BlockSpec((1, d), lambda i: (0, 0)),
                  pl.BlockSpec((1, groups, chunk, chunk), lambda i: (which(i), 0, 0, 0)),
                  pl.BlockSpec((1, chunk, groups), lambda i: (which(i), 0, 0))],
        out_specs=[pl.BlockSpec((chunk, d), lambda i: (i, 0)),
                   pl.BlockSpec((chunk, d), lambda i: (i, 0))],
        out_shape=[jax.ShapeDtypeStruct((m, d), BF16), jax.ShapeDtypeStruct((m, d), F32)],
        compiler_params=_cparams(("parallel",), blk, 4 * _nbytes((chunk, d), F32)),
        name="gmlp_gate",
    )(z, z, vg.reshape(1, d).astype(F32), vb.reshape(1, d).astype(F32), w_stack, bs_stack)
    return o, vn


def _causal_conv(tap, cw_ref, cb_ref, conv_w):
    y = cb_ref[...] + cw_ref[conv_w - 1:conv_w, :] * tap(0)
    for back in range(1, conv_w):
        y = y + cw_ref[conv_w - 1 - back:conv_w - back, :] * tap(back)
    return y


def _ffn_up_prompt_kernel(h_ref, wa_ref, wg_ref, cwa_ref, cwg_ref, cba_ref, cbg_ref,
                          act_ref, tail_a_ref, tail_g_ref, wb_scr, up_scr, *, tiles_per_seq, conv_w, row_chunk):
    i = pl.program_id(1)
    tm = h_ref.shape[0]
    n_chunks = tm // row_chunk

    @pl.when(i == 0)
    def _():
        wb_scr[0] = wa_ref[...].astype(BF16)
        wb_scr[1] = wg_ref[...].astype(BF16)

    @pl.when(lax.rem(i, tiles_per_seq) == 0)
    def _():
        up_scr[:, 0:8, :] = jnp.zeros((2, 8, up_scr.shape[2]), F32)

    def multiply(c):
        hb = h_ref[c * row_chunk:(c + 1) * row_chunk, :]
        for idx in range(2):
            up_scr[idx, 8 + c * row_chunk:8 + (c + 1) * row_chunk, :] = _dot(hb, wb_scr[idx])

    def finish(c):
        lo = 8 + c * row_chunk
        ya = _causal_conv(lambda back: up_scr[0, lo - back:lo - back + row_chunk, :], cwa_ref, cba_ref, conv_w)
        yg = _causal_conv(lambda back: up_scr[1, lo - back:lo - back + row_chunk, :], cwg_ref, cbg_ref, conv_w)
        act_ref[c * row_chunk:(c + 1) * row_chunk, :] = (ya * (yg * jax.nn.sigmoid(yg))).astype(act_ref.dtype)

    multiply(0)
    for c in range(1, n_chunks):
        multiply(c)
        finish(c - 1)
    finish(n_chunks - 1)
    for idx, tail_ref in ((0, tail_a_ref), (1, tail_g_ref)):
        last = up_scr[idx, tm:tm + 8, :]
        tail_ref[0] = last
        up_scr[idx, 0:8, :] = last


def _ffn_up_prompt(h, w_up, layer, cw, cb, *, rows, seq, out_rows):
    _, k, f2 = w_up.shape
    f = f2 // 2
    conv_w = cw.shape[0]
    tm = _pick_tile(seq, 1024, 16)
    tf = _pick_tile(f, 256, LANE)
    nf = f // tf
    n_tiles = rows // tm
    row_chunk = _pick_tile(tm, FFN_ROW_CHUNK, 16)
    blk = (_nbytes((tm, k), BF16) + 2 * _nbytes((k, tf), F32) + _nbytes((tm, tf), BF16)
           + 2 * _nbytes((8, tf), F32) + 6 * _nbytes((8, tf), F32))
    tail = jax.ShapeDtypeStruct((n_tiles, 8, f), F32)
    act, tail_a, tail_g = pl.pallas_call(
        functools.partial(_ffn_up_prompt_kernel, tiles_per_seq=seq // tm, conv_w=conv_w, row_chunk=row_chunk),
        grid=(nf, n_tiles),
        in_specs=[pl.BlockSpec((tm, k), lambda j, i: (i, 0)),
                  pl.BlockSpec((None, k, tf), lambda j, i: (layer, 0, j)),
                  pl.BlockSpec((None, k, tf), lambda j, i: (layer, 0, j + nf)),
                  pl.BlockSpec((conv_w, tf), lambda j, i: (0, j)),
                  pl.BlockSpec((conv_w, tf), lambda j, i: (0, j + nf)),
                  pl.BlockSpec((1, tf), lambda j, i: (0, j)),
                  pl.BlockSpec((1, tf), lambda j, i: (0, j + nf))],
        out_specs=[pl.BlockSpec((tm, tf), lambda j, i: (i, j)),
                   pl.BlockSpec((1, 8, tf), lambda j, i: (i, 0, j)),
                   pl.BlockSpec((1, 8, tf), lambda j, i: (i, 0, j))],
        out_shape=[jax.ShapeDtypeStruct((out_rows, f), BF16), tail, tail],
        scratch_shapes=[pltpu.VMEM((2, k, tf), BF16), pltpu.VMEM((2, 8 + tm, tf), F32)],
        compiler_params=_cparams(("parallel", "arbitrary"), blk,
                                 2 * _nbytes((k, tf), BF16) + 2 * _nbytes((8 + tm, tf), F32)
                                 + 6 * _nbytes((row_chunk, tf), F32)),
        name="ffn_up_conv_prompt",
    )(h, w_up, w_up, cw, cw, cb.reshape(1, f2), cb.reshape(1, f2))
    tiles_per_seq = seq // tm
    keep = slice(8 - (conv_w - 1), 8)
    tails = jnp.concatenate([tail_a[tiles_per_seq - 1::tiles_per_seq, keep], tail_g[tiles_per_seq - 1::tiles_per_seq, keep]],
                            axis=-1)
    return act, tails


def _ffn_up_sample_kernel(h_ref, wa_ref, wg_ref, ba_ref, bg_ref, cwa_ref, cwg_ref, cba_ref, cbg_ref,
                          act_ref, upa_ref, upg_ref, *, steps, conv_w):
    bs = h_ref.shape[0] // steps
    hb = h_ref[...]

    def branch(w_ref, buf_ref, cw_ref, cb_ref, up_ref):
        up = _dot(hb, w_ref[...].astype(BF16))
        up_ref[...] = up

        def at(t):
            return up[t * bs:(t + 1) * bs, :] if t >= 0 else buf_ref[conv_w - 1 + t]

        return [_causal_conv(lambda back, t=t: at(t - back), cw_ref, cb_ref, conv_w) for t in range(steps)]

    ya = branch(wa_ref, ba_ref, cwa_ref, cba_ref, upa_ref)
    yg = branch(wg_ref, bg_ref, cwg_ref, cbg_ref, upg_ref)
    for t in range(steps):
        act_ref[t * bs:(t + 1) * bs, :] = (ya[t] * (yg[t] * jax.nn.sigmoid(yg[t]))).astype(act_ref.dtype)


def _ffn_up_sample(h_t, w_up, layer, cw, cb, buf_t, *, steps):
    _, k, f2 = w_up.shape
    f = f2 // 2
    conv_w = cw.shape[0]
    ms = h_t.shape[0]
    bs = ms // steps
    assert steps >= conv_w - 1 and bs % 16 == 0
    tf = _pick_tile(f, 256, LANE)
    nf = f // tf
    blk = (_nbytes((ms, k), BF16) + 2 * _nbytes((k, tf), F32) + 2 * _nbytes((conv_w - 1, bs, tf), F32)
           + _nbytes((ms, tf), BF16) + 2 * _nbytes((ms, tf), F32) + 6 * _nbytes((8, tf), F32))
    up_shape = jax.ShapeDtypeStruct((ms, f), F32)
    act, up_a, up_g = pl.pallas_call(
        functools.partial(_ffn_up_sample_kernel, steps=steps, conv_w=conv_w),
        grid=(nf,),
        in_specs=[pl.BlockSpec((ms, k), lambda j: (0, 0)),
                  pl.BlockSpec((None, k, tf), lambda j: (layer, 0, j)),
                  pl.BlockSpec((None, k, tf), lambda j: (layer, 0, j + nf)),
                  pl.BlockSpec((conv_w - 1, bs, tf), lambda j: (0, 0, j)),
                  pl.BlockSpec((conv_w - 1, bs, tf), lambda j: (0, 0, j + nf)),
                  pl.BlockSpec((conv_w, tf), lambda j: (0, j)),
                  pl.BlockSpec((conv_w, tf), lambda j: (0, j + nf)),
                  pl.BlockSpec((1, tf), lambda j: (0, j)),
                  pl.BlockSpec((1, tf), lambda j: (0, j + nf))],
        out_specs=[pl.BlockSpec((ms, tf), lambda j: (0, j))] * 3,
        out_shape=[jax.ShapeDtypeStruct((ms, f), BF16), up_shape, up_shape],
        compiler_params=_cparams(("parallel",), blk, 2 * _nbytes((k, tf), BF16) + 8 * _nbytes((ms, tf), F32)),
        name="ffn_up_conv_sample",
    )(h_t, w_up, w_up, buf_t, buf_t, cw, cw, cb.reshape(1, f2), cb.reshape(1, f2))
    keep = slice((steps - (conv_w - 1)) * bs, ms)
    up_tail = jnp.concatenate([up_a[keep], up_g[keep]], axis=-1).reshape(conv_w - 1, bs, f2).transpose(1, 0, 2)
    return act, up_tail


def _pad_sample(a, mp, bs, ts, fill=0.0):
    n = a.shape[1]
    s = a[mp:].reshape(bs, ts, n)
    s = jnp.pad(s, ((0, 0), (0, SAMPLE_PAD_T - ts), (0, 0)), constant_values=fill)
    return s.reshape(bs * SAMPLE_PAD_T, n)


def _merge_sample(y_full, y_pad, mp, bs, ts):
    n = y_pad.shape[1]
    ys = y_pad.reshape(bs, SAMPLE_PAD_T, n)[:, :ts].reshape(bs * ts, n).astype(y_full.dtype)
    return lax.dynamic_update_slice(y_full, ys, (mp, 0))


def _forward(x_prompt, x_sample, state_hgrn_S, state_mlstm_C, state_mlstm_n, state_mlstm_m, state_ffn_conv,
             norm_mix, norm_ffn, norm_final,
             hgrn_w_q, hgrn_w_f, hgrn_w_i, hgrn_w_g, hgrn_lb, hgrn_onorm, hgrn_w_o,
             mlstm_w_q, mlstm_w_k, mlstm_w_v, mlstm_w_og, mlstm_w_if, mlstm_b_if, mlstm_hnorm, mlstm_w_out,
             gmlp_w_in, gmlp_b_in, gmlp_vnorm_g, gmlp_vnorm_b, gmlp_w_s, gmlp_b_s, gmlp_w_out,
             ffn_w_up, ffn_conv_w, ffn_conv_b, ffn_w_down):
    bp, tp, d = x_prompt.shape
    bs, ts, _ = x_sample.shape
    mp, ms = bp * tp, bs * ts
    m = mp + ms
    depth = norm_mix.shape[0]
    hg_heads, hg_dk, hg_dv = state_hgrn_S.shape[2:]
    ml_heads, ml_dqk, ml_dv = state_mlstm_C.shape[2:]
    conv_w = ffn_conv_w.shape[1]

    x = jnp.concatenate([x_prompt.reshape(mp, d), x_sample.reshape(ms, d)], axis=0)

    lb = jax.nn.softmax(hgrn_lb.astype(F32), axis=0)
    lbs = jnp.cumsum(lb, axis=0) - lb[0]

    w_down = ffn_w_down.astype(BF16)
    out_c, out_n, out_m, out_v, out_conv = [], [], [], [], []
    s_p_all = s_s_all = state_hgrn_S.shape[0]
    for i in range(depth):
        j = i // N_MIXERS
        kind = i % N_MIXERS
        h = _rmsnorm(x, norm_mix[i], BF16)
        if kind == 0:
            (q,) = _matmul(h, hgrn_w_q, _epi_silu, name="hgrn_q", layer=j)
            lf, k = _matmul(h, hgrn_w_f, _epi_hgrn_forget, extras=[("row", lbs[j])],
                            out_dtypes=(F32, F32), name="hgrn_f", layer=j)
            (v,) = _matmul(h, hgrn_w_i, _epi_id, name="hgrn_i", layer=j)
            (g,) = _matmul(h, hgrn_w_g, _epi_sigmoid, name="hgrn_g", layer=j)
            y, s_p_all = _hgrn_scan(q, lf, k, v, g, hgrn_onorm[j],
                                    jnp.zeros((1, bp, hg_heads, hg_dk, hg_dv), F32), j, s_p_all, m,
                                    row_start=0, batch=bp, seq=tp, chunk=min(HGRN_CHUNK, tp), group=1)
            group = max(1, SAMPLE_ROWS // ts)
            y, s_s_all = _hgrn_scan(q, lf, k, v, g, hgrn_onorm[j], state_hgrn_S.astype(F32), j, s_s_all, y,
                                    row_start=mp, batch=bs, seq=ts, chunk=group * ts, group=group)
            (x,) = _matmul(y, hgrn_w_o, _epi_residual, extras=[("tile", x)], name="hgrn_o", layer=j)
        elif kind == 1:
            (q,) = _matmul(h, mlstm_w_q, _epi_id, name="mlstm_q", layer=j)
            (k,) = _matmul(h, mlstm_w_k, functools.partial(_epi_scale, scale=ml_dqk ** -0.5), name="mlstm_k",
                           layer=j)
            (v,) = _matmul(h, mlstm_w_v, _epi_id, name="mlstm_v", layer=j)
            (og,) = _matmul(h, mlstm_w_og, _epi_sigmoid, name="mlstm_og", layer=j)
            w_if = jnp.pad(mlstm_w_if[j], ((0, 0), (0, LANE - 2 * ml_heads)))
            b_if = jnp.pad(mlstm_b_if[j], (0, LANE - 2 * ml_heads))
            (gates,) = _matmul(h, w_if, functools.partial(_epi_mlstm_gates, heads=ml_heads),
                               extras=[("row", b_if)], name="mlstm_if")
            chunk = min(MLSTM_CHUNK, tp)
            y, c_p, n_p, m_p = _mlstm_scan(
                q, k, v, og, gates, mlstm_hnorm[j],
                jnp.zeros((bp, ml_heads, ml_dqk, ml_dv), F32), jnp.zeros((bp, ml_heads, ml_dqk), F32),
                jnp.zeros((bp, ml_heads), F32),
                batch=bp, seq=tp, chunk=chunk, out_rows=m, out_dtype=BF16)
            pads = [_pad_sample(a, mp, bs, ts) for a in (q, k, v, og)]
            gate_fill = jnp.where(jnp.arange(LANE) < ml_heads, NEG_BIG, 0.0).astype(F32)
            gs = gates[mp:].reshape(bs, ts, LANE)
            gs = jnp.concatenate([gs, jnp.broadcast_to(gate_fill, (bs, SAMPLE_PAD_T - ts, LANE))], axis=1)
            y_s, c_s, n_s, m_s = _mlstm_scan(
                *pads, gs.reshape(bs * SAMPLE_PAD_T, LANE), mlstm_hnorm[j],
                state_mlstm_C[j].astype(F32), state_mlstm_n[j].astype(F32), state_mlstm_m[j].astype(F32),
                batch=bs, seq=SAMPLE_PAD_T, chunk=SAMPLE_PAD_T, out_rows=bs * SAMPLE_PAD_T, out_dtype=BF16)
            y = _merge_sample(y, y_s, mp, bs, ts)
            out_c.append((c_p, c_s))
            out_n.append((n_p, n_s))
            out_m.append((m_p, m_s))
            (x,) = _matmul(y, mlstm_w_out, _epi_residual, extras=[("tile", x)], name="mlstm_out", layer=j)
        else:
            (z,) = _matmul(h, gmlp_w_in, _epi_bias_gelu, extras=[("row", gmlp_b_in[j])], name="gmlp_in", layer=j)
            groups = gmlp_w_s.shape[1]
            lp = min(GMLP_CHUNK, tp)
            ls = min(GMLP_CHUNK, ts)
            reps = GMLP_CHUNK // ls
            w_s_blk = jnp.einsum("ab,gts->gatbs", jnp.eye(reps, dtype=F32), gmlp_w_s[j][:, :ls, :ls])
            w_s_blk = w_s_blk.reshape(groups, GMLP_CHUNK, GMLP_CHUNK)
            w_p = gmlp_w_s[j][:, :lp, :lp]
            w_stack = jnp.stack([w_p, w_s_blk])
            bs_stack = jnp.stack([gmlp_b_s[j][:, :lp].T, jnp.tile(gmlp_b_s[j][:, :ls].T, (reps, 1))])
            y, vn = _gmlp_gate(z, gmlp_vnorm_g[j], gmlp_vnorm_b[j], w_stack, bs_stack,
                               n_prompt_chunks=mp // GMLP_CHUNK)
            out_v.append(vn[mp:].reshape(bs, ts, d))
            (x,) = _matmul(y, gmlp_w_out, _epi_residual, extras=[("tile", x)], name="gmlp_out", layer=j)

        h = _rmsnorm(x, norm_ffn[i], BF16)
        act, tail_p = _ffn_up_prompt(h, ffn_w_up, i, ffn_conv_w[i], ffn_conv_b[i], rows=mp, seq=tp, out_rows=m)
        h_t = h[mp:].reshape(bs, ts, d).transpose(1, 0, 2).reshape(ms, d)
        buf_t = state_ffn_conv[i].astype(F32).transpose(1, 0, 2)
        act_t, tail_s = _ffn_up_sample(h_t, ffn_w_up, i, ffn_conv_w[i], ffn_conv_b[i], buf_t, steps=ts)
        act_s = act_t.reshape(ts, bs, -1).transpose(1, 0, 2).reshape(ms, -1)
        act = lax.dynamic_update_slice(act, act_s, (mp, 0))
        out_conv.append((tail_p, tail_s))
        (x,) = _matmul(act, w_down, _epi_residual, extras=[("tile", x)], name="ffn_down", layer=i)

    y_p = _rmsnorm(x, norm_final, F32, row_start=0, n_rows=mp).reshape(bp, tp, d)
    y_s = _rmsnorm(x, norm_final, F32, row_start=mp, n_rows=ms).reshape(bs, ts, d)
    stack = lambda pairs, idx: jnp.stack([p[idx] for p in pairs])
    return (y_p, y_s,
            s_p_all, s_s_all, stack(out_c, 0), stack(out_c, 1),
            stack(out_n, 0), stack(out_n, 1), stack(out_m, 0), stack(out_m, 1),
            jnp.stack(out_v), stack(out_conv, 0), stack(out_conv, 1))


_forward_jit = jax.jit(_forward)


def kernel(x_prompt, x_sample, state_hgrn_S, state_mlstm_C, state_mlstm_n, state_mlstm_m, state_ffn_conv, norm_mix, norm_ffn, norm_final, hgrn_w_q, hgrn_w_f, hgrn_w_i, hgrn_w_g, hgrn_lb, hgrn_onorm, hgrn_w_o, mlstm_w_q, mlstm_w_k, mlstm_w_v, mlstm_w_og, mlstm_w_if, mlstm_b_if, mlstm_hnorm, mlstm_w_out, gmlp_w_in, gmlp_b_in, gmlp_vnorm_g, gmlp_vnorm_b, gmlp_w_s, gmlp_b_s, gmlp_w_out, ffn_w_up, ffn_conv_w, ffn_conv_b, ffn_w_down):
    return _forward_jit(x_prompt, x_sample, state_hgrn_S, state_mlstm_C, state_mlstm_n, state_mlstm_m, state_ffn_conv,
                        norm_mix, norm_ffn, norm_final,
                        hgrn_w_q, hgrn_w_f, hgrn_w_i, hgrn_w_g, hgrn_lb, hgrn_onorm, hgrn_w_o,
                        mlstm_w_q, mlstm_w_k, mlstm_w_v, mlstm_w_og, mlstm_w_if, mlstm_b_if, mlstm_hnorm, mlstm_w_out,
                        gmlp_w_in, gmlp_b_in, gmlp_vnorm_g, gmlp_vnorm_b, gmlp_w_s, gmlp_b_s, gmlp_w_out,
                        ffn_w_up, ffn_conv_w, ffn_conv_b, ffn_w_down)
```

```python
import functools
import math

import numpy as np
import jax
import jax.numpy as jnp
from jax import lax
from jax.experimental import pallas as pl
from jax.experimental.pallas import tpu as pltpu

F32 = jnp.float32
BF16 = jnp.bfloat16

EPS = 1e-6
GATE_CAP = 15.0
N_MIXERS = 3
HGRN_CHUNK = 64
MLSTM_CHUNK = 128
GMLP_CHUNK = 128
FFN_ROW_CHUNK = 256
SAMPLE_ROWS = 16

LANE = 128
VMEM_PHYSICAL_V7X = 64 * 1024 * 1024
VMEM_BUDGET = VMEM_PHYSICAL_V7X - 8 * 1024 * 1024

MM_TN_MAX = 1024
MM_W_TILE_BYTES = 8 << 20
MM_X_TILE_BYTES = 6 << 20
MM_VMEM_TARGET = 44 << 20


def _nbytes(shape, dtype):
    return int(np.prod(shape)) * jnp.dtype(dtype).itemsize


def _cparams(semantics, block_bytes, temp_bytes=0):
    need = 2 * block_bytes + temp_bytes + (4 << 20)
    return pltpu.CompilerParams(dimension_semantics=semantics,
                                vmem_limit_bytes=int(min(max(need, 16 << 20), VMEM_BUDGET)))


def _pick_tile(n, target, align):
    best = None
    for d in range(align, min(n, target) + 1, align):
        if n % d == 0:
            best = d
    return best if best is not None else n


def _dot(a, b):
    return jnp.dot(a, b, preferred_element_type=F32)


def _dot_nt(a, b):
    return lax.dot_general(a, b, (((1,), (1,)), ((), ())), preferred_element_type=F32)


def _dot_tn(a, b):
    return lax.dot_general(a, b, (((0,), (0,)), ((), ())), preferred_element_type=F32)


def _log_sigmoid(x):
    return -(jnp.maximum(-x, 0.0) + jnp.log1p(jnp.exp(-jnp.abs(x))))


def _split2(x):
    hi = x.astype(BF16)
    lo = (x - hi.astype(F32)).astype(BF16)
    return hi, lo


def _split3(x):
    hi = x.astype(BF16)
    r = x - hi.astype(F32)
    mid = r.astype(BF16)
    lo = (r - mid.astype(F32)).astype(BF16)
    return hi, mid, lo


def _rmsnorm_kernel(x_ref, g_ref, o_ref):
    x = x_ref[...]
    y = x * lax.rsqrt(jnp.mean(x * x, axis=-1, keepdims=True) + EPS)
    o_ref[...] = (y * g_ref[...]).astype(o_ref.dtype)


def _rmsnorm(x, g, out_dtype, row_start=0, n_rows=None):
    m_all, d = x.shape
    n_rows = m_all - row_start if n_rows is None else n_rows
    tm = _pick_tile(math.gcd(n_rows, row_start) if row_start else n_rows, 544, 16)
    off = row_start // tm
    blk = _nbytes((tm, d), F32) + _nbytes((tm, d), out_dtype)
    return pl.pallas_call(
        _rmsnorm_kernel,
        grid=(n_rows // tm,),
        in_specs=[pl.BlockSpec((tm, d), lambda i: (i + off, 0)),
                  pl.BlockSpec((1, d), lambda i: (0, 0))],
        out_specs=pl.BlockSpec((tm, d), lambda i: (i, 0)),
        out_shape=jax.ShapeDtypeStruct((n_rows, d), out_dtype),
        compiler_params=_cparams(("parallel",), blk, _nbytes((tm, d), F32)),
        name="rmsnorm",
    )(x, g.reshape(1, d).astype(F32))


def _mm_kernel(*refs, epi, n_extra, n_out, cast_w):
    x_ref, w_ref = refs[0], refs[1]
    extras = refs[2:2 + n_extra]
    outs = refs[2 + n_extra:2 + n_extra + n_out]
    if cast_w:
        wb_ref = refs[2 + n_extra + n_out]

        @pl.when(pl.program_id(1) == 0)
        def _():
            wb_ref[...] = w_ref[...].astype(BF16)
    else:
        wb_ref = w_ref

    acc = _dot(x_ref[...], wb_ref[...])
    vals = epi(acc, *(e[...] for e in extras))
    for o_ref, val in zip(outs, vals):
        o_ref[...] = val.astype(o_ref.dtype)


def _matmul(x, w, epi, extras=(), out_dtypes=(F32,), name="matmul", layer=None):
    m, k = x.shape
    n = w.shape[-1]
    cast_w = w.dtype != BF16
    w_item = jnp.dtype(w.dtype).itemsize
    tn = _pick_tile(n, max(LANE, min(MM_TN_MAX, MM_W_TILE_BYTES // (w_item * k) // LANE * LANE)), LANE)
    n_tile_io = len(out_dtypes) + sum(kind == "tile" for kind, _ in extras)

    def vmem_need(tm):
        blocks = tm * k * 2 + k * tn * w_item + n_tile_io * tm * tn * 4
        return 2 * blocks + (k * tn * 2 if cast_w else 0) + 3 * tm * tn * 4

    tm = _pick_tile(m, 16, 16)
    for cand in sorted((d for d in range(16, m + 1, 16) if m % d == 0), reverse=True):
        if cand * k * 2 <= MM_X_TILE_BYTES and vmem_need(cand) <= MM_VMEM_TARGET:
            tm = cand
            break
    w_spec = (pl.BlockSpec((k, tn), lambda j, i: (0, j)) if layer is None else
              pl.BlockSpec((None, k, tn), lambda j, i: (layer, 0, j)))
    in_specs = [pl.BlockSpec((tm, k), lambda j, i: (i, 0)), w_spec]
    args = [x, w]
    blk = _nbytes((tm, k), x.dtype) + _nbytes((k, tn), w.dtype)
    for kind, arr in extras:
        if kind == "row":
            in_specs.append(pl.BlockSpec((1, tn), lambda j, i: (0, j)))
            args.append(arr.reshape(1, n).astype(F32))
            blk += _nbytes((8, tn), F32)
        else:
            in_specs.append(pl.BlockSpec((tm, tn), lambda j, i: (i, j)))
            args.append(arr)
            blk += _nbytes((tm, tn), arr.dtype)
    out_specs = [pl.BlockSpec((tm, tn), lambda j, i: (i, j)) for _ in out_dtypes]
    out_shape = [jax.ShapeDtypeStruct((m, n), dt) for dt in out_dtypes]
    blk += sum(_nbytes((tm, tn), dt) for dt in out_dtypes)
    outs = pl.pallas_call(
        functools.partial(_mm_kernel, epi=epi, n_extra=len(extras), n_out=len(out_dtypes), cast_w=cast_w),
        grid=(n // tn, m // tm),
        in_specs=in_specs, out_specs=out_specs, out_shape=out_shape,
        scratch_shapes=[pltpu.VMEM((k, tn), BF16)] if cast_w else [],
        compiler_params=_cparams(("parallel", "arbitrary"), blk,
                                 (_nbytes((k, tn), BF16) if cast_w else 0) + 3 * _nbytes((tm, tn), F32)),
        name=name,
    )(*args)
    return outs


def _epi_id(acc):
    return (acc,)


def _epi_silu(acc):
    return (acc * jax.nn.sigmoid(acc),)


def _epi_sigmoid(acc):
    return (jax.nn.sigmoid(acc),)


def _epi_scale(acc, *, scale):
    return (acc * scale,)


def _epi_residual(acc, res):
    return (res + acc,)


def _epi_hgrn_forget(acc, lb):
    lbh = jnp.maximum(lb, 0.0)
    a = jnp.log(lbh)
    c = jnp.log1p(-lbh) + _log_sigmoid(acc)
    logf = jnp.maximum(a, c) + jnp.log1p(jnp.exp(-jnp.abs(a - c)))
    kk = (1.0 - lbh) * jax.nn.sigmoid(-acc)
    return logf, kk


def _epi_bias_gelu(acc, b):
    x = acc + b
    return (0.5 * x * (1.0 + lax.erf(x * (2.0 ** -0.5))),)


def _epi_mlstm_gates(acc, b, *, heads):
    gates = GATE_CAP * jnp.tanh((acc + b) / GATE_CAP)
    col = lax.broadcasted_iota(jnp.int32, acc.shape, 1)
    return (jnp.where(col < heads, gates, _log_sigmoid(gates)),)


def _hgrn_levels(chunk, t_real):
    return tuple(m for m in (2 ** p for p in range(int(math.log2(chunk)) - 1, -1, -1)) if m < t_real)


def _hgrn_sum_matrix(chunk, levels, t_len):
    t = np.arange(chunk)[:, None]
    r = np.arange(chunk)[None, :]
    same = (t // t_len) == (r // t_len)
    mats = [(r <= t) & same, (r > t) & same]
    for m in levels:
        mid = (t // (2 * m)) * (2 * m) + m - 1
        second = (t % (2 * m)) >= m
        mats.append(np.where(second, (r > mid) & (r <= t), (r > t) & (r <= mid)))
    return np.concatenate(mats, axis=0).astype(np.float32)


def _hgrn_kernel(*refs, heads, chunk, levels, dk, group, n_aliased, slot):
    q_ref, lf_ref, k_ref, v_ref, g_ref, onorm_ref, d_ref, s0_ref = refs[:8]
    y_ref, s_ref, o_scr = refs[8 + n_aliased:]
    c = pl.program_id(1)
    t_len = chunk // group
    dv = s_ref.shape[-1]

    @pl.when(c == 0)
    def _():
        for other in range(s_ref.shape[0]):
            if other != slot:
                s_ref[other] = jnp.zeros(s_ref.shape[1:], F32)
        s_ref[slot] = s0_ref[0]

    row = lax.broadcasted_iota(jnp.int32, (chunk, 1), 0)
    in_seq = [None] if group == 1 else [(row >> int(math.log2(t_len))) == s for s in range(group)]
    seq_ones = [jnp.ones((chunk, dv), BF16) if mask is None else
                jnp.where(mask, jnp.ones((chunk, dv), F32), 0.0).astype(BF16) for mask in in_seq]
    r2 = lax.broadcasted_iota(jnp.int32, (chunk, chunk), 0)
    c2 = lax.broadcasted_iota(jnp.int32, (chunk, chunk), 1)

    lf_hi, lf_lo = _split2(lf_ref[...])
    dmat = d_ref[...]
    xs = _dot(dmat, lf_hi) + _dot(dmat, lf_lo)
    q = q_ref[...]
    k = k_ref[...]
    vb = v_ref[...].astype(BF16)
    ex_b = jnp.exp(xs[0:chunk])
    q_in = q * ex_b
    k_out = k * jnp.exp(xs[chunk:2 * chunk])
    q_in = [(q_in if mask is None else jnp.where(mask, q_in, 0.0)).astype(BF16) for mask in in_seq]
    k_out = [(k_out if mask is None else jnp.where(mask, k_out, 0.0)).astype(BF16) for mask in in_seq]
    qs = [q.astype(BF16)]
    ks = [k.astype(BF16)]
    masks = [r2 == c2]
    for li, m in enumerate(levels):
        ex = jnp.exp(xs[(2 + li) * chunk:(3 + li) * chunk])
        second = (row & m) != 0
        qs.append(jnp.where(second, q * ex, 0.0).astype(BF16))
        ks.append(jnp.where(second, 0.0, k * ex).astype(BF16))
        shift = int(math.log2(2 * m))
        masks.append(None if 2 * m == chunk else (r2 >> shift) == (c2 >> shift))

    scores = []
    for h in range(heads):
        cols = slice(h * dk, (h + 1) * dk)
        a = None
        for qs_l, ks_l, mask in zip(qs, ks, masks):
            al = _dot_nt(qs_l[:, cols], ks_l[:, cols])
            if mask is not None:
                al = jnp.where(mask, al, 0.0)
            a = al if a is None else a + al
        scores.append(a.astype(BF16))

    for h in range(heads):
        cols = slice(h * dk, (h + 1) * dk)
        o = _dot(scores[h], vb[:, cols])
        for s in range(group):
            st = s_ref[slot, s, h]
            o = o + _dot(q_in[s][:, cols], st.astype(BF16))
            decay = jnp.exp(_dot_tn(lf_hi[:, cols], seq_ones[s]) + _dot_tn(lf_lo[:, cols], seq_ones[s]))
            s_ref[slot, s, h] = st * decay + _dot_tn(k_out[s][:, cols], vb[:, cols])
        o_scr[:, cols] = o

    o = o_scr[...]
    y = o * lax.rsqrt(jnp.mean(o * o, axis=-1, keepdims=True) + EPS) * onorm_ref[...]
    y_ref[...] = (y * g_ref[...]).astype(y_ref.dtype)


def _hgrn_scan(q, lf, k, v, g, onorm, s0_all, layer, s_out_prev, y_prev, *, row_start, batch, seq, chunk, group):
    d = q.shape[1]
    heads, dk, dv = s0_all.shape[2:]
    t_chunk = chunk // group
    n_chunks = seq // t_chunk
    assert group == 1 or n_chunks == 1
    assert row_start % chunk == 0 and batch % group == 0
    levels = _hgrn_levels(chunk, t_chunk)
    dmat = jnp.asarray(_hgrn_sum_matrix(chunk, levels, t_chunk), dtype=BF16)
    first_blk = row_start // chunk
    tok = pl.BlockSpec((chunk, d), lambda b, c: (first_blk + b * n_chunks + c, 0))
    in_layer = min(layer, s0_all.shape[0] - 1)
    st_in = pl.BlockSpec((1, group, heads, dk, dv), lambda b, c: (in_layer, b, 0, 0, 0))
    n_layers = s_out_prev if isinstance(s_out_prev, int) else s_out_prev.shape[0]
    fresh_state = isinstance(s_out_prev, int)
    slots = n_layers if fresh_state else 1
    st_out = pl.BlockSpec((slots, group, heads, dk, dv), lambda b, c: (0 if fresh_state else layer, b, 0, 0, 0))
    in_specs = [tok, tok, tok, tok, tok,
                pl.BlockSpec((1, d), lambda b, c: (0, 0)),
                pl.BlockSpec(dmat.shape, lambda b, c: (0, 0)),
                st_in]
    args = [q, lf, k, v, g, onorm.reshape(1, d).astype(F32), dmat, s0_all]
    aliases = {}
    for out_idx, prev in ((0, y_prev), (1, s_out_prev)):
        if not isinstance(prev, int):
            aliases[len(args)] = out_idx
            in_specs.append(pl.BlockSpec(memory_space=pl.ANY))
            args.append(prev)
    y_rows = y_prev if isinstance(y_prev, int) else y_prev.shape[0]
    blk = (4 * _nbytes((chunk, d), F32) + 2 * _nbytes((chunk, d), BF16) + (1 + slots) * _nbytes((group, heads, dk, dv), F32)
           + _nbytes(dmat.shape, BF16))
    n_exp = 2 + len(levels)
    y, s = pl.pallas_call(
        functools.partial(_hgrn_kernel, heads=heads, chunk=chunk, levels=levels, dk=dk, group=group,
                          n_aliased=len(aliases), slot=layer if fresh_state else 0),
        grid=(batch // group, n_chunks),
        in_specs=in_specs,
        out_specs=[tok, st_out],
        out_shape=[jax.ShapeDtypeStruct((y_rows, d), BF16),
                   jax.ShapeDtypeStruct((n_layers, batch, heads, dk, dv), F32)],
        scratch_shapes=[pltpu.VMEM((chunk, d), F32)],
        input_output_aliases=aliases,
        compiler_params=_cparams(("parallel", "arbitrary"), blk, (3 * n_exp + 8) * _nbytes((chunk, d), F32)),
        name="hgrn_scan",
    )(*args)
    return y, s


def _mlstm_kernel(*refs, heads, chunk, dqk, dv, group, n_aliased):
    q_ref, k_ref, v_ref, og_ref, gt_ref, hn_ref, tri_ref, sel_ref, c0_ref, n0_ref, m0_ref = refs[:11]
    y_ref, c_ref, n_ref, m_ref = refs[11 + n_aliased:]
    c = pl.program_id(1)
    t_len = chunk // group

    @pl.when(c == 0)
    def _():
        c_ref[...] = c0_ref[...]
        n_ref[...] = n0_ref[...]
        m_ref[...] = m0_ref[...]

    gates = gt_ref[...]
    tri = tri_ref[...]
    sel = sel_ref[...]
    g3 = _split3(gates)
    cum = sum(_dot(tri, p) for p in g3)
    gates_t = sum(_dot_nt(sel, p) for p in g3)
    cum_t = sum(_dot_nt(sel, p) for p in _split3(cum))
    row = lax.broadcasted_iota(jnp.int32, (chunk, 1), 0)
    r2 = lax.broadcasted_iota(jnp.int32, (chunk, chunk), 0)
    c2 = lax.broadcasted_iota(jnp.int32, (chunk, chunk), 1)
    causal = c2 <= r2
    in_seq = [None]
    if group > 1:
        shift = int(math.log2(t_len))
        causal = causal & ((r2 >> shift) == (c2 >> shift))
        in_seq = [(row >> shift) == s for s in range(group)]

    def per_row(vals):
        if group == 1:
            return vals[0]
        out = jnp.where(in_seq[0], vals[0], 0.0)
        for s in range(1, group):
            out = jnp.where(in_seq[s], vals[s], out)
        return out

    lane = lax.broadcasted_iota(jnp.int32, (1, LANE), 1)
    m_rows = [m_ref[s] for s in range(group)]
    m_new = list(m_rows)
    for h in range(heads):
        qc = slice(h * dqk, (h + 1) * dqk)
        vc = slice(h * dv, (h + 1) * dv)
        qh = q_ref[:, qc]
        kh = k_ref[:, qc]
        vb = v_ref[:, vc].astype(BF16)
        qb = qh.astype(BF16)
        b_c = cum[:, heads + h:heads + h + 1]
        ig_c = gates[:, h:h + 1]
        b_r = cum_t[heads + h:heads + h + 1, :]
        ig_r = gates_t[h:h + 1, :]
        m_h = [m_rows[s][:, h:h + 1] for s in range(group)]
        dlog = jnp.where(causal, b_c - b_r + ig_r, -jnp.inf)
        inter = b_c + per_row(m_h)
        mt = jnp.maximum(inter, jnp.max(dlog, axis=-1, keepdims=True))
        wts = jnp.exp(dlog - mt) * _dot_nt(qb, kh.astype(BF16))
        sc = jnp.exp(inter - mt)
        c_h = [c_ref[s, h] for s in range(group)]
        n_h = [n_ref[s, h:h + 1, :] for s in range(group)]
        if group == 1:
            q_c = _dot(qb, c_h[0].astype(BF16))
        else:
            q_c = sum(_dot(jnp.where(in_seq[s], qh, 0.0).astype(BF16), c_h[s].astype(BF16)) for s in range(group))
        num = sc * q_c + _dot(wts.astype(BF16), vb)
        den = sc * jnp.sum(qh * per_row(n_h), axis=-1, keepdims=True) + jnp.sum(wts, axis=-1, keepdims=True)
        out = num / jnp.maximum(jnp.abs(den), jnp.exp(-mt))
        for s in range(group):
            last = (s + 1) * t_len - 1
            m_last = mt[last:last + 1, :]
            b_last = b_c[last:last + 1, :]
            sc_state = jnp.exp(b_last + m_h[s] - m_last)
            w_k = jnp.exp(b_last - b_c + ig_c - m_last)
            if group > 1:
                w_k = jnp.where(in_seq[s], w_k, 0.0)
            kw = w_k * kh
            c_ref[s, h] = sc_state * c_h[s] + _dot_tn(kw.astype(BF16), vb)
            n_ref[s, h:h + 1, :] = sc_state * n_h[s] + jnp.sum(kw, axis=0, keepdims=True)
            m_new[s] = jnp.where(lane == h, m_last, m_new[s])
        y = out * lax.rsqrt(jnp.mean(out * out, axis=-1, keepdims=True) + EPS) * hn_ref[:, vc]
        y_ref[:, vc] = (y * og_ref[:, vc]).astype(y_ref.dtype)
    for s in range(group):
        m_ref[s] = m_new[s]


def _mlstm_scan(q, k, v, og, gates, hnorm, c0, n0, m0, y_prev, *, row_start, batch, seq, chunk, group):
    heads, dqk, dv = c0.shape[1], c0.shape[2], c0.shape[3]
    dq_all, dv_all = q.shape[1], v.shape[1]
    t_chunk = chunk // group
    n_chunks = seq // t_chunk
    assert 2 * heads <= 16 and (group == 1 or n_chunks == 1)
    assert row_start % chunk == 0 and batch % group == 0
    idx = np.arange(chunk)
    same_seq = (idx[:, None] // t_chunk) == (idx[None, :] // t_chunk)
    tri = jnp.asarray(np.tril(np.ones((chunk, chunk), np.float32)) * same_seq, dtype=BF16)
    sel = jnp.asarray(np.eye(16, LANE, dtype=np.float32), dtype=BF16)
    m0p = jnp.pad(m0, ((0, 0), (0, LANE - heads))).reshape(batch, 1, LANE)
    first_blk = row_start // chunk
    tq = pl.BlockSpec((chunk, dq_all), lambda b, c: (first_blk + b * n_chunks + c, 0))
    tv = pl.BlockSpec((chunk, dv_all), lambda b, c: (first_blk + b * n_chunks + c, 0))
    tg = pl.BlockSpec((chunk, LANE), lambda b, c: (first_blk + b * n_chunks + c, 0))
    sc_ = pl.BlockSpec((group, heads, dqk, dv), lambda b, c: (b, 0, 0, 0))
    sn_ = pl.BlockSpec((group, heads, dqk), lambda b, c: (b, 0, 0))
    sm_ = pl.BlockSpec((group, 1, LANE), lambda b, c: (b, 0, 0))
    in_specs = [tq, tq, tv, tv, tg,
                pl.BlockSpec((1, dv_all), lambda b, c: (0, 0)),
                pl.BlockSpec(tri.shape, lambda b, c: (0, 0)),
                pl.BlockSpec(sel.shape, lambda b, c: (0, 0)),
                sc_, sn_, sm_]
    args = [q, k, v, og, gates, hnorm.reshape(1, dv_all).astype(F32), tri, sel, c0, n0, m0p]
    aliases = {}
    if not isinstance(y_prev, int):
        aliases[len(args)] = 0
        in_specs.append(pl.BlockSpec(memory_space=pl.ANY))
        args.append(y_prev)
    y_rows = y_prev if isinstance(y_prev, int) else y_prev.shape[0]
    blk = (2 * _nbytes((chunk, dq_all), F32) + _nbytes((chunk, dv_all), v.dtype) + _nbytes((chunk, dv_all), F32)
           + _nbytes((chunk, dv_all), BF16) + 2 * _nbytes((group,) + c0.shape[1:], F32))
    y, c_out, n_out, m_out = pl.pallas_call(
        functools.partial(_mlstm_kernel, heads=heads, chunk=chunk, dqk=dqk, dv=dv, group=group,
                          n_aliased=len(aliases)),
        grid=(batch // group, n_chunks),
        in_specs=in_specs,
        out_specs=[tv, sc_, sn_, sm_],
        out_shape=[jax.ShapeDtypeStruct((y_rows, dv_all), BF16),
                   jax.ShapeDtypeStruct(c0.shape, F32),
                   jax.ShapeDtypeStruct(n0.shape, F32),
                   jax.ShapeDtypeStruct((batch, 1, LANE), F32)],
        input_output_aliases=aliases,
        compiler_params=_cparams(("parallel", "arbitrary"), blk, 8 * _nbytes((chunk, dv_all), F32)),
        name="mlstm_scan",
    )(*args)
    return y, c_out, n_out, m_out[:, 0, :heads]


def _gmlp_kernel(u_ref, v_ref, vg_ref, vb_ref, w_ref, bs_ref, o_ref, vn_ref, *, groups, gd):
    v = v_ref[...]
    mu = jnp.mean(v, axis=-1, keepdims=True)
    xc = v - mu
    vn = xc * lax.rsqrt(jnp.mean(xc * xc, axis=-1, keepdims=True) + EPS) * vg_ref[...] + vb_ref[...]
    vn_ref[...] = vn
    n = v.shape[0]
    causal = (lax.broadcasted_iota(jnp.int32, (n, n), 1) <= lax.broadcasted_iota(jnp.int32, (n, n), 0))
    bs = bs_ref[0]
    for g in range(groups):
        cols = slice(g * gd, (g + 1) * gd)
        wg = jnp.where(causal, w_ref[0, g], 0.0).astype(BF16)
        mix = _dot(wg, vn[:, cols].astype(BF16)) + bs[:, g:g + 1]
        o_ref[:, cols] = (u_ref[:, cols] * mix).astype(o_ref.dtype)


def _gmlp_gate(z, vg, vb, w_stack, bs_stack, *, n_prompt_chunks):
    m, d2 = z.shape
    d = d2 // 2
    groups, chunk = w_stack.shape[1], w_stack.shape[2]
    gd = d // groups
    assert m % chunk == 0

    def which(i):
        return jnp.minimum(i // n_prompt_chunks, 1)

    blk = (2 * _nbytes((chunk, d), F32) + _nbytes((chunk, d), BF16) + _nbytes((chunk, d), F32)
           + _nbytes((groups, chunk, chunk), F32))
    o, vn = pl.pallas_call(
        functools.partial(_gmlp_kernel, groups=groups, gd=gd),
        grid=(m // chunk,),
        in_specs=[pl.BlockSpec((chunk, d), lambda i: (i, 0)),
                  pl.BlockSpec((chunk, d), lambda i: (i, 1)),
                  pl.BlockSpec((1, d), lambda i: (0, 0)),
                  pl.BlockSpec((1, d), lambda i: (0, 0)),
                  pl.BlockSpec((1, groups, chunk, chunk), lambda i: (which(i), 0, 0, 0)),
                  pl.BlockSpec((1, chunk, groups), lambda i: (which(i), 0, 0))],
        out_specs=[pl.BlockSpec((chunk, d), lambda i: (i, 0)),
                   pl.BlockSpec((chunk, d), lambda i: (i, 0))],
        out_shape=[jax.ShapeDtypeStruct((m, d), BF16), jax.ShapeDtypeStruct((m, d), F32)],
        compiler_params=_cparams(("parallel",), blk, 4 * _nbytes((chunk, d), F32)),
        name="gmlp_gate",
    )(z, z, vg.reshape(1, d).astype(F32), vb.reshape(1, d).astype(F32), w_stack, bs_stack)
    return o, vn


def _causal_conv(tap, cw_ref, cb_ref, conv_w):
    y = cb_ref[...] + cw_ref[conv_w - 1:conv_w, :] * tap(0)
    for back in range(1, conv_w):
        y = y + cw_ref[conv_w - 1 - back:conv_w - back, :] * tap(back)
    return y


def _ffn_up_prompt_kernel(h_ref, wa_ref, wg_ref, cwa_ref, cwg_ref, cba_ref, cbg_ref,
                          _, act_ref, tail_a_ref, tail_g_ref, wb_scr, up_scr, *, tiles_per_seq, conv_w, row_chunk):
    i = pl.program_id(1)
    tm = h_ref.shape[0]
    n_chunks = tm // row_chunk

    @pl.when(i == 0)
    def _():
        wb_scr[0] = wa_ref[...].astype(BF16)
        wb_scr[1] = wg_ref[...].astype(BF16)

    @pl.when(lax.rem(i, tiles_per_seq) == 0)
    def _():
        up_scr[:, 0:8, :] = jnp.zeros((2, 8, up_scr.shape[2]), F32)

    def multiply(c):
        hb = h_ref[c * row_chunk:(c + 1) * row_chunk, :]
        for idx in range(2):
            up_scr[idx, 8 + c * row_chunk:8 + (c + 1) * row_chunk, :] = _dot(hb, wb_scr[idx])

    def finish(c):
        lo = 8 + c * row_chunk
        ya = _causal_conv(lambda back: up_scr[0, lo - back:lo - back + row_chunk, :], cwa_ref, cba_ref, conv_w)
        yg = _causal_conv(lambda back: up_scr[1, lo - back:lo - back + row_chunk, :], cwg_ref, cbg_ref, conv_w)
        act_ref[c * row_chunk:(c + 1) * row_chunk, :] = (ya * (yg * jax.nn.sigmoid(yg))).astype(act_ref.dtype)

    multiply(0)
    for c in range(1, n_chunks):
        multiply(c)
        finish(c - 1)
    finish(n_chunks - 1)
    for idx, tail_ref in ((0, tail_a_ref), (1, tail_g_ref)):
        last = up_scr[idx, tm:tm + 8, :]
        tail_ref[0] = last
        up_scr[idx, 0:8, :] = last


def _ffn_up_prompt(h, w_up, layer, cw, cb, act_prev, *, rows, seq):
    _, k, f2 = w_up.shape
    f = f2 // 2
    conv_w = cw.shape[0]
    tm = _pick_tile(seq, 1024, 16)
    tf = _pick_tile(f, 256, LANE)
    nf = f // tf
    n_tiles = rows // tm
    row_chunk = _pick_tile(tm, FFN_ROW_CHUNK, 16)
    blk = (_nbytes((tm, k), BF16) + 2 * _nbytes((k, tf), F32) + _nbytes((tm, tf), BF16)
           + 2 * _nbytes((8, tf), F32) + 6 * _nbytes((8, tf), F32))
    tail = jax.ShapeDtypeStruct((n_tiles, 8, f), F32)
    act, tail_a, tail_g = pl.pallas_call(
        functools.partial(_ffn_up_prompt_kernel, tiles_per_seq=seq // tm, conv_w=conv_w, row_chunk=row_chunk),
        grid=(nf, n_tiles),
        in_specs=[pl.BlockSpec((tm, k), lambda j, i: (i, 0)),
                  pl.BlockSpec((None, k, tf), lambda j, i: (layer, 0, j)),
                  pl.BlockSpec((None, k, tf), lambda j, i: (layer, 0, j + nf)),
                  pl.BlockSpec((conv_w, tf), lambda j, i: (0, j)),
                  pl.BlockSpec((conv_w, tf), lambda j, i: (0, j + nf)),
                  pl.BlockSpec((1, tf), lambda j, i: (0, j)),
                  pl.BlockSpec((1, tf), lambda j, i: (0, j + nf)),
                  pl.BlockSpec(memory_space=pl.ANY)],
        out_specs=[pl.BlockSpec((tm, tf), lambda j, i: (i, j)),
                   pl.BlockSpec((1, 8, tf), lambda j, i: (i, 0, j)),
                   pl.BlockSpec((1, 8, tf), lambda j, i: (i, 0, j))],
        out_shape=[jax.ShapeDtypeStruct(act_prev.shape, BF16), tail, tail],
        input_output_aliases={7: 0},
        scratch_shapes=[pltpu.VMEM((2, k, tf), BF16), pltpu.VMEM((2, 8 + tm, tf), F32)],
        compiler_params=_cparams(("parallel", "arbitrary"), blk,
                                 2 * _nbytes((k, tf), BF16) + 2 * _nbytes((8 + tm, tf), F32)
                                 + 6 * _nbytes((row_chunk, tf), F32)),
        name="ffn_up_conv_prompt",
    )(h, w_up, w_up, cw, cw, cb.reshape(1, f2), cb.reshape(1, f2), act_prev)
    tiles_per_seq = seq // tm
    keep = slice(8 - (conv_w - 1), 8)
    tails = jnp.concatenate([tail_a[tiles_per_seq - 1::tiles_per_seq, keep], tail_g[tiles_per_seq - 1::tiles_per_seq, keep]],
                            axis=-1)
    return act, tails


def _ffn_up_sample_kernel(h_ref, wa_ref, wg_ref, ba_ref, bg_ref, cwa_ref, cwg_ref, cba_ref, cbg_ref,
                          act_ref, upa_ref, upg_ref, *, steps, conv_w):
    bs = h_ref.shape[0] // steps
    hb = h_ref[...]

    def branch(w_ref, buf_ref, cw_ref, cb_ref, up_ref):
        up = _dot(hb, w_ref[...].astype(BF16))
        up_ref[...] = up

        def at(t):
            return up[t * bs:(t + 1) * bs, :] if t >= 0 else buf_ref[conv_w - 1 + t]

        return [_causal_conv(lambda back, t=t: at(t - back), cw_ref, cb_ref, conv_w) for t in range(steps)]

    ya = branch(wa_ref, ba_ref, cwa_ref, cba_ref, upa_ref)
    yg = branch(wg_ref, bg_ref, cwg_ref, cbg_ref, upg_ref)
    for t in range(steps):
        act_ref[t * bs:(t + 1) * bs, :] = (ya[t] * (yg[t] * jax.nn.sigmoid(yg[t]))).astype(act_ref.dtype)


def _ffn_up_sample(h_t, w_up, layer, cw, cb, buf_t, *, steps):
    _, k, f2 = w_up.shape
    f = f2 // 2
    conv_w = cw.shape[0]
    ms = h_t.shape[0]
    bs = ms // steps
    assert steps >= conv_w - 1 and bs % 16 == 0
    tf = _pick_tile(f, 256, LANE)
    nf = f // tf
    blk = (_nbytes((ms, k), BF16) + 2 * _nbytes((k, tf), F32) + 2 * _nbytes((conv_w - 1, bs, tf), F32)
           + _nbytes((ms, tf), BF16) + 2 * _nbytes((ms, tf), F32) + 6 * _nbytes((8, tf), F32))
    up_shape = jax.ShapeDtypeStruct((ms, f), F32)
    act, up_a, up_g = pl.pallas_call(
        functools.partial(_ffn_up_sample_kernel, steps=steps, conv_w=conv_w),
        grid=(nf,),
        in_specs=[pl.BlockSpec((ms, k), lambda j: (0, 0)),
                  pl.BlockSpec((None, k, tf), lambda j: (layer, 0, j)),
                  pl.BlockSpec((None, k, tf), lambda j: (layer, 0, j + nf)),
                  pl.BlockSpec((conv_w - 1, bs, tf), lambda j: (0, 0, j)),
                  pl.BlockSpec((conv_w - 1, bs, tf), lambda j: (0, 0, j + nf)),
                  pl.BlockSpec((conv_w, tf), lambda j: (0, j)),
                  pl.BlockSpec((conv_w, tf), lambda j: (0, j + nf)),
                  pl.BlockSpec((1, tf), lambda j: (0, j)),
                  pl.BlockSpec((1, tf), lambda j: (0, j + nf))],
        out_specs=[pl.BlockSpec((ms, tf), lambda j: (0, j))] * 3,
        out_shape=[jax.ShapeDtypeStruct((ms, f), BF16), up_shape, up_shape],
        compiler_params=_cparams(("parallel",), blk, 2 * _nbytes((k, tf), BF16) + 8 * _nbytes((ms, tf), F32)),
        name="ffn_up_conv_sample",
    )(h_t, w_up, w_up, buf_t, buf_t, cw, cw, cb.reshape(1, f2), cb.reshape(1, f2))
    keep = slice((steps - (conv_w - 1)) * bs, ms)
    up_tail = jnp.concatenate([up_a[keep], up_g[keep]], axis=-1).reshape(conv_w - 1, bs, f2).transpose(1, 0, 2)
    return act, up_tail


def _forward(x_prompt, x_sample, state_hgrn_S, state_mlstm_C, state_mlstm_n, state_mlstm_m, state_ffn_conv,
             norm_mix, norm_ffn, norm_final,
             hgrn_w_q, hgrn_w_f, hgrn_w_i, hgrn_w_g, hgrn_lb, hgrn_onorm, hgrn_w_o,
             mlstm_w_q, mlstm_w_k, mlstm_w_v, mlstm_w_og, mlstm_w_if, mlstm_b_if, mlstm_hnorm, mlstm_w_out,
             gmlp_w_in, gmlp_b_in, gmlp_vnorm_g, gmlp_vnorm_b, gmlp_w_s, gmlp_b_s, gmlp_w_out,
             ffn_w_up, ffn_conv_w, ffn_conv_b, ffn_w_down):
    bp, tp, d = x_prompt.shape
    bs, ts, _ = x_sample.shape
    mp, ms = bp * tp, bs * ts
    m = mp + ms
    depth = norm_mix.shape[0]
    hg_heads, hg_dk, hg_dv = state_hgrn_S.shape[2:]
    ml_heads, ml_dqk, ml_dv = state_mlstm_C.shape[2:]

    x = jnp.concatenate([x_prompt.reshape(mp, d), x_sample.reshape(ms, d)], axis=0)

    lb = jax.nn.softmax(hgrn_lb.astype(F32), axis=0)
    lbs = jnp.cumsum(lb, axis=0) - lb[0]

    w_down = ffn_w_down.astype(BF16)
    act = jnp.zeros((m, ffn_w_down.shape[1]), BF16)
    out_c, out_n, out_m, out_v, out_conv = [], [], [], [], []
    s_p_all = s_s_all = state_hgrn_S.shape[0]
    for i in range(depth):
        j = i // N_MIXERS
        kind = i % N_MIXERS
        h = _rmsnorm(x, norm_mix[i], BF16)
        if kind == 0:
            (q,) = _matmul(h, hgrn_w_q, _epi_silu, name="hgrn_q", layer=j)
            lf, k = _matmul(h, hgrn_w_f, _epi_hgrn_forget, extras=[("row", lbs[j])],
                            out_dtypes=(F32, F32), name="hgrn_f", layer=j)
            (v,) = _matmul(h, hgrn_w_i, _epi_id, out_dtypes=(BF16,), name="hgrn_i", layer=j)
            (g,) = _matmul(h, hgrn_w_g, _epi_sigmoid, name="hgrn_g", layer=j)
            y, s_p_all = _hgrn_scan(q, lf, k, v, g, hgrn_onorm[j],
                                    jnp.zeros((1, bp, hg_heads, hg_dk, hg_dv), F32), j, s_p_all, h,
                                    row_start=0, batch=bp, seq=tp, chunk=min(HGRN_CHUNK, tp), group=1)
            group = max(1, SAMPLE_ROWS // ts)
            y, s_s_all = _hgrn_scan(q, lf, k, v, g, hgrn_onorm[j], state_hgrn_S.astype(F32), j, s_s_all, y,
                                    row_start=mp, batch=bs, seq=ts, chunk=group * ts, group=group)
            (x,) = _matmul(y, hgrn_w_o, _epi_residual, extras=[("tile", x)], name="hgrn_o", layer=j)
        elif kind == 1:
            (q,) = _matmul(h, mlstm_w_q, _epi_id, name="mlstm_q", layer=j)
            (k,) = _matmul(h, mlstm_w_k, functools.partial(_epi_scale, scale=ml_dqk ** -0.5), name="mlstm_k",
                           layer=j)
            (v,) = _matmul(h, mlstm_w_v, _epi_id, out_dtypes=(BF16,), name="mlstm_v", layer=j)
            (og,) = _matmul(h, mlstm_w_og, _epi_sigmoid, name="mlstm_og", layer=j)
            w_if = jnp.pad(mlstm_w_if[j], ((0, 0), (0, LANE - 2 * ml_heads)))
            b_if = jnp.pad(mlstm_b_if[j], (0, LANE - 2 * ml_heads))
            (gates,) = _matmul(h, w_if, functools.partial(_epi_mlstm_gates, heads=ml_heads),
                               extras=[("row", b_if)], name="mlstm_if")
            y, c_p, n_p, m_p = _mlstm_scan(
                q, k, v, og, gates, mlstm_hnorm[j],
                jnp.zeros((bp, ml_heads, ml_dqk, ml_dv), F32), jnp.zeros((bp, ml_heads, ml_dqk), F32),
                jnp.zeros((bp, ml_heads), F32), h,
                row_start=0, batch=bp, seq=tp, chunk=min(MLSTM_CHUNK, tp), group=1)
            group = max(1, SAMPLE_ROWS // ts)
            y, c_s, n_s, m_s = _mlstm_scan(
                q, k, v, og, gates, mlstm_hnorm[j],
                state_mlstm_C[j].astype(F32), state_mlstm_n[j].astype(F32), state_mlstm_m[j].astype(F32), y,
                row_start=mp, batch=bs, seq=ts, chunk=group * ts, group=group)
            out_c.append((c_p, c_s))
            out_n.append((n_p, n_s))
            out_m.append((m_p, m_s))
            (x,) = _matmul(y, mlstm_w_out, _epi_residual, extras=[("tile", x)], name="mlstm_out", layer=j)
        else:
            (z,) = _matmul(h, gmlp_w_in, _epi_bias_gelu, extras=[("row", gmlp_b_in[j])], name="gmlp_in", layer=j)
            groups = gmlp_w_s.shape[1]
            lp = min(GMLP_CHUNK, tp)
            ls = min(GMLP_CHUNK, ts)
            reps = GMLP_CHUNK // ls
            w_s_blk = jnp.einsum("ab,gts->gatbs", jnp.eye(reps, dtype=F32), gmlp_w_s[j][:, :ls, :ls])
            w_s_blk = w_s_blk.reshape(groups, GMLP_CHUNK, GMLP_CHUNK)
            w_p = gmlp_w_s[j][:, :lp, :lp]
            w_stack = jnp.stack([w_p, w_s_blk])
            bs_stack = jnp.stack([gmlp_b_s[j][:, :lp].T, jnp.tile(gmlp_b_s[j][:, :ls].T, (reps, 1))])
            y, vn = _gmlp_gate(z, gmlp_vnorm_g[j], gmlp_vnorm_b[j], w_stack, bs_stack,
                               n_prompt_chunks=mp // GMLP_CHUNK)
            out_v.append(vn[mp:].reshape(bs, ts, d))
            (x,) = _matmul(y, gmlp_w_out, _epi_residual, extras=[("tile", x)], name="gmlp_out", layer=j)

        h = _rmsnorm(x, norm_ffn[i], BF16)
        act, tail_p = _ffn_up_prompt(h, ffn_w_up, i, ffn_conv_w[i], ffn_conv_b[i], act, rows=mp, seq=tp)
        h_t = h[mp:].reshape(bs, ts, d).transpose(1, 0, 2).reshape(ms, d)
        buf_t = state_ffn_conv[i].astype(F32).transpose(1, 0, 2)
        act_t, tail_s = _ffn_up_sample(h_t, ffn_w_up, i, ffn_conv_w[i], ffn_conv_b[i], buf_t, steps=ts)
        act_s = act_t.reshape(ts, bs, -1).transpose(1, 0, 2).reshape(ms, -1)
        act = lax.dynamic_update_slice(act, act_s, (mp, 0))
        out_conv.append((tail_p, tail_s))
        (x,) = _matmul(act, w_down, _epi_residual, extras=[("tile", x)], name="ffn_down", layer=i)

    y_p = _rmsnorm(x, norm_final, F32, row_start=0, n_rows=mp).reshape(bp, tp, d)
    y_s = _rmsnorm(x, norm_final, F32, row_start=mp, n_rows=ms).reshape(bs, ts, d)
    stack = lambda pairs, idx: jnp.stack([p[idx] for p in pairs])
    return (y_p, y_s,
            s_p_all, s_s_all, stack(out_c, 0), stack(out_c, 1),
            stack(out_n, 0), stack(out_n, 1), stack(out_m, 0), stack(out_m, 1),
            jnp.stack(out_v), stack(out_conv, 0), stack(out_conv, 1))


_forward_jit = jax.jit(_forward)


def kernel(x_prompt, x_sample, state_hgrn_S, state_mlstm_C, state_mlstm_n, state_mlstm_m, state_ffn_conv, norm_mix, norm_ffn, norm_final, hgrn_w_q, hgrn_w_f, hgrn_w_i, hgrn_w_g, hgrn_lb, hgrn_onorm, hgrn_w_o, mlstm_w_q, mlstm_w_k, mlstm_w_v, mlstm_w_og, mlstm_w_if, mlstm_b_if, mlstm_hnorm, mlstm_w_out, gmlp_w_in, gmlp_b_in, gmlp_vnorm_g, gmlp_vnorm_b, gmlp_w_s, gmlp_b_s, gmlp_w_out, ffn_w_up, ffn_conv_w, ffn_conv_b, ffn_w_down):
    return _forward_jit(x_prompt, x_sample, state_hgrn_S, state_mlstm_C, state_mlstm_n, state_mlstm_m, state_ffn_conv,
                        norm_mix, norm_ffn, norm_final,
                        hgrn_w_q, hgrn_w_f, hgrn_w_i, hgrn_w_g, hgrn_lb, hgrn_onorm, hgrn_w_o,
                        mlstm_w_q, mlstm_w_k, mlstm_w_v, mlstm_w_og, mlstm_w_if, mlstm_b_if, mlstm_hnorm, mlstm_w_out,
                        gmlp_w_in, gmlp_b_in, gmlp_vnorm_g, gmlp_vnorm_b, gmlp_w_s, gmlp_b_s, gmlp_w_out,
                        ffn_w_up, ffn_conv_w, ffn_conv_b, ffn_w_down)
```

```python
import functools
import math

import numpy as np
import jax
import jax.numpy as jnp
from jax import lax
from jax.experimental import pallas as pl
from jax.experimental.pallas import tpu as pltpu

F32 = jnp.float32
BF16 = jnp.bfloat16

EPS = 1e-6
GATE_CAP = 15.0
N_MIXERS = 3
HGRN_CHUNK = 64
MLSTM_CHUNK = 128
GMLP_CHUNK = 128
FFN_ROW_CHUNK = 256
FFN_TF = 512
SAMPLE_ROWS = 16

LANE = 128
VMEM_PHYSICAL_V7X = 64 * 1024 * 1024
VMEM_BUDGET = VMEM_PHYSICAL_V7X - 8 * 1024 * 1024

MM_TN_MAX = 1024
MM_W_TILE_BYTES = 8 << 20
MM_X_TILE_BYTES = 6 << 20
MM_VMEM_TARGET = 44 << 20


def _nbytes(shape, dtype):
    return int(np.prod(shape)) * jnp.dtype(dtype).itemsize


def _cparams(semantics, block_bytes, temp_bytes=0):
    need = 2 * block_bytes + temp_bytes + (4 << 20)
    return pltpu.CompilerParams(dimension_semantics=semantics,
                                vmem_limit_bytes=int(min(max(need, 16 << 20), VMEM_BUDGET)))


def _pick_tile(n, target, align):
    best = None
    for d in range(align, min(n, target) + 1, align):
        if n % d == 0:
            best = d
    return best if best is not None else n


def _dot(a, b):
    return jnp.dot(a, b, preferred_element_type=F32)


def _dot_nt(a, b):
    return lax.dot_general(a, b, (((1,), (1,)), ((), ())), preferred_element_type=F32)


def _dot_tn(a, b):
    return lax.dot_general(a, b, (((0,), (0,)), ((), ())), preferred_element_type=F32)


def _log_sigmoid(x):
    return -(jnp.maximum(-x, 0.0) + jnp.log1p(jnp.exp(-jnp.abs(x))))


def _split2(x):
    hi = x.astype(BF16)
    lo = (x - hi.astype(F32)).astype(BF16)
    return hi, lo


def _split3(x):
    hi = x.astype(BF16)
    r = x - hi.astype(F32)
    mid = r.astype(BF16)
    lo = (r - mid.astype(F32)).astype(BF16)
    return hi, mid, lo


def _rmsnorm_kernel(x_ref, g_ref, o_ref):
    x = x_ref[...]
    y = x * lax.rsqrt(jnp.mean(x * x, axis=-1, keepdims=True) + EPS)
    o_ref[...] = (y * g_ref[...]).astype(o_ref.dtype)


def _rmsnorm(x, g, out_dtype, row_start=0, n_rows=None):
    m_all, d = x.shape
    n_rows = m_all - row_start if n_rows is None else n_rows
    tm = _pick_tile(math.gcd(n_rows, row_start) if row_start else n_rows, 544, 16)
    off = row_start // tm
    blk = _nbytes((tm, d), F32) + _nbytes((tm, d), out_dtype)
    return pl.pallas_call(
        _rmsnorm_kernel,
        grid=(n_rows // tm,),
        in_specs=[pl.BlockSpec((tm, d), lambda i: (i + off, 0)),
                  pl.BlockSpec((1, d), lambda i: (0, 0))],
        out_specs=pl.BlockSpec((tm, d), lambda i: (i, 0)),
        out_shape=jax.ShapeDtypeStruct((n_rows, d), out_dtype),
        compiler_params=_cparams(("parallel",), blk, _nbytes((tm, d), F32)),
        name="rmsnorm",
    )(x, g.reshape(1, d).astype(F32))


def _mm_kernel(*refs, epi, n_extra, n_out, cast_w):
    x_ref, w_ref = refs[0], refs[1]
    extras = refs[2:2 + n_extra]
    outs = refs[2 + n_extra:2 + n_extra + n_out]
    if cast_w:
        wb_ref = refs[2 + n_extra + n_out]

        @pl.when(pl.program_id(1) == 0)
        def _():
            wb_ref[...] = w_ref[...].astype(BF16)
    else:
        wb_ref = w_ref

    acc = _dot(x_ref[...], wb_ref[...])
    vals = epi(acc, *(e[...] for e in extras))
    for o_ref, val in zip(outs, vals):
        o_ref[...] = val.astype(o_ref.dtype)


def _matmul(x, w, epi, extras=(), out_dtypes=(F32,), name="matmul", layer=None):
    m, k = x.shape
    n = w.shape[-1]
    cast_w = w.dtype != BF16
    w_item = jnp.dtype(w.dtype).itemsize
    tn = _pick_tile(n, max(LANE, min(MM_TN_MAX, MM_W_TILE_BYTES // (w_item * k) // LANE * LANE)), LANE)
    n_tile_io = len(out_dtypes) + sum(kind == "tile" for kind, _ in extras)

    def vmem_need(tm):
        blocks = tm * k * 2 + k * tn * w_item + n_tile_io * tm * tn * 4
        return 2 * blocks + (k * tn * 2 if cast_w else 0) + 3 * tm * tn * 4

    tm = _pick_tile(m, 16, 16)
    for cand in sorted((d for d in range(16, m + 1, 16) if m % d == 0), reverse=True):
        if cand * k * 2 <= MM_X_TILE_BYTES and vmem_need(cand) <= MM_VMEM_TARGET:
            tm = cand
            break
    w_spec = (pl.BlockSpec((k, tn), lambda j, i: (0, j)) if layer is None else
              pl.BlockSpec((None, k, tn), lambda j, i: (layer, 0, j)))
    in_specs = [pl.BlockSpec((tm, k), lambda j, i: (i, 0)), w_spec]
    args = [x, w]
    blk = _nbytes((tm, k), x.dtype) + _nbytes((k, tn), w.dtype)
    for kind, arr in extras:
        if kind == "row":
            in_specs.append(pl.BlockSpec((1, tn), lambda j, i: (0, j)))
            args.append(arr.reshape(1, n).astype(F32))
            blk += _nbytes((8, tn), F32)
        else:
            in_specs.append(pl.BlockSpec((tm, tn), lambda j, i: (i, j)))
            args.append(arr)
            blk += _nbytes((tm, tn), arr.dtype)
    out_specs = [pl.BlockSpec((tm, tn), lambda j, i: (i, j)) for _ in out_dtypes]
    out_shape = [jax.ShapeDtypeStruct((m, n), dt) for dt in out_dtypes]
    blk += sum(_nbytes((tm, tn), dt) for dt in out_dtypes)
    outs = pl.pallas_call(
        functools.partial(_mm_kernel, epi=epi, n_extra=len(extras), n_out=len(out_dtypes), cast_w=cast_w),
        grid=(n // tn, m // tm),
        in_specs=in_specs, out_specs=out_specs, out_shape=out_shape,
        scratch_shapes=[pltpu.VMEM((k, tn), BF16)] if cast_w else [],
        compiler_params=_cparams(("parallel", "arbitrary"), blk,
                                 (_nbytes((k, tn), BF16) if cast_w else 0) + 3 * _nbytes((tm, tn), F32)),
        name=name,
    )(*args)
    return outs


def _epi_id(acc):
    return (acc,)


def _epi_silu(acc):
    return (acc * jax.nn.sigmoid(acc),)


def _epi_sigmoid(acc):
    return (jax.nn.sigmoid(acc),)


def _epi_scale(acc, *, scale):
    return (acc * scale,)


def _epi_residual(acc, res):
    return (res + acc,)


def _epi_hgrn_forget(acc, lb):
    lbh = jnp.maximum(lb, 0.0)
    a = jnp.log(lbh)
    c = jnp.log1p(-lbh) + _log_sigmoid(acc)
    logf = jnp.maximum(a, c) + jnp.log1p(jnp.exp(-jnp.abs(a - c)))
    kk = (1.0 - lbh) * jax.nn.sigmoid(-acc)
    return logf, kk


def _epi_bias_gelu(acc, b):
    x = acc + b
    return (0.5 * x * (1.0 + lax.erf(x * (2.0 ** -0.5))),)


def _epi_mlstm_gates(acc, b, *, heads):
    gates = GATE_CAP * jnp.tanh((acc + b) / GATE_CAP)
    col = lax.broadcasted_iota(jnp.int32, acc.shape, 1)
    return (jnp.where(col < heads, gates, _log_sigmoid(gates)),)


def _hgrn_levels(chunk, t_real):
    return tuple(m for m in (2 ** p for p in range(int(math.log2(chunk)) - 1, -1, -1)) if m < t_real)


def _hgrn_sum_matrix(chunk, levels, t_len):
    t = np.arange(chunk)[:, None]
    r = np.arange(chunk)[None, :]
    same = (t // t_len) == (r // t_len)
    mats = [(r <= t) & same, (r > t) & same]
    for m in levels:
        mid = (t // (2 * m)) * (2 * m) + m - 1
        second = (t % (2 * m)) >= m
        mats.append(np.where(second, (r > mid) & (r <= t), (r > t) & (r <= mid)))
    return np.concatenate(mats, axis=0).astype(np.float32)


def _hgrn_kernel(*refs, heads, chunk, levels, dk, group, n_aliased, slot):
    q_ref, lf_ref, k_ref, v_ref, g_ref, onorm_ref, d_ref, s0_ref = refs[:8]
    y_ref, s_ref, o_scr = refs[8 + n_aliased:]
    c = pl.program_id(1)
    t_len = chunk // group
    dv = s_ref.shape[-1]

    @pl.when(c == 0)
    def _():
        for other in range(s_ref.shape[0]):
            if other != slot:
                s_ref[other] = jnp.zeros(s_ref.shape[1:], F32)
        s_ref[slot] = s0_ref[0]

    row = lax.broadcasted_iota(jnp.int32, (chunk, 1), 0)
    in_seq = [None] if group == 1 else [(row >> int(math.log2(t_len))) == s for s in range(group)]
    seq_ones = [jnp.ones((chunk, dv), BF16) if mask is None else
                jnp.where(mask, jnp.ones((chunk, dv), F32), 0.0).astype(BF16) for mask in in_seq]
    r2 = lax.broadcasted_iota(jnp.int32, (chunk, chunk), 0)
    c2 = lax.broadcasted_iota(jnp.int32, (chunk, chunk), 1)

    lf_hi, lf_lo = _split2(lf_ref[...])
    xs = _dot(d_ref[...], jnp.concatenate([lf_hi, lf_lo], axis=0))
    q = q_ref[...].astype(F32)
    k = k_ref[...].astype(F32)
    vb = v_ref[...].astype(BF16)
    ex_b = jnp.exp(xs[0:chunk])
    q_in = q * ex_b
    k_out = k * jnp.exp(xs[chunk:2 * chunk])
    q_in = [(q_in if mask is None else jnp.where(mask, q_in, 0.0)).astype(BF16) for mask in in_seq]
    k_out = [(k_out if mask is None else jnp.where(mask, k_out, 0.0)).astype(BF16) for mask in in_seq]
    qs = [q_ref[...].astype(BF16)]
    ks = [k_ref[...].astype(BF16)]
    masks = [r2 == c2]
    for li, m in enumerate(levels):
        ex = jnp.exp(xs[(2 + li) * chunk:(3 + li) * chunk])
        second = (row & m) != 0
        qs.append(jnp.where(second, q * ex, 0.0).astype(BF16))
        ks.append(jnp.where(second, 0.0, k * ex).astype(BF16))
        shift = int(math.log2(2 * m))
        masks.append(None if 2 * m == chunk else (r2 >> shift) == (c2 >> shift))

    scores = []
    for h in range(heads):
        cols = slice(h * dk, (h + 1) * dk)
        a = None
        for qs_l, ks_l, mask in zip(qs, ks, masks):
            al = _dot_nt(qs_l[:, cols], ks_l[:, cols])
            if mask is not None:
                al = jnp.where(mask, al, 0.0)
            a = al if a is None else a + al
        scores.append(a.astype(BF16))

    for h in range(heads):
        cols = slice(h * dk, (h + 1) * dk)
        o = _dot(scores[h], vb[:, cols])
        for s in range(group):
            st = s_ref[slot, s, h]
            o = o + _dot(q_in[s][:, cols], st.astype(BF16))
            decay = jnp.exp(_dot_tn(lf_hi[:, cols], seq_ones[s]) + _dot_tn(lf_lo[:, cols], seq_ones[s]))
            s_ref[slot, s, h] = st * decay + _dot_tn(k_out[s][:, cols], vb[:, cols])
        o_scr[:, cols] = o

    o = o_scr[...]
    y = o * lax.rsqrt(jnp.mean(o * o, axis=-1, keepdims=True) + EPS) * onorm_ref[...]
    y_ref[...] = (y * g_ref[...]).astype(y_ref.dtype)


def _hgrn_scan(q, lf, k, v, g, onorm, s0_all, layer, s_out_prev, y_prev, *, row_start, batch, seq, chunk, group):
    d = q.shape[1]
    heads, dk, dv = s0_all.shape[2:]
    t_chunk = chunk // group
    n_chunks = seq // t_chunk
    assert group == 1 or n_chunks == 1
    assert row_start % chunk == 0 and batch % group == 0
    levels = _hgrn_levels(chunk, t_chunk)
    dmat = _hgrn_sum_matrix(chunk, levels, t_chunk)
    dmat = jnp.asarray(np.concatenate([dmat, dmat], axis=1), dtype=BF16)
    first_blk = row_start // chunk
    tok = pl.BlockSpec((chunk, d), lambda b, c: (first_blk + b * n_chunks + c, 0))
    in_layer = min(layer, s0_all.shape[0] - 1)
    st_in = pl.BlockSpec((1, group, heads, dk, dv), lambda b, c: (in_layer, b, 0, 0, 0))
    n_layers = s_out_prev if isinstance(s_out_prev, int) else s_out_prev.shape[0]
    fresh_state = isinstance(s_out_prev, int)
    slots = n_layers if fresh_state else 1
    st_out = pl.BlockSpec((slots, group, heads, dk, dv), lambda b, c: (0 if fresh_state else layer, b, 0, 0, 0))
    in_specs = [tok, tok, tok, tok, tok,
                pl.BlockSpec((1, d), lambda b, c: (0, 0)),
                pl.BlockSpec(dmat.shape, lambda b, c: (0, 0)),
                st_in]
    args = [q, lf, k, v, g, onorm.reshape(1, d).astype(F32), dmat, s0_all]
    aliases = {}
    for out_idx, prev in ((0, y_prev), (1, s_out_prev)):
        if not isinstance(prev, int):
            aliases[len(args)] = out_idx
            in_specs.append(pl.BlockSpec(memory_space=pl.ANY))
            args.append(prev)
    y_rows = y_prev if isinstance(y_prev, int) else y_prev.shape[0]
    blk = (sum(_nbytes((chunk, d), a.dtype) for a in (q, lf, k, v, g)) + _nbytes((chunk, d), BF16) + (1 + slots) * _nbytes((group, heads, dk, dv), F32)
           + _nbytes(dmat.shape, BF16))
    n_exp = 2 + len(levels)
    y, s = pl.pallas_call(
        functools.partial(_hgrn_kernel, heads=heads, chunk=chunk, levels=levels, dk=dk, group=group,
                          n_aliased=len(aliases), slot=layer if fresh_state else 0),
        grid=(batch // group, n_chunks),
        in_specs=in_specs,
        out_specs=[tok, st_out],
        out_shape=[jax.ShapeDtypeStruct((y_rows, d), BF16),
                   jax.ShapeDtypeStruct((n_layers, batch, heads, dk, dv), F32)],
        scratch_shapes=[pltpu.VMEM((chunk, d), F32)],
        input_output_aliases=aliases,
        compiler_params=_cparams(("parallel", "arbitrary"), blk, (3 * n_exp + 8) * _nbytes((chunk, d), F32)),
        name="hgrn_scan",
    )(*args)
    return y, s


def _mlstm_kernel(*refs, heads, chunk, dqk, dv, group, n_aliased):
    q_ref, k_ref, v_ref, og_ref, gt_ref, hn_ref, tri_ref, sel_ref, c0_ref, n0_ref, m0_ref = refs[:11]
    y_ref, c_ref, n_ref, m_ref = refs[11 + n_aliased:]
    c = pl.program_id(1)
    t_len = chunk // group

    @pl.when(c == 0)
    def _():
        c_ref[...] = c0_ref[...]
        n_ref[...] = n0_ref[...]
        m_ref[...] = m0_ref[...]

    gates = gt_ref[...]
    tri = tri_ref[...]
    sel = sel_ref[...]
    g3 = _split3(gates)
    cum = sum(_dot(tri, p) for p in g3)
    gates_t = sum(_dot_nt(sel, p) for p in g3)
    cum_t = sum(_dot_nt(sel, p) for p in _split3(cum))
    row = lax.broadcasted_iota(jnp.int32, (chunk, 1), 0)
    r2 = lax.broadcasted_iota(jnp.int32, (chunk, chunk), 0)
    c2 = lax.broadcasted_iota(jnp.int32, (chunk, chunk), 1)
    causal = c2 <= r2
    in_seq = [None]
    if group > 1:
        shift = int(math.log2(t_len))
        causal = causal & ((r2 >> shift) == (c2 >> shift))
        in_seq = [(row >> shift) == s for s in range(group)]

    def per_row(vals):
        if group == 1:
            return vals[0]
        out = jnp.where(in_seq[0], vals[0], 0.0)
        for s in range(1, group):
            out = jnp.where(in_seq[s], vals[s], out)
        return out

    lane = lax.broadcasted_iota(jnp.int32, (1, LANE), 1)
    m_rows = [m_ref[s] for s in range(group)]
    m_new = list(m_rows)
    for h in range(heads):
        qc = slice(h * dqk, (h + 1) * dqk)
        vc = slice(h * dv, (h + 1) * dv)
        qb = q_ref[:, qc].astype(BF16)
        kb = k_ref[:, qc].astype(BF16)
        qh = qb.astype(F32)
        kh = kb.astype(F32)
        vb = v_ref[:, vc].astype(BF16)
        b_c = cum[:, heads + h:heads + h + 1]
        ig_c = gates[:, h:h + 1]
        b_r = cum_t[heads + h:heads + h + 1, :]
        ig_r = gates_t[h:h + 1, :]
        m_h = [m_rows[s][:, h:h + 1] for s in range(group)]
        dlog = jnp.where(causal, b_c - b_r + ig_r, -jnp.inf)
        inter = b_c + per_row(m_h)
        mt = jnp.maximum(inter, jnp.max(dlog, axis=-1, keepdims=True))
        wts = jnp.exp(dlog - mt) * _dot_nt(qb, kb)
        sc = jnp.exp(inter - mt)
        c_h = [c_ref[s, h] for s in range(group)]
        n_h = [n_ref[s, h:h + 1, :] for s in range(group)]
        if group == 1:
            q_c = _dot(qb, c_h[0].astype(BF16))
        else:
            q_c = sum(_dot(jnp.where(in_seq[s], qh.astype(F32), 0.0).astype(BF16), c_h[s].astype(BF16))
                      for s in range(group))
        num = sc * q_c + _dot(wts.astype(BF16), vb)
        den = sc * jnp.sum(qh * per_row(n_h), axis=-1, keepdims=True) + jnp.sum(wts, axis=-1, keepdims=True)
        out = num / jnp.maximum(jnp.abs(den), jnp.exp(-mt))
        for s in range(group):
            last = (s + 1) * t_len - 1
            m_last = mt[last:last + 1, :]
            b_last = b_c[last:last + 1, :]
            sc_state = jnp.exp(b_last + m_h[s] - m_last)
            w_k = jnp.exp(b_last - b_c + ig_c - m_last)
            if group > 1:
                w_k = jnp.where(in_seq[s], w_k, 0.0)
            kw = w_k * kh
            c_ref[s, h] = sc_state * c_h[s] + _dot_tn(kw.astype(BF16), vb)
            n_ref[s, h:h + 1, :] = sc_state * n_h[s] + jnp.sum(kw, axis=0, keepdims=True)
            m_new[s] = jnp.where(lane == h, m_last, m_new[s])
        y = out * lax.rsqrt(jnp.mean(out * out, axis=-1, keepdims=True) + EPS) * hn_ref[:, vc]
        y_ref[:, vc] = (y * og_ref[:, vc]).astype(y_ref.dtype)
    for s in range(group):
        m_ref[s] = m_new[s]


def _mlstm_scan(q, k, v, og, gates, hnorm, c0, n0, m0, y_prev, *, row_start, batch, seq, chunk, group):
    heads, dqk, dv = c0.shape[1], c0.shape[2], c0.shape[3]
    dq_all, dv_all = q.shape[1], v.shape[1]
    t_chunk = chunk // group
    n_chunks = seq // t_chunk
    assert 2 * heads <= 16 and (group == 1 or n_chunks == 1)
    assert row_start % chunk == 0 and batch % group == 0
    idx = np.arange(chunk)
    same_seq = (idx[:, None] // t_chunk) == (idx[None, :] // t_chunk)
    tri = jnp.asarray(np.tril(np.ones((chunk, chunk), np.float32)) * same_seq, dtype=BF16)
    sel = jnp.asarray(np.eye(16, LANE, dtype=np.float32), dtype=BF16)
    m0p = jnp.pad(m0, ((0, 0), (0, LANE - heads))).reshape(batch, 1, LANE)
    first_blk = row_start // chunk
    tq = pl.BlockSpec((chunk, dq_all), lambda b, c: (first_blk + b * n_chunks + c, 0))
    tv = pl.BlockSpec((chunk, dv_all), lambda b, c: (first_blk + b * n_chunks + c, 0))
    tg = pl.BlockSpec((chunk, LANE), lambda b, c: (first_blk + b * n_chunks + c, 0))
    sc_ = pl.BlockSpec((group, heads, dqk, dv), lambda b, c: (b, 0, 0, 0))
    sn_ = pl.BlockSpec((group, heads, dqk), lambda b, c: (b, 0, 0))
    sm_ = pl.BlockSpec((group, 1, LANE), lambda b, c: (b, 0, 0))
    in_specs = [tq, tq, tv, tv, tg,
                pl.BlockSpec((1, dv_all), lambda b, c: (0, 0)),
                pl.BlockSpec(tri.shape, lambda b, c: (0, 0)),
                pl.BlockSpec(sel.shape, lambda b, c: (0, 0)),
                sc_, sn_, sm_]
    args = [q, k, v, og, gates, hnorm.reshape(1, dv_all).astype(F32), tri, sel, c0, n0, m0p]
    aliases = {}
    if not isinstance(y_prev, int):
        aliases[len(args)] = 0
        in_specs.append(pl.BlockSpec(memory_space=pl.ANY))
        args.append(y_prev)
    y_rows = y_prev if isinstance(y_prev, int) else y_prev.shape[0]
    blk = (_nbytes((chunk, dq_all), q.dtype) + _nbytes((chunk, dq_all), k.dtype) + _nbytes((chunk, dv_all), v.dtype)
           + _nbytes((chunk, dv_all), og.dtype) + _nbytes((chunk, LANE), F32)
           + _nbytes((chunk, dv_all), BF16) + 2 * _nbytes((group,) + c0.shape[1:], F32))
    y, c_out, n_out, m_out = pl.pallas_call(
        functools.partial(_mlstm_kernel, heads=heads, chunk=chunk, dqk=dqk, dv=dv, group=group,
                          n_aliased=len(aliases)),
        grid=(batch // group, n_chunks),
        in_specs=in_specs,
        out_specs=[tv, sc_, sn_, sm_],
        out_shape=[jax.ShapeDtypeStruct((y_rows, dv_all), BF16),
                   jax.ShapeDtypeStruct(c0.shape, F32),
                   jax.ShapeDtypeStruct(n0.shape, F32),
                   jax.ShapeDtypeStruct((batch, 1, LANE), F32)],
        input_output_aliases=aliases,
        compiler_params=_cparams(("parallel", "arbitrary"), blk, 8 * _nbytes((chunk, dv_all), F32)),
        name="mlstm_scan",
    )(*args)
    return y, c_out, n_out, m_out[:, 0, :heads]


def _gmlp_kernel(u_ref, v_ref, vg_ref, vb_ref, w_ref, bs_ref, o_ref, vn_ref, *, groups, gd):
    v = v_ref[...]
    mu = jnp.mean(v, axis=-1, keepdims=True)
    xc = v - mu
    vn = xc * lax.rsqrt(jnp.mean(xc * xc, axis=-1, keepdims=True) + EPS) * vg_ref[...] + vb_ref[...]
    vn_ref[...] = vn
    n = v.shape[0]
    causal = (lax.broadcasted_iota(jnp.int32, (n, n), 1) <= lax.broadcasted_iota(jnp.int32, (n, n), 0))
    bs = bs_ref[0]
    for g in range(groups):
        cols = slice(g * gd, (g + 1) * gd)
        wg = jnp.where(causal, w_ref[0, g], 0.0).astype(BF16)
        mix = _dot(wg, vn[:, cols].astype(BF16)) + bs[:, g:g + 1]
        o_ref[:, cols] = (u_ref[:, cols] * mix).astype(o_ref.dtype)


def _gmlp_gate(z, vg, vb, w_stack, bs_stack, *, n_prompt_chunks):
    m, d2 = z.shape
    d = d2 // 2
    groups, chunk = w_stack.shape[1], w_stack.shape[2]
    gd = d // groups
    assert m % chunk == 0

    def which(i):
        return jnp.minimum(i // n_prompt_chunks, 1)

    blk = (2 * _nbytes((chunk, d), F32) + _nbytes((chunk, d), BF16) + _nbytes((chunk, d), F32)
           + _nbytes((groups, chunk, chunk), F32))
    o, vn = pl.pallas_call(
        functools.partial(_gmlp_kernel, groups=groups, gd=gd),
        grid=(m // chunk,),
        in_specs=[pl.BlockSpec((chunk, d), lambda i: (i, 0)),
                  pl.BlockSpec((chunk, d), lambda i: (i, 1)),
                  pl.BlockSpec((1, d), lambda i: (0, 0)),
                  pl.BlockSpec((1, d), lambda i: (0, 0)),
                  pl.BlockSpec((1, groups, chunk, chunk), lambda i: (which(i), 0, 0, 0)),
                  pl.BlockSpec((1, chunk, groups), lambda i: (which(i), 0, 0))],
        out_specs=[pl.BlockSpec((chunk, d), lambda i: (i, 0)),
                   pl.BlockSpec((chunk, d), lambda i: (i, 0))],
        out_shape=[jax.ShapeDtypeStruct((m, d), BF16), jax.ShapeDtypeStruct((m, d), F32)],
        compiler_params=_cparams(("parallel",), blk, 4 * _nbytes((chunk, d), F32)),
        name="gmlp_gate",
    )(z, z, vg.reshape(1, d).astype(F32), vb.reshape(1, d).astype(F32), w_stack, bs_stack)
    return o, vn


def _causal_conv(tap, cw_ref, cb_ref, conv_w):
    y = cb_ref[...] + cw_ref[conv_w - 1:conv_w, :] * tap(0)
    for back in range(1, conv_w):
        y = y + cw_ref[conv_w - 1 - back:conv_w - back, :] * tap(back)
    return y


def _ffn_up_prompt_kernel(h_ref, wa_ref, wg_ref, cwa_ref, cwg_ref, cba_ref, cbg_ref,
                          _, act_ref, tail_a_ref, tail_g_ref, wb_scr, up_scr, *, tiles_per_seq, conv_w, row_chunk):
    i = pl.program_id(1)
    tm = h_ref.shape[0]
    n_chunks = tm // row_chunk

    @pl.when(i == 0)
    def _():
        wb_scr[0] = wa_ref[...].astype(BF16)
        wb_scr[1] = wg_ref[...].astype(BF16)

    @pl.when(lax.rem(i, tiles_per_seq) == 0)
    def _():
        up_scr[:, 0:8, :] = jnp.zeros((2, 8, up_scr.shape[2]), F32)

    def multiply(c):
        hb = h_ref[c * row_chunk:(c + 1) * row_chunk, :]
        for idx in range(2):
            up_scr[idx, 8 + c * row_chunk:8 + (c + 1) * row_chunk, :] = _dot(hb, wb_scr[idx])

    def finish(c):
        lo = 8 + c * row_chunk
        ya = _causal_conv(lambda back: up_scr[0, lo - back:lo - back + row_chunk, :], cwa_ref, cba_ref, conv_w)
        yg = _causal_conv(lambda back: up_scr[1, lo - back:lo - back + row_chunk, :], cwg_ref, cbg_ref, conv_w)
        act_ref[c * row_chunk:(c + 1) * row_chunk, :] = (ya * (yg * jax.nn.sigmoid(yg))).astype(act_ref.dtype)

    multiply(0)
    for c in range(1, n_chunks):
        multiply(c)
        finish(c - 1)
    finish(n_chunks - 1)
    for idx, tail_ref in ((0, tail_a_ref), (1, tail_g_ref)):
        last = up_scr[idx, tm:tm + 8, :]
        tail_ref[0] = last
        up_scr[idx, 0:8, :] = last


def _ffn_up_prompt(h, w_up, layer, cw, cb, act_prev, *, rows, seq):
    _, k, f2 = w_up.shape
    f = f2 // 2
    conv_w = cw.shape[0]
    tm = _pick_tile(seq, 1024, 16)
    tf = _pick_tile(f, FFN_TF, LANE)
    nf = f // tf
    n_tiles = rows // tm
    row_chunk = _pick_tile(tm, FFN_ROW_CHUNK, 16)
    blk = (_nbytes((tm, k), BF16) + 2 * _nbytes((k, tf), F32) + _nbytes((tm, tf), BF16)
           + 2 * _nbytes((8, tf), F32) + 6 * _nbytes((8, tf), F32))
    tail = jax.ShapeDtypeStruct((n_tiles, 8, f), F32)
    act, tail_a, tail_g = pl.pallas_call(
        functools.partial(_ffn_up_prompt_kernel, tiles_per_seq=seq // tm, conv_w=conv_w, row_chunk=row_chunk),
        grid=(nf, n_tiles),
        in_specs=[pl.BlockSpec((tm, k), lambda j, i: (i, 0)),
                  pl.BlockSpec((None, k, tf), lambda j, i: (layer, 0, j)),
                  pl.BlockSpec((None, k, tf), lambda j, i: (layer, 0, j + nf)),
                  pl.BlockSpec((conv_w, tf), lambda j, i: (0, j)),
                  pl.BlockSpec((conv_w, tf), lambda j, i: (0, j + nf)),
                  pl.BlockSpec((1, tf), lambda j, i: (0, j)),
                  pl.BlockSpec((1, tf), lambda j, i: (0, j + nf)),
                  pl.BlockSpec(memory_space=pl.ANY)],
        out_specs=[pl.BlockSpec((tm, tf), lambda j, i: (i, j)),
                   pl.BlockSpec((1, 8, tf), lambda j, i: (i, 0, j)),
                   pl.BlockSpec((1, 8, tf), lambda j, i: (i, 0, j))],
        out_shape=[jax.ShapeDtypeStruct(act_prev.shape, BF16), tail, tail],
        input_output_aliases={7: 0},
        scratch_shapes=[pltpu.VMEM((2, k, tf), BF16), pltpu.VMEM((2, 8 + tm, tf), F32)],
        compiler_params=_cparams(("parallel", "arbitrary"), blk,
                                 2 * _nbytes((k, tf), BF16) + 2 * _nbytes((8 + tm, tf), F32)
                                 + 6 * _nbytes((row_chunk, tf), F32)),
        name="ffn_up_conv_prompt",
    )(h, w_up, w_up, cw, cw, cb.reshape(1, f2), cb.reshape(1, f2), act_prev)
    tiles_per_seq = seq // tm
    keep = slice(8 - (conv_w - 1), 8)
    tails = jnp.concatenate([tail_a[tiles_per_seq - 1::tiles_per_seq, keep], tail_g[tiles_per_seq - 1::tiles_per_seq, keep]],
                            axis=-1)
    return act, tails


def _ffn_up_sample_kernel(*refs, steps, conv_w):
    h_ref, wa_ref, wg_ref, cwa_ref, cwg_ref, cba_ref, cbg_ref = refs[:7]
    bufs_a = refs[7:7 + conv_w - 1]
    bufs_g = refs[7 + conv_w - 1:7 + 2 * (conv_w - 1)]
    act_ref, upa_ref, upg_ref = refs[7 + 2 * (conv_w - 1):]
    bs = h_ref.shape[0] // steps
    hb = h_ref[...]

    def branch(w_ref, bufs, cw_ref, cb_ref, up_ref):
        up = _dot(hb, w_ref[...].astype(BF16))
        up_ref[...] = up

        def at(t):
            return up[t * bs:(t + 1) * bs, :] if t >= 0 else bufs[conv_w - 1 + t][...]

        return [_causal_conv(lambda back, t=t: at(t - back), cw_ref, cb_ref, conv_w) for t in range(steps)]

    ya = branch(wa_ref, bufs_a, cwa_ref, cba_ref, upa_ref)
    yg = branch(wg_ref, bufs_g, cwg_ref, cbg_ref, upg_ref)
    for t in range(steps):
        act_ref[t * bs:(t + 1) * bs, :] = (ya[t] * (yg[t] * jax.nn.sigmoid(yg[t]))).astype(act_ref.dtype)


def _ffn_up_sample(h_t, w_up, layer, cw, cb, conv_state, *, steps):
    _, k, f2 = w_up.shape
    f = f2 // 2
    conv_w = cw.shape[0]
    ms = h_t.shape[0]
    bs = ms // steps
    assert steps >= conv_w - 1 and bs % 16 == 0 and conv_state.shape[1] == bs
    tf = _pick_tile(f, 256, LANE)
    nf = f // tf
    state_rows = conv_state.reshape(conv_state.shape[0], bs, (conv_w - 1) * f2)
    state_specs = [pl.BlockSpec((None, bs, tf), lambda j, r=r, off=off: (layer, 0, r * (f2 // tf) + off + j))
                   for off in (0, nf) for r in range(conv_w - 1)]
    blk = (_nbytes((ms, k), BF16) + 2 * _nbytes((k, tf), F32) + 2 * _nbytes((conv_w - 1, bs, tf), F32)
           + _nbytes((ms, tf), BF16) + 2 * _nbytes((ms, tf), F32) + 6 * _nbytes((8, tf), F32))
    up_shape = jax.ShapeDtypeStruct((ms, f), F32)
    act, up_a, up_g = pl.pallas_call(
        functools.partial(_ffn_up_sample_kernel, steps=steps, conv_w=conv_w),
        grid=(nf,),
        in_specs=[pl.BlockSpec((ms, k), lambda j: (0, 0)),
                  pl.BlockSpec((None, k, tf), lambda j: (layer, 0, j)),
                  pl.BlockSpec((None, k, tf), lambda j: (layer, 0, j + nf)),
                  pl.BlockSpec((conv_w, tf), lambda j: (0, j)),
                  pl.BlockSpec((conv_w, tf), lambda j: (0, j + nf)),
                  pl.BlockSpec((1, tf), lambda j: (0, j)),
                  pl.BlockSpec((1, tf), lambda j: (0, j + nf))] + state_specs,
        out_specs=[pl.BlockSpec((ms, tf), lambda j: (0, j))] * 3,
        out_shape=[jax.ShapeDtypeStruct((ms, f), BF16), up_shape, up_shape],
        compiler_params=_cparams(("parallel",), blk, 2 * _nbytes((k, tf), BF16) + 8 * _nbytes((ms, tf), F32)),
        name="ffn_up_conv_sample",
    )(h_t, w_up, w_up, cw, cw, cb.reshape(1, f2), cb.reshape(1, f2), *([state_rows] * (2 * (conv_w - 1))))
    keep = slice((steps - (conv_w - 1)) * bs, ms)
    up_tail = jnp.concatenate([up_a[keep], up_g[keep]], axis=-1).reshape(conv_w - 1, bs, f2).transpose(1, 0, 2)
    return act, up_tail


def _forward(x_prompt, x_sample, state_hgrn_S, state_mlstm_C, state_mlstm_n, state_mlstm_m, state_ffn_conv,
             norm_mix, norm_ffn, norm_final,
             hgrn_w_q, hgrn_w_f, hgrn_w_i, hgrn_w_g, hgrn_lb, hgrn_onorm, hgrn_w_o,
             mlstm_w_q, mlstm_w_k, mlstm_w_v, mlstm_w_og, mlstm_w_if, mlstm_b_if, mlstm_hnorm, mlstm_w_out,
             gmlp_w_in, gmlp_b_in, gmlp_vnorm_g, gmlp_vnorm_b, gmlp_w_s, gmlp_b_s, gmlp_w_out,
             ffn_w_up, ffn_conv_w, ffn_conv_b, ffn_w_down):
    bp, tp, d = x_prompt.shape
    bs, ts, _ = x_sample.shape
    mp, ms = bp * tp, bs * ts
    m = mp + ms
    depth = norm_mix.shape[0]
    hg_heads, hg_dk, hg_dv = state_hgrn_S.shape[2:]
    ml_heads, ml_dqk, ml_dv = state_mlstm_C.shape[2:]

    x = jnp.concatenate([x_prompt.reshape(mp, d), x_sample.reshape(ms, d)], axis=0)

    lb = jax.nn.softmax(hgrn_lb.astype(F32), axis=0)
    lbs = jnp.cumsum(lb, axis=0) - lb[0]

    w_down = ffn_w_down.astype(BF16)
    act = jnp.zeros((m, ffn_w_down.shape[1]), BF16)
    out_c, out_n, out_m, out_v, out_conv = [], [], [], [], []
    s_p_all = s_s_all = state_hgrn_S.shape[0]
    for i in range(depth):
        j = i // N_MIXERS
        kind = i % N_MIXERS
        h = _rmsnorm(x, norm_mix[i], BF16)
        if kind == 0:
            (q,) = _matmul(h, hgrn_w_q, _epi_silu, out_dtypes=(BF16,), name="hgrn_q", layer=j)
            lf, k = _matmul(h, hgrn_w_f, _epi_hgrn_forget, extras=[("row", lbs[j])],
                            out_dtypes=(F32, BF16), name="hgrn_f", layer=j)
            (v,) = _matmul(h, hgrn_w_i, _epi_id, out_dtypes=(BF16,), name="hgrn_i", layer=j)
            (g,) = _matmul(h, hgrn_w_g, _epi_sigmoid, out_dtypes=(BF16,), name="hgrn_g", layer=j)
            y, s_p_all = _hgrn_scan(q, lf, k, v, g, hgrn_onorm[j],
                                    jnp.zeros((1, bp, hg_heads, hg_dk, hg_dv), F32), j, s_p_all, h,
                                    row_start=0, batch=bp, seq=tp, chunk=min(HGRN_CHUNK, tp), group=1)
            group = max(1, SAMPLE_ROWS // ts)
            y, s_s_all = _hgrn_scan(q, lf, k, v, g, hgrn_onorm[j], state_hgrn_S.astype(F32), j, s_s_all, y,
                                    row_start=mp, batch=bs, seq=ts, chunk=group * ts, group=group)
            (x,) = _matmul(y, hgrn_w_o, _epi_residual, extras=[("tile", x)], name="hgrn_o", layer=j)
        elif kind == 1:
            (q,) = _matmul(h, mlstm_w_q, _epi_id, out_dtypes=(BF16,), name="mlstm_q", layer=j)
            (k,) = _matmul(h, mlstm_w_k, functools.partial(_epi_scale, scale=ml_dqk ** -0.5),
                           out_dtypes=(BF16,), name="mlstm_k", layer=j)
            (v,) = _matmul(h, mlstm_w_v, _epi_id, out_dtypes=(BF16,), name="mlstm_v", layer=j)
            (og,) = _matmul(h, mlstm_w_og, _epi_sigmoid, out_dtypes=(BF16,), name="mlstm_og", layer=j)
            w_if = jnp.pad(mlstm_w_if[j], ((0, 0), (0, LANE - 2 * ml_heads)))
            b_if = jnp.pad(mlstm_b_if[j], (0, LANE - 2 * ml_heads))
            (gates,) = _matmul(h, w_if, functools.partial(_epi_mlstm_gates, heads=ml_heads),
                               extras=[("row", b_if)], name="mlstm_if")
            y, c_p, n_p, m_p = _mlstm_scan(
                q, k, v, og, gates, mlstm_hnorm[j],
                jnp.zeros((bp, ml_heads, ml_dqk, ml_dv), F32), jnp.zeros((bp, ml_heads, ml_dqk), F32),
                jnp.zeros((bp, ml_heads), F32), h,
                row_start=0, batch=bp, seq=tp, chunk=min(MLSTM_CHUNK, tp), group=1)
            group = max(1, SAMPLE_ROWS // ts)
            y, c_s, n_s, m_s = _mlstm_scan(
                q, k, v, og, gates, mlstm_hnorm[j],
                state_mlstm_C[j].astype(F32), state_mlstm_n[j].astype(F32), state_mlstm_m[j].astype(F32), y,
                row_start=mp, batch=bs, seq=ts, chunk=group * ts, group=group)
            out_c.append((c_p, c_s))
            out_n.append((n_p, n_s))
            out_m.append((m_p, m_s))
            (x,) = _matmul(y, mlstm_w_out, _epi_residual, extras=[("tile", x)], name="mlstm_out", layer=j)
        else:
            (z,) = _matmul(h, gmlp_w_in, _epi_bias_gelu, extras=[("row", gmlp_b_in[j])], name="gmlp_in", layer=j)
            groups = gmlp_w_s.shape[1]
            lp = min(GMLP_CHUNK, tp)
            ls = min(GMLP_CHUNK, ts)
            reps = GMLP_CHUNK // ls
            w_s_blk = jnp.einsum("ab,gts->gatbs", jnp.eye(reps, dtype=F32), gmlp_w_s[j][:, :ls, :ls])
            w_s_blk = w_s_blk.reshape(groups, GMLP_CHUNK, GMLP_CHUNK)
            w_p = gmlp_w_s[j][:, :lp, :lp]
            w_stack = jnp.stack([w_p, w_s_blk])
            bs_stack = jnp.stack([gmlp_b_s[j][:, :lp].T, jnp.tile(gmlp_b_s[j][:, :ls].T, (reps, 1))])
            y, vn = _gmlp_gate(z, gmlp_vnorm_g[j], gmlp_vnorm_b[j], w_stack, bs_stack,
                               n_prompt_chunks=mp // GMLP_CHUNK)
            out_v.append(vn[mp:].reshape(bs, ts, d))
            (x,) = _matmul(y, gmlp_w_out, _epi_residual, extras=[("tile", x)], name="gmlp_out", layer=j)

        h = _rmsnorm(x, norm_ffn[i], BF16)
        act, tail_p = _ffn_up_prompt(h, ffn_w_up, i, ffn_conv_w[i], ffn_conv_b[i], act, rows=mp, seq=tp)
        h_t = h[mp:].reshape(bs, ts, d).transpose(1, 0, 2).reshape(ms, d)
        act_t, tail_s = _ffn_up_sample(h_t, ffn_w_up, i, ffn_conv_w[i], ffn_conv_b[i], state_ffn_conv.astype(F32),
                                       steps=ts)
        act_s = act_t.reshape(ts, bs, -1).transpose(1, 0, 2).reshape(ms, -1)
        act = lax.dynamic_update_slice(act, act_s, (mp, 0))
        out_conv.append((tail_p, tail_s))
        (x,) = _matmul(act, w_down, _epi_residual, extras=[("tile", x)], name="ffn_down", layer=i)

    y_p = _rmsnorm(x, norm_final, F32, row_start=0, n_rows=mp).reshape(bp, tp, d)
    y_s = _rmsnorm(x, norm_final, F32, row_start=mp, n_rows=ms).reshape(bs, ts, d)
    stack = lambda pairs, idx: jnp.stack([p[idx] for p in pairs])
    return (y_p, y_s,
            s_p_all, s_s_all, stack(out_c, 0), stack(out_c, 1),
            stack(out_n, 0), stack(out_n, 1), stack(out_m, 0), stack(out_m, 1),
            jnp.stack(out_v), stack(out_conv, 0), stack(out_conv, 1))


_forward_jit = jax.jit(_forward)


def kernel(x_prompt, x_sample, state_hgrn_S, state_mlstm_C, state_mlstm_n, state_mlstm_m, state_ffn_conv, norm_mix, norm_ffn, norm_final, hgrn_w_q, hgrn_w_f, hgrn_w_i, hgrn_w_g, hgrn_lb, hgrn_onorm, hgrn_w_o, mlstm_w_q, mlstm_w_k, mlstm_w_v, mlstm_w_og, mlstm_w_if, mlstm_b_if, mlstm_hnorm, mlstm_w_out, gmlp_w_in, gmlp_b_in, gmlp_vnorm_g, gmlp_vnorm_b, gmlp_w_s, gmlp_b_s, gmlp_w_out, ffn_w_up, ffn_conv_w, ffn_conv_b, ffn_w_down):
    return _forward_jit(x_prompt, x_sample, state_hgrn_S, state_mlstm_C, state_mlstm_n, state_mlstm_m, state_ffn_conv,
                        norm_mix, norm_ffn, norm_final,
                        hgrn_w_q, hgrn_w_f, hgrn_w_i, hgrn_w_g, hgrn_lb, hgrn_onorm, hgrn_w_o,
                        mlstm_w_q, mlstm_w_k, mlstm_w_v, mlstm_w_og, mlstm_w_if, mlstm_b_if, mlstm_hnorm, mlstm_w_out,
                        gmlp_w_in, gmlp_b_in, gmlp_vnorm_g, gmlp_vnorm_b, gmlp_w_s, gmlp_b_s, gmlp_w_out,
                        ffn_w_up, ffn_conv_w, ffn_conv_b, ffn_w_down)
```

```python
import functools
import math

import numpy as np
import jax
import jax.numpy as jnp
from jax import lax
from jax.experimental import pallas as pl
from jax.experimental.pallas import tpu as pltpu

F32 = jnp.float32
BF16 = jnp.bfloat16

EPS = 1e-6
GATE_CAP = 15.0
N_MIXERS = 3
HGRN_CHUNK = 64
MLSTM_CHUNK = 128
GMLP_CHUNK = 128
FFN_ROW_CHUNK = 256
FFN_TF = 512
SAMPLE_ROWS = 16

LANE = 128
VMEM_PHYSICAL_V7X = 64 * 1024 * 1024
VMEM_BUDGET = VMEM_PHYSICAL_V7X - 8 * 1024 * 1024

MM_TN_MAX = 1024
MM_W_TILE_BYTES = 8 << 20
MM_X_TILE_BYTES = 13 << 20
MM_VMEM_TARGET = 52 << 20


def _nbytes(shape, dtype):
    return int(np.prod(shape)) * jnp.dtype(dtype).itemsize


def _cparams(semantics, block_bytes, temp_bytes=0):
    need = 2 * block_bytes + temp_bytes + (4 << 20)
    return pltpu.CompilerParams(dimension_semantics=semantics,
                                vmem_limit_bytes=int(min(max(need, 16 << 20), VMEM_BUDGET)))


def _pick_tile(n, target, align):
    best = None
    for d in range(align, min(n, target) + 1, align):
        if n % d == 0:
            best = d
    return best if best is not None else n


def _dot(a, b):
    return jnp.dot(a, b, preferred_element_type=F32)


def _dot_nt(a, b):
    return lax.dot_general(a, b, (((1,), (1,)), ((), ())), preferred_element_type=F32)


def _dot_tn(a, b):
    return lax.dot_general(a, b, (((0,), (0,)), ((), ())), preferred_element_type=F32)


def _log_sigmoid(x):
    return -(jnp.maximum(-x, 0.0) + jnp.log1p(jnp.exp(-jnp.abs(x))))


def _split2(x):
    hi = x.astype(BF16)
    lo = (x - hi.astype(F32)).astype(BF16)
    return hi, lo


def _split3(x):
    hi = x.astype(BF16)
    r = x - hi.astype(F32)
    mid = r.astype(BF16)
    lo = (r - mid.astype(F32)).astype(BF16)
    return hi, mid, lo


def _rmsnorm_kernel(x_ref, g_ref, o_ref):
    x = x_ref[...]
    y = x * lax.rsqrt(jnp.mean(x * x, axis=-1, keepdims=True) + EPS)
    o_ref[...] = (y * g_ref[...]).astype(o_ref.dtype)


def _rmsnorm(x, g, out_dtype, row_start=0, n_rows=None):
    m_all, d = x.shape
    n_rows = m_all - row_start if n_rows is None else n_rows
    tm = _pick_tile(math.gcd(n_rows, row_start) if row_start else n_rows, 544, 16)
    off = row_start // tm
    blk = _nbytes((tm, d), F32) + _nbytes((tm, d), out_dtype)
    return pl.pallas_call(
        _rmsnorm_kernel,
        grid=(n_rows // tm,),
        in_specs=[pl.BlockSpec((tm, d), lambda i: (i + off, 0)),
                  pl.BlockSpec((1, d), lambda i: (0, 0))],
        out_specs=pl.BlockSpec((tm, d), lambda i: (i, 0)),
        out_shape=jax.ShapeDtypeStruct((n_rows, d), out_dtype),
        compiler_params=_cparams(("parallel",), blk, _nbytes((tm, d), F32)),
        name="rmsnorm",
    )(x, g.reshape(1, d).astype(F32))


def _mm_kernel(*refs, epi, n_extra, n_out, cast_w):
    x_ref, w_ref = refs[0], refs[1]
    extras = refs[2:2 + n_extra]
    outs = refs[2 + n_extra:2 + n_extra + n_out]
    if cast_w:
        wb_ref = refs[2 + n_extra + n_out]

        @pl.when(pl.program_id(1) == 0)
        def _():
            wb_ref[...] = w_ref[...].astype(BF16)
    else:
        wb_ref = w_ref

    acc = _dot(x_ref[...], wb_ref[...])
    vals = epi(acc, *(e[...] for e in extras))
    for o_ref, val in zip(outs, vals):
        o_ref[...] = val.astype(o_ref.dtype)


def _matmul(x, w, epi, extras=(), out_dtypes=(F32,), name="matmul", layer=None):
    m, k = x.shape
    n = w.shape[-1]
    cast_w = w.dtype != BF16
    w_item = jnp.dtype(w.dtype).itemsize
    tn = _pick_tile(n, max(LANE, min(MM_TN_MAX, MM_W_TILE_BYTES // (w_item * k) // LANE * LANE)), LANE)
    n_tile_io = len(out_dtypes) + sum(kind == "tile" for kind, _ in extras)

    def vmem_need(tm):
        blocks = tm * k * 2 + k * tn * w_item + n_tile_io * tm * tn * 4
        return 2 * blocks + (k * tn * 2 if cast_w else 0) + 3 * tm * tn * 4

    tm = _pick_tile(m, 16, 16)
    for cand in sorted((d for d in range(16, m + 1, 16) if m % d == 0), reverse=True):
        if cand * k * 2 <= MM_X_TILE_BYTES and vmem_need(cand) <= MM_VMEM_TARGET:
            tm = cand
            break
    w_spec = (pl.BlockSpec((k, tn), lambda j, i: (0, j)) if layer is None else
              pl.BlockSpec((None, k, tn), lambda j, i: (layer, 0, j)))
    in_specs = [pl.BlockSpec((tm, k), lambda j, i: (i, 0)), w_spec]
    args = [x, w]
    blk = _nbytes((tm, k), x.dtype) + _nbytes((k, tn), w.dtype)
    for kind, arr in extras:
        if kind == "row":
            in_specs.append(pl.BlockSpec((1, tn), lambda j, i: (0, j)))
            args.append(arr.reshape(1, n).astype(F32))
            blk += _nbytes((8, tn), F32)
        else:
            in_specs.append(pl.BlockSpec((tm, tn), lambda j, i: (i, j)))
            args.append(arr)
            blk += _nbytes((tm, tn), arr.dtype)
    out_specs = [pl.BlockSpec((tm, tn), lambda j, i: (i, j)) for _ in out_dtypes]
    out_shape = [jax.ShapeDtypeStruct((m, n), dt) for dt in out_dtypes]
    blk += sum(_nbytes((tm, tn), dt) for dt in out_dtypes)
    outs = pl.pallas_call(
        functools.partial(_mm_kernel, epi=epi, n_extra=len(extras), n_out=len(out_dtypes), cast_w=cast_w),
        grid=(n // tn, m // tm),
        in_specs=in_specs, out_specs=out_specs, out_shape=out_shape,
        scratch_shapes=[pltpu.VMEM((k, tn), BF16)] if cast_w else [],
        compiler_params=_cparams(("parallel", "arbitrary"), blk,
                                 (_nbytes((k, tn), BF16) if cast_w else 0) + 3 * _nbytes((tm, tn), F32)),
        name=name,
    )(*args)
    return outs


def _epi_id(acc):
    return (acc,)


def _epi_silu(acc):
    return (acc * jax.nn.sigmoid(acc),)


def _epi_sigmoid(acc):
    return (jax.nn.sigmoid(acc),)


def _epi_scale(acc, *, scale):
    return (acc * scale,)


def _epi_residual(acc, res):
    return (res + acc,)


def _epi_hgrn_forget(acc, lb):
    lbh = jnp.maximum(lb, 0.0)
    a = jnp.log(lbh)
    c = jnp.log1p(-lbh) + _log_sigmoid(acc)
    logf = jnp.maximum(a, c) + jnp.log1p(jnp.exp(-jnp.abs(a - c)))
    kk = (1.0 - lbh) * jax.nn.sigmoid(-acc)
    return logf, kk


def _epi_hgrn_forget_no_floor(acc):
    e = jnp.exp(-jnp.abs(acc))
    logf = -(jnp.maximum(-acc, 0.0) + jnp.log1p(e))
    kk = jnp.where(acc >= 0.0, e, 1.0) / (1.0 + e)
    return logf, kk


def _epi_bias_gelu(acc, b):
    x = acc + b
    return (0.5 * x * (1.0 + lax.erf(x * (2.0 ** -0.5))),)


def _epi_mlstm_gates(acc, b, *, heads):
    gates = GATE_CAP * jnp.tanh((acc + b) / GATE_CAP)
    col = lax.broadcasted_iota(jnp.int32, acc.shape, 1)
    return (jnp.where(col < heads, gates, _log_sigmoid(gates)),)


def _hgrn_levels(chunk, t_real):
    return tuple(m for m in (2 ** p for p in range(int(math.log2(chunk)) - 1, -1, -1)) if m < t_real)


def _hgrn_sum_matrix(chunk, levels, t_len):
    t = np.arange(chunk)[:, None]
    r = np.arange(chunk)[None, :]
    same = (t // t_len) == (r // t_len)
    mats = [(r <= t) & same, (r > t) & same]
    for m in levels:
        mid = (t // (2 * m)) * (2 * m) + m - 1
        second = (t % (2 * m)) >= m
        mats.append(np.where(second, (r > mid) & (r <= t), (r > t) & (r <= mid)))
    return np.concatenate(mats, axis=0).astype(np.float32)


def _hgrn_kernel(*refs, heads, chunk, levels, dk, group, n_aliased, slot):
    q_ref, lf_ref, k_ref, v_ref, g_ref, onorm_ref, d_ref, s0_ref = refs[:8]
    y_ref, s_ref, o_scr = refs[8 + n_aliased:]
    c = pl.program_id(1)
    t_len = chunk // group
    dv = s_ref.shape[-1]

    @pl.when(c == 0)
    def _():
        for other in range(s_ref.shape[0]):
            if other != slot:
                s_ref[other] = jnp.zeros(s_ref.shape[1:], F32)
        s_ref[slot] = s0_ref[0]

    row = lax.broadcasted_iota(jnp.int32, (chunk, 1), 0)
    in_seq = [None] if group == 1 else [(row >> int(math.log2(t_len))) == s for s in range(group)]
    seq_ones = [jnp.ones((chunk, dv), BF16) if mask is None else
                jnp.where(mask, jnp.ones((chunk, dv), F32), 0.0).astype(BF16) for mask in in_seq]
    r2 = lax.broadcasted_iota(jnp.int32, (chunk, chunk), 0)
    c2 = lax.broadcasted_iota(jnp.int32, (chunk, chunk), 1)

    lf_hi, lf_lo = _split2(lf_ref[...])
    xs = _dot(d_ref[...], jnp.concatenate([lf_hi, lf_lo], axis=0))
    q = q_ref[...].astype(F32)
    k = k_ref[...].astype(F32)
    vb = v_ref[...].astype(BF16)
    ex_b = jnp.exp(xs[0:chunk])
    q_in = q * ex_b
    k_out = k * jnp.exp(xs[chunk:2 * chunk])
    q_in = [(q_in if mask is None else jnp.where(mask, q_in, 0.0)).astype(BF16) for mask in in_seq]
    k_out = [(k_out if mask is None else jnp.where(mask, k_out, 0.0)).astype(BF16) for mask in in_seq]
    qs = [q_ref[...].astype(BF16)]
    ks = [k_ref[...].astype(BF16)]
    masks = [r2 == c2]
    for li, m in enumerate(levels):
        ex = jnp.exp(xs[(2 + li) * chunk:(3 + li) * chunk])
        second = (row & m) != 0
        qs.append(jnp.where(second, q * ex, 0.0).astype(BF16))
        ks.append(jnp.where(second, 0.0, k * ex).astype(BF16))
        shift = int(math.log2(2 * m))
        masks.append(None if 2 * m == chunk else (r2 >> shift) == (c2 >> shift))

    scores = []
    for h in range(heads):
        cols = slice(h * dk, (h + 1) * dk)
        a = None
        for qs_l, ks_l, mask in zip(qs, ks, masks):
            al = _dot_nt(qs_l[:, cols], ks_l[:, cols])
            if mask is not None:
                al = jnp.where(mask, al, 0.0)
            a = al if a is None else a + al
        scores.append(a.astype(BF16))

    for h in range(heads):
        cols = slice(h * dk, (h + 1) * dk)
        o = _dot(scores[h], vb[:, cols])
        for s in range(group):
            st = s_ref[slot, s, h]
            o = o + _dot(q_in[s][:, cols], st.astype(BF16))
            decay = jnp.exp(_dot_tn(lf_hi[:, cols], seq_ones[s]) + _dot_tn(lf_lo[:, cols], seq_ones[s]))
            s_ref[slot, s, h] = st * decay + _dot_tn(k_out[s][:, cols], vb[:, cols])
        o_scr[:, cols] = o

    o = o_scr[...]
    y = o * lax.rsqrt(jnp.mean(o * o, axis=-1, keepdims=True) + EPS) * onorm_ref[...]
    y_ref[...] = (y * g_ref[...]).astype(y_ref.dtype)


def _hgrn_scan(q, lf, k, v, g, onorm, s0_all, layer, s_out_prev, y_prev, *, row_start, batch, seq, chunk, group):
    d = q.shape[1]
    heads, dk, dv = s0_all.shape[2:]
    t_chunk = chunk // group
    n_chunks = seq // t_chunk
    assert group == 1 or n_chunks == 1
    assert row_start % chunk == 0 and batch % group == 0
    levels = _hgrn_levels(chunk, t_chunk)
    dmat = _hgrn_sum_matrix(chunk, levels, t_chunk)
    dmat = jnp.asarray(np.concatenate([dmat, dmat], axis=1), dtype=BF16)
    first_blk = row_start // chunk
    tok = pl.BlockSpec((chunk, d), lambda b, c: (first_blk + b * n_chunks + c, 0))
    in_layer = min(layer, s0_all.shape[0] - 1)
    st_in = pl.BlockSpec((1, group, heads, dk, dv), lambda b, c: (in_layer, b, 0, 0, 0))
    n_layers = s_out_prev if isinstance(s_out_prev, int) else s_out_prev.shape[0]
    fresh_state = isinstance(s_out_prev, int)
    slots = n_layers if fresh_state else 1
    st_out = pl.BlockSpec((slots, group, heads, dk, dv), lambda b, c: (0 if fresh_state else layer, b, 0, 0, 0))
    in_specs = [tok, tok, tok, tok, tok,
                pl.BlockSpec((1, d), lambda b, c: (0, 0)),
                pl.BlockSpec(dmat.shape, lambda b, c: (0, 0)),
                st_in]
    args = [q, lf, k, v, g, onorm.reshape(1, d).astype(F32), dmat, s0_all]
    aliases = {}
    for out_idx, prev in ((0, y_prev), (1, s_out_prev)):
        if not isinstance(prev, int):
            aliases[len(args)] = out_idx
            in_specs.append(pl.BlockSpec(memory_space=pl.ANY))
            args.append(prev)
    y_rows = y_prev if isinstance(y_prev, int) else y_prev.shape[0]
    blk = (sum(_nbytes((chunk, d), a.dtype) for a in (q, lf, k, v, g)) + _nbytes((chunk, d), BF16) + (1 + slots) * _nbytes((group, heads, dk, dv), F32)
           + _nbytes(dmat.shape, BF16))
    n_exp = 2 + len(levels)
    y, s = pl.pallas_call(
        functools.partial(_hgrn_kernel, heads=heads, chunk=chunk, levels=levels, dk=dk, group=group,
                          n_aliased=len(aliases), slot=layer if fresh_state else 0),
        grid=(batch // group, n_chunks),
        in_specs=in_specs,
        out_specs=[tok, st_out],
        out_shape=[jax.ShapeDtypeStruct((y_rows, d), BF16),
                   jax.ShapeDtypeStruct((n_layers, batch, heads, dk, dv), F32)],
        scratch_shapes=[pltpu.VMEM((chunk, d), F32)],
        input_output_aliases=aliases,
        compiler_params=_cparams(("parallel", "arbitrary"), blk, (3 * n_exp + 8) * _nbytes((chunk, d), F32)),
        name="hgrn_scan",
    )(*args)
    return y, s


def _mlstm_kernel(*refs, heads, chunk, dqk, dv, group, n_aliased):
    q_ref, k_ref, v_ref, og_ref, gt_ref, hn_ref, tri_ref, sel_ref, c0_ref, n0_ref, m0_ref = refs[:11]
    y_ref, c_ref, n_ref, m_ref = refs[11 + n_aliased:]
    c = pl.program_id(1)
    t_len = chunk // group

    @pl.when(c == 0)
    def _():
        c_ref[...] = c0_ref[...]
        n_ref[...] = n0_ref[...]
        m_ref[...] = m0_ref[...]

    gates = gt_ref[...]
    tri = tri_ref[...]
    sel = sel_ref[...]
    g3 = _split3(gates)
    cum = sum(_dot(tri, p) for p in g3)
    gates_t = sum(_dot_nt(sel, p) for p in g3)
    cum_t = sum(_dot_nt(sel, p) for p in _split3(cum))
    row = lax.broadcasted_iota(jnp.int32, (chunk, 1), 0)
    r2 = lax.broadcasted_iota(jnp.int32, (chunk, chunk), 0)
    c2 = lax.broadcasted_iota(jnp.int32, (chunk, chunk), 1)
    causal = c2 <= r2
    in_seq = [None]
    if group > 1:
        shift = int(math.log2(t_len))
        causal = causal & ((r2 >> shift) == (c2 >> shift))
        in_seq = [(row >> shift) == s for s in range(group)]

    def per_row(vals):
        if group == 1:
            return vals[0]
        out = jnp.where(in_seq[0], vals[0], 0.0)
        for s in range(1, group):
            out = jnp.where(in_seq[s], vals[s], out)
        return out

    lane = lax.broadcasted_iota(jnp.int32, (1, LANE), 1)
    m_rows = [m_ref[s] for s in range(group)]
    m_new = list(m_rows)
    for h in range(heads):
        qc = slice(h * dqk, (h + 1) * dqk)
        vc = slice(h * dv, (h + 1) * dv)
        qb = q_ref[:, qc].astype(BF16)
        kb = k_ref[:, qc].astype(BF16)
        qh = qb.astype(F32)
        kh = kb.astype(F32)
        vb = v_ref[:, vc].astype(BF16)
        b_c = cum[:, heads + h:heads + h + 1]
        ig_c = gates[:, h:h + 1]
        b_r = cum_t[heads + h:heads + h + 1, :]
        ig_r = gates_t[h:h + 1, :]
        m_h = [m_rows[s][:, h:h + 1] for s in range(group)]
        dlog = jnp.where(causal, b_c - b_r + ig_r, -jnp.inf)
        inter = b_c + per_row(m_h)
        mt = jnp.maximum(inter, jnp.max(dlog, axis=-1, keepdims=True))
        wts = jnp.exp(dlog - mt) * _dot_nt(qb, kb)
        sc = jnp.exp(inter - mt)
        c_h = [c_ref[s, h] for s in range(group)]
        n_h = [n_ref[s, h:h + 1, :] for s in range(group)]
        if group == 1:
            q_c = _dot(qb, c_h[0].astype(BF16))
        else:
            q_c = sum(_dot(jnp.where(in_seq[s], qh.astype(F32), 0.0).astype(BF16), c_h[s].astype(BF16))
                      for s in range(group))
        num = sc * q_c + _dot(wts.astype(BF16), vb)
        den = sc * jnp.sum(qh * per_row(n_h), axis=-1, keepdims=True) + jnp.sum(wts, axis=-1, keepdims=True)
        out = num / jnp.maximum(jnp.abs(den), jnp.exp(-mt))
        for s in range(group):
            last = (s + 1) * t_len - 1
            m_last = mt[last:last + 1, :]
            b_last = b_c[last:last + 1, :]
            sc_state = jnp.exp(b_last + m_h[s] - m_last)
            w_k = jnp.exp(b_last - b_c + ig_c - m_last)
            if group > 1:
                w_k = jnp.where(in_seq[s], w_k, 0.0)
            kw = w_k * kh
            c_ref[s, h] = sc_state * c_h[s] + _dot_tn(kw.astype(BF16), vb)
            n_ref[s, h:h + 1, :] = sc_state * n_h[s] + jnp.sum(kw, axis=0, keepdims=True)
            m_new[s] = jnp.where(lane == h, m_last, m_new[s])
        y = out * lax.rsqrt(jnp.mean(out * out, axis=-1, keepdims=True) + EPS) * hn_ref[:, vc]
        y_ref[:, vc] = (y * og_ref[:, vc]).astype(y_ref.dtype)
    for s in range(group):
        m_ref[s] = m_new[s]


def _mlstm_scan(q, k, v, og, gates, hnorm, c0, n0, m0, y_prev, *, row_start, batch, seq, chunk, group):
    heads, dqk, dv = c0.shape[1], c0.shape[2], c0.shape[3]
    dq_all, dv_all = q.shape[1], v.shape[1]
    t_chunk = chunk // group
    n_chunks = seq // t_chunk
    assert 2 * heads <= 16 and (group == 1 or n_chunks == 1)
    assert row_start % chunk == 0 and batch % group == 0
    idx = np.arange(chunk)
    same_seq = (idx[:, None] // t_chunk) == (idx[None, :] // t_chunk)
    tri = jnp.asarray(np.tril(np.ones((chunk, chunk), np.float32)) * same_seq, dtype=BF16)
    sel = jnp.asarray(np.eye(16, LANE, dtype=np.float32), dtype=BF16)
    m0p = jnp.pad(m0, ((0, 0), (0, LANE - heads))).reshape(batch, 1, LANE)
    first_blk = row_start // chunk
    tq = pl.BlockSpec((chunk, dq_all), lambda b, c: (first_blk + b * n_chunks + c, 0))
    tv = pl.BlockSpec((chunk, dv_all), lambda b, c: (first_blk + b * n_chunks + c, 0))
    tg = pl.BlockSpec((chunk, LANE), lambda b, c: (first_blk + b * n_chunks + c, 0))
    sc_ = pl.BlockSpec((group, heads, dqk, dv), lambda b, c: (b, 0, 0, 0))
    sn_ = pl.BlockSpec((group, heads, dqk), lambda b, c: (b, 0, 0))
    sm_ = pl.BlockSpec((group, 1, LANE), lambda b, c: (b, 0, 0))
    in_specs = [tq, tq, tv, tv, tg,
                pl.BlockSpec((1, dv_all), lambda b, c: (0, 0)),
                pl.BlockSpec(tri.shape, lambda b, c: (0, 0)),
                pl.BlockSpec(sel.shape, lambda b, c: (0, 0)),
                sc_, sn_, sm_]
    args = [q, k, v, og, gates, hnorm.reshape(1, dv_all).astype(F32), tri, sel, c0, n0, m0p]
    aliases = {}
    if not isinstance(y_prev, int):
        aliases[len(args)] = 0
        in_specs.append(pl.BlockSpec(memory_space=pl.ANY))
        args.append(y_prev)
    y_rows = y_prev if isinstance(y_prev, int) else y_prev.shape[0]
    blk = (_nbytes((chunk, dq_all), q.dtype) + _nbytes((chunk, dq_all), k.dtype) + _nbytes((chunk, dv_all), v.dtype)
           + _nbytes((chunk, dv_all), og.dtype) + _nbytes((chunk, LANE), F32)
           + _nbytes((chunk, dv_all), BF16) + 2 * _nbytes((group,) + c0.shape[1:], F32))
    y, c_out, n_out, m_out = pl.pallas_call(
        functools.partial(_mlstm_kernel, heads=heads, chunk=chunk, dqk=dqk, dv=dv, group=group,
                          n_aliased=len(aliases)),
        grid=(batch // group, n_chunks),
        in_specs=in_specs,
        out_specs=[tv, sc_, sn_, sm_],
        out_shape=[jax.ShapeDtypeStruct((y_rows, dv_all), BF16),
                   jax.ShapeDtypeStruct(c0.shape, F32),
                   jax.ShapeDtypeStruct(n0.shape, F32),
                   jax.ShapeDtypeStruct((batch, 1, LANE), F32)],
        input_output_aliases=aliases,
        compiler_params=_cparams(("parallel", "arbitrary"), blk, 8 * _nbytes((chunk, dv_all), F32)),
        name="mlstm_scan",
    )(*args)
    return y, c_out, n_out, m_out[:, 0, :heads]


def _gmlp_kernel(u_ref, v_ref, vg_ref, vb_ref, w_ref, bs_ref, o_ref, vn_ref, *, groups, gd):
    v = v_ref[...]
    mu = jnp.mean(v, axis=-1, keepdims=True)
    xc = v - mu
    vn = xc * lax.rsqrt(jnp.mean(xc * xc, axis=-1, keepdims=True) + EPS) * vg_ref[...] + vb_ref[...]
    vn_ref[...] = vn
    n = v.shape[0]
    causal = (lax.broadcasted_iota(jnp.int32, (n, n), 1) <= lax.broadcasted_iota(jnp.int32, (n, n), 0))
    bs = bs_ref[0]
    for g in range(groups):
        cols = slice(g * gd, (g + 1) * gd)
        wg = jnp.where(causal, w_ref[0, g], 0.0).astype(BF16)
        mix = _dot(wg, vn[:, cols].astype(BF16)) + bs[:, g:g + 1]
        o_ref[:, cols] = (u_ref[:, cols] * mix).astype(o_ref.dtype)


def _gmlp_gate(z, vg, vb, w_stack, bs_stack, *, n_prompt_chunks):
    m, d2 = z.shape
    d = d2 // 2
    groups, chunk = w_stack.shape[1], w_stack.shape[2]
    gd = d // groups
    assert m % chunk == 0

    def which(i):
        return jnp.minimum(i // n_prompt_chunks, 1)

    blk = (2 * _nbytes((chunk, d), F32) + _nbytes((chunk, d), BF16) + _nbytes((chunk, d), F32)
           + _nbytes((groups, chunk, chunk), F32))
    o, vn = pl.pallas_call(
        functools.partial(_gmlp_kernel, groups=groups, gd=gd),
        grid=(m // chunk,),
        in_specs=[pl.BlockSpec((chunk, d), lambda i: (i, 0)),
                  pl.BlockSpec((chunk, d), lambda i: (i, 1)),
                  pl.BlockSpec((1, d), lambda i: (0, 0)),
                  pl.BlockSpec((1, d), lambda i: (0, 0)),
                  pl.BlockSpec((1, groups, chunk, chunk), lambda i: (which(i), 0, 0, 0)),
                  pl.BlockSpec((1, chunk, groups), lambda i: (which(i), 0, 0))],
        out_specs=[pl.BlockSpec((chunk, d), lambda i: (i, 0)),
                   pl.BlockSpec((chunk, d), lambda i: (i, 0))],
        out_shape=[jax.ShapeDtypeStruct((m, d), BF16), jax.ShapeDtypeStruct((m, d), F32)],
        compiler_params=_cparams(("parallel",), blk, 4 * _nbytes((chunk, d), F32)),
        name="gmlp_gate",
    )(z, z, vg.reshape(1, d).astype(F32), vb.reshape(1, d).astype(F32), w_stack, bs_stack)
    return o, vn


def _causal_conv(tap, cw_ref, cb_ref, conv_w):
    y = cb_ref[...] + cw_ref[conv_w - 1:conv_w, :] * tap(0)
    for back in range(1, conv_w):
        y = y + cw_ref[conv_w - 1 - back:conv_w - back, :] * tap(back)
    return y


def _ffn_up_prompt_kernel(h_ref, wa_ref, wg_ref, cwa_ref, cwg_ref, cba_ref, cbg_ref,
                          _, act_ref, tail_a_ref, tail_g_ref, wb_scr, up_scr, *, tiles_per_seq, conv_w, row_chunk):
    i = pl.program_id(1)
    tm = h_ref.shape[0]
    n_chunks = tm // row_chunk

    @pl.when(i == 0)
    def _():
        wb_scr[0] = wa_ref[...].astype(BF16)
        wb_scr[1] = wg_ref[...].astype(BF16)

    @pl.when(lax.rem(i, tiles_per_seq) == 0)
    def _():
        up_scr[:, 0:8, :] = jnp.zeros((2, 8, up_scr.shape[2]), F32)

    def multiply(c):
        hb = h_ref[c * row_chunk:(c + 1) * row_chunk, :]
        for idx in range(2):
            up_scr[idx, 8 + c * row_chunk:8 + (c + 1) * row_chunk, :] = _dot(hb, wb_scr[idx])

    def finish(c):
        lo = 8 + c * row_chunk
        ya = _causal_conv(lambda back: up_scr[0, lo - back:lo - back + row_chunk, :], cwa_ref, cba_ref, conv_w)
        yg = _causal_conv(lambda back: up_scr[1, lo - back:lo - back + row_chunk, :], cwg_ref, cbg_ref, conv_w)
        act_ref[c * row_chunk:(c + 1) * row_chunk, :] = (ya * (yg * jax.nn.sigmoid(yg))).astype(act_ref.dtype)

    multiply(0)
    for c in range(1, n_chunks):
        multiply(c)
        finish(c - 1)
    finish(n_chunks - 1)
    for idx, tail_ref in ((0, tail_a_ref), (1, tail_g_ref)):
        last = up_scr[idx, tm:tm + 8, :]
        tail_ref[0] = last
        up_scr[idx, 0:8, :] = last


def _ffn_up_prompt(h, w_up, layer, cw, cb, act_prev, *, rows, seq):
    _, k, f2 = w_up.shape
    f = f2 // 2
    conv_w = cw.shape[0]
    tm = _pick_tile(seq, 1024, 16)
    tf = _pick_tile(f, FFN_TF, LANE)
    nf = f // tf
    n_tiles = rows // tm
    row_chunk = _pick_tile(tm, FFN_ROW_CHUNK, 16)
    blk = (_nbytes((tm, k), BF16) + 2 * _nbytes((k, tf), F32) + _nbytes((tm, tf), BF16)
           + 2 * _nbytes((8, tf), F32) + 6 * _nbytes((8, tf), F32))
    tail = jax.ShapeDtypeStruct((n_tiles, 8, f), F32)
    act, tail_a, tail_g = pl.pallas_call(
        functools.partial(_ffn_up_prompt_kernel, tiles_per_seq=seq // tm, conv_w=conv_w, row_chunk=row_chunk),
        grid=(nf, n_tiles),
        in_specs=[pl.BlockSpec((tm, k), lambda j, i: (i, 0)),
                  pl.BlockSpec((None, k, tf), lambda j, i: (layer, 0, j)),
                  pl.BlockSpec((None, k, tf), lambda j, i: (layer, 0, j + nf)),
                  pl.BlockSpec((conv_w, tf), lambda j, i: (0, j)),
                  pl.BlockSpec((conv_w, tf), lambda j, i: (0, j + nf)),
                  pl.BlockSpec((1, tf), lambda j, i: (0, j)),
                  pl.BlockSpec((1, tf), lambda j, i: (0, j + nf)),
                  pl.BlockSpec(memory_space=pl.ANY)],
        out_specs=[pl.BlockSpec((tm, tf), lambda j, i: (i, j)),
                   pl.BlockSpec((1, 8, tf), lambda j, i: (i, 0, j)),
                   pl.BlockSpec((1, 8, tf), lambda j, i: (i, 0, j))],
        out_shape=[jax.ShapeDtypeStruct(act_prev.shape, BF16), tail, tail],
        input_output_aliases={7: 0},
        scratch_shapes=[pltpu.VMEM((2, k, tf), BF16), pltpu.VMEM((2, 8 + tm, tf), F32)],
        compiler_params=_cparams(("parallel", "arbitrary"), blk,
                                 2 * _nbytes((k, tf), BF16) + 2 * _nbytes((8 + tm, tf), F32)
                                 + 6 * _nbytes((row_chunk, tf), F32)),
        name="ffn_up_conv_prompt",
    )(h, w_up, w_up, cw, cw, cb.reshape(1, f2), cb.reshape(1, f2), act_prev)
    tiles_per_seq = seq // tm
    keep = slice(8 - (conv_w - 1), 8)
    tails = jnp.concatenate([tail_a[tiles_per_seq - 1::tiles_per_seq, keep], tail_g[tiles_per_seq - 1::tiles_per_seq, keep]],
                            axis=-1)
    return act, tails


def _ffn_up_sample_kernel(*refs, steps, conv_w):
    h_ref, wa_ref, wg_ref, cwa_ref, cwg_ref, cba_ref, cbg_ref = refs[:7]
    bufs_a = refs[7:7 + conv_w - 1]
    bufs_g = refs[7 + conv_w - 1:7 + 2 * (conv_w - 1)]
    act_ref, upa_ref, upg_ref = refs[7 + 2 * (conv_w - 1):]
    bs = h_ref.shape[0] // steps
    hb = h_ref[...]

    def branch(w_ref, bufs, cw_ref, cb_ref, up_ref):
        up = _dot(hb, w_ref[...].astype(BF16))
        up_ref[...] = up

        def at(t):
            return up[t * bs:(t + 1) * bs, :] if t >= 0 else bufs[conv_w - 1 + t][...]

        return [_causal_conv(lambda back, t=t: at(t - back), cw_ref, cb_ref, conv_w) for t in range(steps)]

    ya = branch(wa_ref, bufs_a, cwa_ref, cba_ref, upa_ref)
    yg = branch(wg_ref, bufs_g, cwg_ref, cbg_ref, upg_ref)
    for t in range(steps):
        act_ref[t * bs:(t + 1) * bs, :] = (ya[t] * (yg[t] * jax.nn.sigmoid(yg[t]))).astype(act_ref.dtype)


def _ffn_up_sample(h_t, w_up, layer, cw, cb, conv_state, *, steps):
    _, k, f2 = w_up.shape
    f = f2 // 2
    conv_w = cw.shape[0]
    ms = h_t.shape[0]
    bs = ms // steps
    assert steps >= conv_w - 1 and bs % 16 == 0 and conv_state.shape[1] == bs
    tf = _pick_tile(f, 256, LANE)
    nf = f // tf
    state_specs = [pl.BlockSpec((None, bs, tf), lambda j, r=r, off=off: (r, 0, off + j))
                   for off in (0, nf) for r in range(conv_w - 1)]
    blk = (_nbytes((ms, k), BF16) + 2 * _nbytes((k, tf), F32) + 2 * _nbytes((conv_w - 1, bs, tf), F32)
           + _nbytes((ms, tf), BF16) + 2 * _nbytes((ms, tf), F32) + 6 * _nbytes((8, tf), F32))
    up_shape = jax.ShapeDtypeStruct((ms, f), F32)
    act, up_a, up_g = pl.pallas_call(
        functools.partial(_ffn_up_sample_kernel, steps=steps, conv_w=conv_w),
        grid=(nf,),
        in_specs=[pl.BlockSpec((ms, k), lambda j: (0, 0)),
                  pl.BlockSpec((None, k, tf), lambda j: (layer, 0, j)),
                  pl.BlockSpec((None, k, tf), lambda j: (layer, 0, j + nf)),
                  pl.BlockSpec((conv_w, tf), lambda j: (0, j)),
                  pl.BlockSpec((conv_w, tf), lambda j: (0, j + nf)),
                  pl.BlockSpec((1, tf), lambda j: (0, j)),
                  pl.BlockSpec((1, tf), lambda j: (0, j + nf))] + state_specs,
        out_specs=[pl.BlockSpec((ms, tf), lambda j: (0, j))] * 3,
        out_shape=[jax.ShapeDtypeStruct((ms, f), BF16), up_shape, up_shape],
        compiler_params=_cparams(("parallel",), blk, 2 * _nbytes((k, tf), BF16) + 8 * _nbytes((ms, tf), F32)),
        name="ffn_up_conv_sample",
    )(h_t, w_up, w_up, cw, cw, cb.reshape(1, f2), cb.reshape(1, f2), *([conv_state] * (2 * (conv_w - 1))))
    keep = slice((steps - (conv_w - 1)) * bs, ms)
    up_tail = jnp.concatenate([up_a[keep], up_g[keep]], axis=-1).reshape(conv_w - 1, bs, f2).transpose(1, 0, 2)
    return act, up_tail


def _forward(x_prompt, x_sample, state_hgrn_S, state_mlstm_C, state_mlstm_n, state_mlstm_m, state_ffn_conv,
             norm_mix, norm_ffn, norm_final,
             hgrn_w_q, hgrn_w_f, hgrn_w_i, hgrn_w_g, hgrn_lb, hgrn_onorm, hgrn_w_o,
             mlstm_w_q, mlstm_w_k, mlstm_w_v, mlstm_w_og, mlstm_w_if, mlstm_b_if, mlstm_hnorm, mlstm_w_out,
             gmlp_w_in, gmlp_b_in, gmlp_vnorm_g, gmlp_vnorm_b, gmlp_w_s, gmlp_b_s, gmlp_w_out,
             ffn_w_up, ffn_conv_w, ffn_conv_b, ffn_w_down):
    bp, tp, d = x_prompt.shape
    bs, ts, _ = x_sample.shape
    mp, ms = bp * tp, bs * ts
    m = mp + ms
    depth = norm_mix.shape[0]
    hg_heads, hg_dk, hg_dv = state_hgrn_S.shape[2:]
    ml_heads, ml_dqk, ml_dv = state_mlstm_C.shape[2:]

    x = jnp.concatenate([x_prompt.reshape(mp, d), x_sample.reshape(ms, d)], axis=0)

    lb = jax.nn.softmax(hgrn_lb.astype(F32), axis=0)
    lbs = jnp.cumsum(lb, axis=0) - lb[0]

    w_down = ffn_w_down.astype(BF16)
    act = jnp.zeros((m, ffn_w_down.shape[1]), BF16)
    out_c, out_n, out_m, out_v, out_conv = [], [], [], [], []
    s_p_all = s_s_all = state_hgrn_S.shape[0]
    for i in range(depth):
        j = i // N_MIXERS
        kind = i % N_MIXERS
        h = _rmsnorm(x, norm_mix[i], BF16)
        if kind == 0:
            (q,) = _matmul(h, hgrn_w_q, _epi_silu, out_dtypes=(BF16,), name="hgrn_q", layer=j)
            if j == 0:
                lf, k = _matmul(h, hgrn_w_f, _epi_hgrn_forget_no_floor,
                                out_dtypes=(F32, BF16), name="hgrn_f", layer=j)
            else:
                lf, k = _matmul(h, hgrn_w_f, _epi_hgrn_forget, extras=[("row", lbs[j])],
                                out_dtypes=(F32, BF16), name="hgrn_f", layer=j)
            (v,) = _matmul(h, hgrn_w_i, _epi_id, out_dtypes=(BF16,), name="hgrn_i", layer=j)
            (g,) = _matmul(h, hgrn_w_g, _epi_sigmoid, out_dtypes=(BF16,), name="hgrn_g", layer=j)
            y, s_p_all = _hgrn_scan(q, lf, k, v, g, hgrn_onorm[j],
                                    jnp.zeros((1, bp, hg_heads, hg_dk, hg_dv), F32), j, s_p_all, h,
                                    row_start=0, batch=bp, seq=tp, chunk=min(HGRN_CHUNK, tp), group=1)
            group = max(1, SAMPLE_ROWS // ts)
            y, s_s_all = _hgrn_scan(q, lf, k, v, g, hgrn_onorm[j], state_hgrn_S.astype(F32), j, s_s_all, y,
                                    row_start=mp, batch=bs, seq=ts, chunk=group * ts, group=group)
            (x,) = _matmul(y, hgrn_w_o, _epi_residual, extras=[("tile", x)], name="hgrn_o", layer=j)
        elif kind == 1:
            (q,) = _matmul(h, mlstm_w_q, _epi_id, out_dtypes=(BF16,), name="mlstm_q", layer=j)
            (k,) = _matmul(h, mlstm_w_k, functools.partial(_epi_scale, scale=ml_dqk ** -0.5),
                           out_dtypes=(BF16,), name="mlstm_k", layer=j)
            (v,) = _matmul(h, mlstm_w_v, _epi_id, out_dtypes=(BF16,), name="mlstm_v", layer=j)
            (og,) = _matmul(h, mlstm_w_og, _epi_sigmoid, out_dtypes=(BF16,), name="mlstm_og", layer=j)
            w_if = jnp.pad(mlstm_w_if[j], ((0, 0), (0, LANE - 2 * ml_heads)))
            b_if = jnp.pad(mlstm_b_if[j], (0, LANE - 2 * ml_heads))
            (gates,) = _matmul(h, w_if, functools.partial(_epi_mlstm_gates, heads=ml_heads),
                               extras=[("row", b_if)], name="mlstm_if")
            y, c_p, n_p, m_p = _mlstm_scan(
                q, k, v, og, gates, mlstm_hnorm[j],
                jnp.zeros((bp, ml_heads, ml_dqk, ml_dv), F32), jnp.zeros((bp, ml_heads, ml_dqk), F32),
                jnp.zeros((bp, ml_heads), F32), h,
                row_start=0, batch=bp, seq=tp, chunk=min(MLSTM_CHUNK, tp), group=1)
            group = max(1, SAMPLE_ROWS // ts)
            y, c_s, n_s, m_s = _mlstm_scan(
                q, k, v, og, gates, mlstm_hnorm[j],
                state_mlstm_C[j].astype(F32), state_mlstm_n[j].astype(F32), state_mlstm_m[j].astype(F32), y,
                row_start=mp, batch=bs, seq=ts, chunk=group * ts, group=group)
            out_c.append((c_p, c_s))
            out_n.append((n_p, n_s))
            out_m.append((m_p, m_s))
            (x,) = _matmul(y, mlstm_w_out, _epi_residual, extras=[("tile", x)], name="mlstm_out", layer=j)
        else:
            (z,) = _matmul(h, gmlp_w_in, _epi_bias_gelu, extras=[("row", gmlp_b_in[j])], name="gmlp_in", layer=j)
            groups = gmlp_w_s.shape[1]
            lp = min(GMLP_CHUNK, tp)
            ls = min(GMLP_CHUNK, ts)
            reps = GMLP_CHUNK // ls
            w_s_blk = jnp.einsum("ab,gts->gatbs", jnp.eye(reps, dtype=F32), gmlp_w_s[j][:, :ls, :ls])
            w_s_blk = w_s_blk.reshape(groups, GMLP_CHUNK, GMLP_CHUNK)
            w_p = gmlp_w_s[j][:, :lp, :lp]
            w_stack = jnp.stack([w_p, w_s_blk])
            bs_stack = jnp.stack([gmlp_b_s[j][:, :lp].T, jnp.tile(gmlp_b_s[j][:, :ls].T, (reps, 1))])
            y, vn = _gmlp_gate(z, gmlp_vnorm_g[j], gmlp_vnorm_b[j], w_stack, bs_stack,
                               n_prompt_chunks=mp // GMLP_CHUNK)
            out_v.append(vn[mp:].reshape(bs, ts, d))
            (x,) = _matmul(y, gmlp_w_out, _epi_residual, extras=[("tile", x)], name="gmlp_out", layer=j)

        h = _rmsnorm(x, norm_ffn[i], BF16)
        act, tail_p = _ffn_up_prompt(h, ffn_w_up, i, ffn_conv_w[i], ffn_conv_b[i], act, rows=mp, seq=tp)
        h_t = h[mp:].reshape(bs, ts, d).transpose(1, 0, 2).reshape(ms, d)
        act_t, tail_s = _ffn_up_sample(h_t, ffn_w_up, i, ffn_conv_w[i], ffn_conv_b[i],
                                       state_ffn_conv[i].astype(F32).transpose(1, 0, 2), steps=ts)
        act_s = act_t.reshape(ts, bs, -1).transpose(1, 0, 2).reshape(ms, -1)
        act = lax.dynamic_update_slice(act, act_s, (mp, 0))
        out_conv.append((tail_p, tail_s))
        (x,) = _matmul(act, w_down, _epi_residual, extras=[("tile", x)], name="ffn_down", layer=i)

    y_p = _rmsnorm(x, norm_final, F32, row_start=0, n_rows=mp).reshape(bp, tp, d)
    y_s = _rmsnorm(x, norm_final, F32, row_start=mp, n_rows=ms).reshape(bs, ts, d)
    stack = lambda pairs, idx: jnp.stack([p[idx] for p in pairs])
    return (y_p, y_s,
            s_p_all, s_s_all, stack(out_c, 0), stack(out_c, 1),
            stack(out_n, 0), stack(out_n, 1), stack(out_m, 0), stack(out_m, 1),
            jnp.stack(out_v), stack(out_conv, 0), stack(out_conv, 1))


_forward_jit = jax.jit(_forward)


def kernel(x_prompt, x_sample, state_hgrn_S, state_mlstm_C, state_mlstm_n, state_mlstm_m, state_ffn_conv, norm_mix, norm_ffn, norm_final, hgrn_w_q, hgrn_w_f, hgrn_w_i, hgrn_w_g, hgrn_lb, hgrn_onorm, hgrn_w_o, mlstm_w_q, mlstm_w_k, mlstm_w_v, mlstm_w_og, mlstm_w_if, mlstm_b_if, mlstm_hnorm, mlstm_w_out, gmlp_w_in, gmlp_b_in, gmlp_vnorm_g, gmlp_vnorm_b, gmlp_w_s, gmlp_b_s, gmlp_w_out, ffn_w_up, ffn_conv_w, ffn_conv_b, ffn_w_down):
    return _forward_jit(x_prompt, x_sample, state_hgrn_S, state_mlstm_C, state_mlstm_n, state_mlstm_m, state_ffn_conv,
                        norm_mix, norm_ffn, norm_final,
                        hgrn_w_q, hgrn_w_f, hgrn_w_i, hgrn_w_g, hgrn_lb, hgrn_onorm, hgrn_w_o,
                        mlstm_w_q, mlstm_w_k, mlstm_w_v, mlstm_w_og, mlstm_w_if, mlstm_b_if, mlstm_hnorm, mlstm_w_out,
                        gmlp_w_in, gmlp_b_in, gmlp_vnorm_g, gmlp_vnorm_b, gmlp_w_s, gmlp_b_s, gmlp_w_out,
                        ffn_w_up, ffn_conv_w, ffn_conv_b, ffn_w_down)
```

```python
import functools
import math

import numpy as np
import jax
import jax.numpy as jnp
from jax import lax
from jax.experimental import pallas as pl
from jax.experimental.pallas import tpu as pltpu

F32 = jnp.float32
BF16 = jnp.bfloat16

EPS = 1e-6
GATE_CAP = 15.0
N_MIXERS = 3
HGRN_CHUNK = 128
MLSTM_CHUNK = 128
GMLP_CHUNK = 128
FFN_ROW_CHUNK = 256
FFN_TF = 512
SAMPLE_ROWS = 16

LANE = 128
VMEM_PHYSICAL_V7X = 64 * 1024 * 1024
VMEM_BUDGET = VMEM_PHYSICAL_V7X - 8 * 1024 * 1024

MM_TN_MAX = 1024
MM_W_TILE_BYTES = 12 << 20
MM_X_TILE_BYTES = 13 << 20
MM_VMEM_TARGET = 52 << 20


def _nbytes(shape, dtype):
    return int(np.prod(shape)) * jnp.dtype(dtype).itemsize


def _cparams(semantics, block_bytes, temp_bytes=0):
    need = 2 * block_bytes + temp_bytes + (4 << 20)
    return pltpu.CompilerParams(dimension_semantics=semantics,
                                vmem_limit_bytes=int(min(max(need, 16 << 20), VMEM_BUDGET)))


def _pick_tile(n, target, align):
    best = None
    for d in range(align, min(n, target) + 1, align):
        if n % d == 0:
            best = d
    return best if best is not None else n


def _dot(a, b):
    return jnp.dot(a, b, preferred_element_type=F32)


def _dot_nt(a, b):
    return lax.dot_general(a, b, (((1,), (1,)), ((), ())), preferred_element_type=F32)


def _dot_tn(a, b):
    return lax.dot_general(a, b, (((0,), (0,)), ((), ())), preferred_element_type=F32)


def _log_sigmoid(x):
    return -(jnp.maximum(-x, 0.0) + jnp.log1p(jnp.exp(-jnp.abs(x))))


def _split2(x):
    hi = x.astype(BF16)
    lo = (x - hi.astype(F32)).astype(BF16)
    return hi, lo


def _split3(x):
    hi = x.astype(BF16)
    r = x - hi.astype(F32)
    mid = r.astype(BF16)
    lo = (r - mid.astype(F32)).astype(BF16)
    return hi, mid, lo


def _rmsnorm_kernel(x_ref, g_ref, o_ref):
    x = x_ref[...]
    y = x * lax.rsqrt(jnp.mean(x * x, axis=-1, keepdims=True) + EPS)
    o_ref[...] = (y * g_ref[...]).astype(o_ref.dtype)


def _rmsnorm(x, g, out_dtype, row_start=0, n_rows=None):
    m_all, d = x.shape
    n_rows = m_all - row_start if n_rows is None else n_rows
    tm = _pick_tile(math.gcd(n_rows, row_start) if row_start else n_rows, 544, 16)
    off = row_start // tm
    blk = _nbytes((tm, d), F32) + _nbytes((tm, d), out_dtype)
    return pl.pallas_call(
        _rmsnorm_kernel,
        grid=(n_rows // tm,),
        in_specs=[pl.BlockSpec((tm, d), lambda i: (i + off, 0)),
                  pl.BlockSpec((1, d), lambda i: (0, 0))],
        out_specs=pl.BlockSpec((tm, d), lambda i: (i, 0)),
        out_shape=jax.ShapeDtypeStruct((n_rows, d), out_dtype),
        compiler_params=_cparams(("parallel",), blk, _nbytes((tm, d), F32)),
        name="rmsnorm",
    )(x, g.reshape(1, d).astype(F32))


def _mm_kernel(*refs, epi, n_extra, n_out, cast_w):
    x_ref, w_ref = refs[0], refs[1]
    extras = refs[2:2 + n_extra]
    outs = refs[2 + n_extra:2 + n_extra + n_out]
    if cast_w:
        wb_ref = refs[2 + n_extra + n_out]

        @pl.when(pl.program_id(1) == 0)
        def _():
            wb_ref[...] = w_ref[...].astype(BF16)
    else:
        wb_ref = w_ref

    acc = _dot(x_ref[...], wb_ref[...])
    vals = epi(acc, *(e[...] for e in extras))
    for o_ref, val in zip(outs, vals):
        o_ref[...] = val.astype(o_ref.dtype)


def _matmul(x, w, epi, extras=(), out_dtypes=(F32,), name="matmul", layer=None):
    m, k = x.shape
    n = w.shape[-1]
    cast_w = w.dtype != BF16
    w_item = jnp.dtype(w.dtype).itemsize
    tn = _pick_tile(n, max(LANE, min(MM_TN_MAX, MM_W_TILE_BYTES // (w_item * k) // LANE * LANE)), LANE)
    n_tile_io = len(out_dtypes) + sum(kind == "tile" for kind, _ in extras)

    def vmem_need(tm):
        blocks = tm * k * 2 + k * tn * w_item + n_tile_io * tm * tn * 4
        return 2 * blocks + (k * tn * 2 if cast_w else 0) + 3 * tm * tn * 4

    tm = _pick_tile(m, 16, 16)
    for cand in sorted((d for d in range(16, m + 1, 16) if m % d == 0), reverse=True):
        if cand * k * 2 <= MM_X_TILE_BYTES and vmem_need(cand) <= MM_VMEM_TARGET:
            tm = cand
            break
    w_spec = (pl.BlockSpec((k, tn), lambda j, i: (0, j)) if layer is None else
              pl.BlockSpec((None, k, tn), lambda j, i: (layer, 0, j)))
    in_specs = [pl.BlockSpec((tm, k), lambda j, i: (i, 0)), w_spec]
    args = [x, w]
    blk = _nbytes((tm, k), x.dtype) + _nbytes((k, tn), w.dtype)
    for kind, arr in extras:
        if kind == "row":
            in_specs.append(pl.BlockSpec((1, tn), lambda j, i: (0, j)))
            args.append(arr.reshape(1, n).astype(F32))
            blk += _nbytes((8, tn), F32)
        else:
            in_specs.append(pl.BlockSpec((tm, tn), lambda j, i: (i, j)))
            args.append(arr)
            blk += _nbytes((tm, tn), arr.dtype)
    out_specs = [pl.BlockSpec((tm, tn), lambda j, i: (i, j)) for _ in out_dtypes]
    out_shape = [jax.ShapeDtypeStruct((m, n), dt) for dt in out_dtypes]
    blk += sum(_nbytes((tm, tn), dt) for dt in out_dtypes)
    outs = pl.pallas_call(
        functools.partial(_mm_kernel, epi=epi, n_extra=len(extras), n_out=len(out_dtypes), cast_w=cast_w),
        grid=(n // tn, m // tm),
        in_specs=in_specs, out_specs=out_specs, out_shape=out_shape,
        scratch_shapes=[pltpu.VMEM((k, tn), BF16)] if cast_w else [],
        compiler_params=_cparams(("parallel", "arbitrary"), blk,
                                 (_nbytes((k, tn), BF16) if cast_w else 0) + 3 * _nbytes((tm, tn), F32)),
        name=name,
    )(*args)
    return outs


def _epi_id(acc):
    return (acc,)


def _epi_silu(acc):
    return (acc * jax.nn.sigmoid(acc),)


def _epi_sigmoid(acc):
    return (jax.nn.sigmoid(acc),)


def _epi_scale(acc, *, scale):
    return (acc * scale,)


def _epi_residual(acc, res):
    return (res + acc,)


def _epi_hgrn_forget(acc, lb):
    lbh = jnp.maximum(lb, 0.0)
    a = jnp.log(lbh)
    c = jnp.log1p(-lbh) + _log_sigmoid(acc)
    logf = jnp.maximum(a, c) + jnp.log1p(jnp.exp(-jnp.abs(a - c)))
    kk = (1.0 - lbh) * jax.nn.sigmoid(-acc)
    return logf, kk


def _epi_hgrn_forget_no_floor(acc):
    e = jnp.exp(-jnp.abs(acc))
    logf = -(jnp.maximum(-acc, 0.0) + jnp.log1p(e))
    kk = jnp.where(acc >= 0.0, e, 1.0) / (1.0 + e)
    return logf, kk


def _epi_bias_gelu(acc, b):
    x = acc + b
    return (0.5 * x * (1.0 + lax.erf(x * (2.0 ** -0.5))),)


def _epi_mlstm_gates(acc, b, *, heads):
    gates = GATE_CAP * jnp.tanh((acc + b) / GATE_CAP)
    col = lax.broadcasted_iota(jnp.int32, acc.shape, 1)
    return (jnp.where(col < heads, gates, _log_sigmoid(gates)),)


def _hgrn_levels(chunk, t_real):
    return tuple(m for m in (2 ** p for p in range(int(math.log2(chunk)) - 1, -1, -1)) if m < t_real)


def _hgrn_sum_matrix(chunk, levels, t_len):
    t = np.arange(chunk)[:, None]
    r = np.arange(chunk)[None, :]
    same = (t // t_len) == (r // t_len)
    mats = [(r <= t) & same, (r > t) & same]
    for m in levels:
        mid = (t // (2 * m)) * (2 * m) + m - 1
        second = (t % (2 * m)) >= m
        mats.append(np.where(second, (r > mid) & (r <= t), (r > t) & (r <= mid)))
    return np.concatenate(mats, axis=0).astype(np.float32)


def _hgrn_kernel(*refs, heads, chunk, levels, dk, group, n_aliased, slot):
    q_ref, lf_ref, k_ref, v_ref, g_ref, onorm_ref, d_ref, s0_ref = refs[:8]
    y_ref, s_ref, o_scr = refs[8 + n_aliased:]
    c = pl.program_id(1)
    t_len = chunk // group
    dv = s_ref.shape[-1]

    @pl.when(c == 0)
    def _():
        for other in range(s_ref.shape[0]):
            if other != slot:
                s_ref[other] = jnp.zeros(s_ref.shape[1:], F32)
        s_ref[slot] = s0_ref[0]

    row = lax.broadcasted_iota(jnp.int32, (chunk, 1), 0)
    in_seq = [None] if group == 1 else [(row >> int(math.log2(t_len))) == s for s in range(group)]
    seq_ones = [jnp.ones((chunk, dv), BF16) if mask is None else
                jnp.where(mask, jnp.ones((chunk, dv), F32), 0.0).astype(BF16) for mask in in_seq]
    r2 = lax.broadcasted_iota(jnp.int32, (chunk, chunk), 0)
    c2 = lax.broadcasted_iota(jnp.int32, (chunk, chunk), 1)

    lf_hi, lf_lo = _split2(lf_ref[...])
    xs = _dot(d_ref[...], jnp.concatenate([lf_hi, lf_lo], axis=0))
    q = q_ref[...].astype(F32)
    k = k_ref[...].astype(F32)
    vb = v_ref[...].astype(BF16)
    ex_b = jnp.exp(xs[0:chunk])
    q_in = q * ex_b
    k_out = k * jnp.exp(xs[chunk:2 * chunk])
    q_in = [(q_in if mask is None else jnp.where(mask, q_in, 0.0)).astype(BF16) for mask in in_seq]
    k_out = [(k_out if mask is None else jnp.where(mask, k_out, 0.0)).astype(BF16) for mask in in_seq]
    qs = [q_ref[...].astype(BF16)]
    ks = [k_ref[...].astype(BF16)]
    masks = [r2 == c2]
    for li, m in enumerate(levels):
        ex = jnp.exp(xs[(2 + li) * chunk:(3 + li) * chunk])
        second = (row & m) != 0
        qs.append(jnp.where(second, q * ex, 0.0).astype(BF16))
        ks.append(jnp.where(second, 0.0, k * ex).astype(BF16))
        shift = int(math.log2(2 * m))
        masks.append(None if 2 * m == chunk else (r2 >> shift) == (c2 >> shift))

    scores = []
    for h in range(heads):
        cols = slice(h * dk, (h + 1) * dk)
        a = None
        for qs_l, ks_l, mask in zip(qs, ks, masks):
            al = _dot_nt(qs_l[:, cols], ks_l[:, cols])
            if mask is not None:
                al = jnp.where(mask, al, 0.0)
            a = al if a is None else a + al
        scores.append(a.astype(BF16))

    for h in range(heads):
        cols = slice(h * dk, (h + 1) * dk)
        o = _dot(scores[h], vb[:, cols])
        for s in range(group):
            st = s_ref[slot, s, h]
            o = o + _dot(q_in[s][:, cols], st.astype(BF16))
            decay = jnp.exp(_dot_tn(lf_hi[:, cols], seq_ones[s]) + _dot_tn(lf_lo[:, cols], seq_ones[s]))
            s_ref[slot, s, h] = st * decay + _dot_tn(k_out[s][:, cols], vb[:, cols])
        o_scr[:, cols] = o

    o = o_scr[...]
    y = o * lax.rsqrt(jnp.mean(o * o, axis=-1, keepdims=True) + EPS) * onorm_ref[...]
    y_ref[...] = (y * g_ref[...]).astype(y_ref.dtype)


def _hgrn_scan(q, lf, k, v, g, onorm, s0_all, layer, s_out_prev, y_prev, *, row_start, batch, seq, chunk, group):
    d = q.shape[1]
    heads, dk, dv = s0_all.shape[2:]
    t_chunk = chunk // group
    n_chunks = seq // t_chunk
    assert group == 1 or n_chunks == 1
    assert row_start % chunk == 0 and batch % group == 0
    levels = _hgrn_levels(chunk, t_chunk)
    dmat = _hgrn_sum_matrix(chunk, levels, t_chunk)
    dmat = jnp.asarray(np.concatenate([dmat, dmat], axis=1), dtype=BF16)
    first_blk = row_start // chunk
    tok = pl.BlockSpec((chunk, d), lambda b, c: (first_blk + b * n_chunks + c, 0))
    in_layer = min(layer, s0_all.shape[0] - 1)
    st_in = pl.BlockSpec((1, group, heads, dk, dv), lambda b, c: (in_layer, b, 0, 0, 0))
    n_layers = s_out_prev if isinstance(s_out_prev, int) else s_out_prev.shape[0]
    fresh_state = isinstance(s_out_prev, int)
    slots = n_layers if fresh_state else 1
    st_out = pl.BlockSpec((slots, group, heads, dk, dv), lambda b, c: (0 if fresh_state else layer, b, 0, 0, 0))
    in_specs = [tok, tok, tok, tok, tok,
                pl.BlockSpec((1, d), lambda b, c: (0, 0)),
                pl.BlockSpec(dmat.shape, lambda b, c: (0, 0)),
                st_in]
    args = [q, lf, k, v, g, onorm.reshape(1, d).astype(F32), dmat, s0_all]
    aliases = {}
    for out_idx, prev in ((0, y_prev), (1, s_out_prev)):
        if not isinstance(prev, int):
            aliases[len(args)] = out_idx
            in_specs.append(pl.BlockSpec(memory_space=pl.ANY))
            args.append(prev)
    y_rows = y_prev if isinstance(y_prev, int) else y_prev.shape[0]
    blk = (sum(_nbytes((chunk, d), a.dtype) for a in (q, lf, k, v, g)) + _nbytes((chunk, d), BF16) + (1 + slots) * _nbytes((group, heads, dk, dv), F32)
           + _nbytes(dmat.shape, BF16))
    n_exp = 2 + len(levels)
    y, s = pl.pallas_call(
        functools.partial(_hgrn_kernel, heads=heads, chunk=chunk, levels=levels, dk=dk, group=group,
                          n_aliased=len(aliases), slot=layer if fresh_state else 0),
        grid=(batch // group, n_chunks),
        in_specs=in_specs,
        out_specs=[tok, st_out],
        out_shape=[jax.ShapeDtypeStruct((y_rows, d), BF16),
                   jax.ShapeDtypeStruct((n_layers, batch, heads, dk, dv), F32)],
        scratch_shapes=[pltpu.VMEM((chunk, d), F32)],
        input_output_aliases=aliases,
        compiler_params=_cparams(("parallel", "arbitrary"), blk, (3 * n_exp + 8) * _nbytes((chunk, d), F32)),
        name="hgrn_scan",
    )(*args)
    return y, s


def _mlstm_kernel(*refs, heads, chunk, dqk, dv, group, n_aliased):
    q_ref, k_ref, v_ref, og_ref, gt_ref, hn_ref, tri_ref, sel_ref, c0_ref, n0_ref, m0_ref = refs[:11]
    y_ref, c_ref, n_ref, m_ref = refs[11 + n_aliased:]
    c = pl.program_id(1)
    t_len = chunk // group

    @pl.when(c == 0)
    def _():
        c_ref[...] = c0_ref[...]
        n_ref[...] = n0_ref[...]
        m_ref[...] = m0_ref[...]

    gates = gt_ref[...]
    tri = tri_ref[...]
    sel = sel_ref[...]
    g3 = _split3(gates)
    cum = sum(_dot(tri, p) for p in g3)
    gates_t = sum(_dot_nt(sel, p) for p in g3)
    cum_t = sum(_dot_nt(sel, p) for p in _split3(cum))
    row = lax.broadcasted_iota(jnp.int32, (chunk, 1), 0)
    r2 = lax.broadcasted_iota(jnp.int32, (chunk, chunk), 0)
    c2 = lax.broadcasted_iota(jnp.int32, (chunk, chunk), 1)
    causal = c2 <= r2
    in_seq = [None]
    if group > 1:
        shift = int(math.log2(t_len))
        causal = causal & ((r2 >> shift) == (c2 >> shift))
        in_seq = [(row >> shift) == s for s in range(group)]

    def per_row(vals):
        if group == 1:
            return vals[0]
        out = jnp.where(in_seq[0], vals[0], 0.0)
        for s in range(1, group):
            out = jnp.where(in_seq[s], vals[s], out)
        return out

    lane = lax.broadcasted_iota(jnp.int32, (1, LANE), 1)
    m_rows = [m_ref[s] for s in range(group)]
    m_new = list(m_rows)
    for h in range(heads):
        qc = slice(h * dqk, (h + 1) * dqk)
        vc = slice(h * dv, (h + 1) * dv)
        qb = q_ref[:, qc].astype(BF16)
        kb = k_ref[:, qc].astype(BF16)
        qh = qb.astype(F32)
        kh = kb.astype(F32)
        vb = v_ref[:, vc].astype(BF16)
        b_c = cum[:, heads + h:heads + h + 1]
        ig_c = gates[:, h:h + 1]
        b_r = cum_t[heads + h:heads + h + 1, :]
        ig_r = gates_t[h:h + 1, :]
        m_h = [m_rows[s][:, h:h + 1] for s in range(group)]
        dlog = jnp.where(causal, b_c - b_r + ig_r, -jnp.inf)
        inter = b_c + per_row(m_h)
        mt = jnp.maximum(inter, jnp.max(dlog, axis=-1, keepdims=True))
        wts = jnp.exp(dlog - mt) * _dot_nt(qb, kb)
        sc = jnp.exp(inter - mt)
        c_h = [c_ref[s, h] for s in range(group)]
        n_h = [n_ref[s, h:h + 1, :] for s in range(group)]
        if group == 1:
            q_c = _dot(qb, c_h[0].astype(BF16))
        else:
            q_c = sum(_dot(jnp.where(in_seq[s], qh.astype(F32), 0.0).astype(BF16), c_h[s].astype(BF16))
                      for s in range(group))
        num = sc * q_c + _dot(wts.astype(BF16), vb)
        den = sc * jnp.sum(qh * per_row(n_h), axis=-1, keepdims=True) + jnp.sum(wts, axis=-1, keepdims=True)
        out = num / jnp.maximum(jnp.abs(den), jnp.exp(-mt))
        for s in range(group):
            last = (s + 1) * t_len - 1
            m_last = mt[last:last + 1, :]
            b_last = b_c[last:last + 1, :]
            sc_state = jnp.exp(b_last + m_h[s] - m_last)
            w_k = jnp.exp(b_last - b_c + ig_c - m_last)
            if group > 1:
                w_k = jnp.where(in_seq[s], w_k, 0.0)
            kw = w_k * kh
            c_ref[s, h] = sc_state * c_h[s] + _dot_tn(kw.astype(BF16), vb)
            n_ref[s, h:h + 1, :] = sc_state * n_h[s] + jnp.sum(kw, axis=0, keepdims=True)
            m_new[s] = jnp.where(lane == h, m_last, m_new[s])
        y = out * lax.rsqrt(jnp.mean(out * out, axis=-1, keepdims=True) + EPS) * hn_ref[:, vc]
        y_ref[:, vc] = (y * og_ref[:, vc]).astype(y_ref.dtype)
    for s in range(group):
        m_ref[s] = m_new[s]


def _mlstm_scan(q, k, v, og, gates, hnorm, c0, n0, m0, y_prev, *, row_start, batch, seq, chunk, group):
    heads, dqk, dv = c0.shape[1], c0.shape[2], c0.shape[3]
    dq_all, dv_all = q.shape[1], v.shape[1]
    t_chunk = chunk // group
    n_chunks = seq // t_chunk
    assert 2 * heads <= 16 and (group == 1 or n_chunks == 1)
    assert row_start % chunk == 0 and batch % group == 0
    idx = np.arange(chunk)
    same_seq = (idx[:, None] // t_chunk) == (idx[None, :] // t_chunk)
    tri = jnp.asarray(np.tril(np.ones((chunk, chunk), np.float32)) * same_seq, dtype=BF16)
    sel = jnp.asarray(np.eye(16, LANE, dtype=np.float32), dtype=BF16)
    m0p = jnp.pad(m0, ((0, 0), (0, LANE - heads))).reshape(batch, 1, LANE)
    first_blk = row_start // chunk
    tq = pl.BlockSpec((chunk, dq_all), lambda b, c: (first_blk + b * n_chunks + c, 0))
    tv = pl.BlockSpec((chunk, dv_all), lambda b, c: (first_blk + b * n_chunks + c, 0))
    tg = pl.BlockSpec((chunk, LANE), lambda b, c: (first_blk + b * n_chunks + c, 0))
    sc_ = pl.BlockSpec((group, heads, dqk, dv), lambda b, c: (b, 0, 0, 0))
    sn_ = pl.BlockSpec((group, heads, dqk), lambda b, c: (b, 0, 0))
    sm_ = pl.BlockSpec((group, 1, LANE), lambda b, c: (b, 0, 0))
    in_specs = [tq, tq, tv, tv, tg,
                pl.BlockSpec((1, dv_all), lambda b, c: (0, 0)),
                pl.BlockSpec(tri.shape, lambda b, c: (0, 0)),
                pl.BlockSpec(sel.shape, lambda b, c: (0, 0)),
                sc_, sn_, sm_]
    args = [q, k, v, og, gates, hnorm.reshape(1, dv_all).astype(F32), tri, sel, c0, n0, m0p]
    aliases = {}
    if not isinstance(y_prev, int):
        aliases[len(args)] = 0
        in_specs.append(pl.BlockSpec(memory_space=pl.ANY))
        args.append(y_prev)
    y_rows = y_prev if isinstance(y_prev, int) else y_prev.shape[0]
    blk = (_nbytes((chunk, dq_all), q.dtype) + _nbytes((chunk, dq_all), k.dtype) + _nbytes((chunk, dv_all), v.dtype)
           + _nbytes((chunk, dv_all), og.dtype) + _nbytes((chunk, LANE), F32)
           + _nbytes((chunk, dv_all), BF16) + 2 * _nbytes((group,) + c0.shape[1:], F32))
    y, c_out, n_out, m_out = pl.pallas_call(
        functools.partial(_mlstm_kernel, heads=heads, chunk=chunk, dqk=dqk, dv=dv, group=group,
                          n_aliased=len(aliases)),
        grid=(batch // group, n_chunks),
        in_specs=in_specs,
        out_specs=[tv, sc_, sn_, sm_],
        out_shape=[jax.ShapeDtypeStruct((y_rows, dv_all), BF16),
                   jax.ShapeDtypeStruct(c0.shape, F32),
                   jax.ShapeDtypeStruct(n0.shape, F32),
                   jax.ShapeDtypeStruct((batch, 1, LANE), F32)],
        input_output_aliases=aliases,
        compiler_params=_cparams(("parallel", "arbitrary"), blk, 8 * _nbytes((chunk, dv_all), F32)),
        name="mlstm_scan",
    )(*args)
    return y, c_out, n_out, m_out[:, 0, :heads]


def _gmlp_kernel(u_ref, v_ref, vg_ref, vb_ref, w_ref, bs_ref, o_ref, vn_ref, *, groups, gd):
    v = v_ref[...]
    mu = jnp.mean(v, axis=-1, keepdims=True)
    xc = v - mu
    vn = xc * lax.rsqrt(jnp.mean(xc * xc, axis=-1, keepdims=True) + EPS) * vg_ref[...] + vb_ref[...]
    vn_ref[...] = vn
    n = v.shape[0]
    causal = (lax.broadcasted_iota(jnp.int32, (n, n), 1) <= lax.broadcasted_iota(jnp.int32, (n, n), 0))
    bs = bs_ref[0]
    for g in range(groups):
        cols = slice(g * gd, (g + 1) * gd)
        wg = jnp.where(causal, w_ref[0, g], 0.0).astype(BF16)
        mix = _dot(wg, vn[:, cols].astype(BF16)) + bs[:, g:g + 1]
        o_ref[:, cols] = (u_ref[:, cols] * mix).astype(o_ref.dtype)


def _gmlp_gate(z, vg, vb, w_stack, bs_stack, *, n_prompt_chunks):
    m, d2 = z.shape
    d = d2 // 2
    groups, chunk = w_stack.shape[1], w_stack.shape[2]
    gd = d // groups
    assert m % chunk == 0

    def which(i):
        return jnp.minimum(i // n_prompt_chunks, 1)

    blk = (2 * _nbytes((chunk, d), F32) + _nbytes((chunk, d), BF16) + _nbytes((chunk, d), F32)
           + _nbytes((groups, chunk, chunk), F32))
    o, vn = pl.pallas_call(
        functools.partial(_gmlp_kernel, groups=groups, gd=gd),
        grid=(m // chunk,),
        in_specs=[pl.BlockSpec((chunk, d), lambda i: (i, 0)),
                  pl.BlockSpec((chunk, d), lambda i: (i, 1)),
                  pl.BlockSpec((1, d), lambda i: (0, 0)),
                  pl.BlockSpec((1, d), lambda i: (0, 0)),
                  pl.BlockSpec((1, groups, chunk, chunk), lambda i: (which(i), 0, 0, 0)),
                  pl.BlockSpec((1, chunk, groups), lambda i: (which(i), 0, 0))],
        out_specs=[pl.BlockSpec((chunk, d), lambda i: (i, 0)),
                   pl.BlockSpec((chunk, d), lambda i: (jnp.maximum(i - n_prompt_chunks, 0), 0))],
        out_shape=[jax.ShapeDtypeStruct((m, d), BF16),
                   jax.ShapeDtypeStruct((m - n_prompt_chunks * chunk, d), F32)],
        compiler_params=_cparams(("arbitrary",), blk, 4 * _nbytes((chunk, d), F32)),
        name="gmlp_gate",
    )(z, z, vg.reshape(1, d).astype(F32), vb.reshape(1, d).astype(F32), w_stack, bs_stack)
    return o, vn


def _causal_conv(tap, cw_ref, cb_ref, conv_w):
    y = cb_ref[...] + cw_ref[conv_w - 1:conv_w, :] * tap(0)
    for back in range(1, conv_w):
        y = y + cw_ref[conv_w - 1 - back:conv_w - back, :] * tap(back)
    return y


def _ffn_up_prompt_kernel(h_ref, wa_ref, wg_ref, cwa_ref, cwg_ref, cba_ref, cbg_ref,
                          _, act_ref, tail_a_ref, tail_g_ref, wb_scr, up_scr, *, tiles_per_seq, conv_w, row_chunk):
    i = pl.program_id(1)
    tm = h_ref.shape[0]
    n_chunks = tm // row_chunk

    @pl.when(i == 0)
    def _():
        wb_scr[0] = wa_ref[...].astype(BF16)
        wb_scr[1] = wg_ref[...].astype(BF16)

    @pl.when(lax.rem(i, tiles_per_seq) == 0)
    def _():
        up_scr[:, 0:8, :] = jnp.zeros((2, 8, up_scr.shape[2]), F32)

    def multiply(c):
        hb = h_ref[c * row_chunk:(c + 1) * row_chunk, :]
        for idx in range(2):
            up_scr[idx, 8 + c * row_chunk:8 + (c + 1) * row_chunk, :] = _dot(hb, wb_scr[idx])

    def finish(c):
        lo = 8 + c * row_chunk
        ya = _causal_conv(lambda back: up_scr[0, lo - back:lo - back + row_chunk, :], cwa_ref, cba_ref, conv_w)
        yg = _causal_conv(lambda back: up_scr[1, lo - back:lo - back + row_chunk, :], cwg_ref, cbg_ref, conv_w)
        act_ref[c * row_chunk:(c + 1) * row_chunk, :] = (ya * (yg * jax.nn.sigmoid(yg))).astype(act_ref.dtype)

    multiply(0)
    for c in range(1, n_chunks):
        multiply(c)
        finish(c - 1)
    finish(n_chunks - 1)
    for idx, tail_ref in ((0, tail_a_ref), (1, tail_g_ref)):
        last = up_scr[idx, tm:tm + 8, :]
        tail_ref[0] = last
        up_scr[idx, 0:8, :] = last


def _ffn_up_prompt(h, w_up, layer, cw, cb, act_prev, *, rows, seq):
    _, k, f2 = w_up.shape
    f = f2 // 2
    conv_w = cw.shape[0]
    tm = _pick_tile(seq, 1024, 16)
    tf = _pick_tile(f, FFN_TF, LANE)
    nf = f // tf
    n_tiles = rows // tm
    row_chunk = _pick_tile(tm, FFN_ROW_CHUNK, 16)
    blk = (_nbytes((tm, k), BF16) + 2 * _nbytes((k, tf), F32) + _nbytes((tm, tf), BF16)
           + 2 * _nbytes((8, tf), F32) + 6 * _nbytes((8, tf), F32))
    tail = jax.ShapeDtypeStruct((n_tiles, 8, f), F32)
    act, tail_a, tail_g = pl.pallas_call(
        functools.partial(_ffn_up_prompt_kernel, tiles_per_seq=seq // tm, conv_w=conv_w, row_chunk=row_chunk),
        grid=(nf, n_tiles),
        in_specs=[pl.BlockSpec((tm, k), lambda j, i: (i, 0)),
                  pl.BlockSpec((None, k, tf), lambda j, i: (layer, 0, j)),
                  pl.BlockSpec((None, k, tf), lambda j, i: (layer, 0, j + nf)),
                  pl.BlockSpec((conv_w, tf), lambda j, i: (0, j)),
                  pl.BlockSpec((conv_w, tf), lambda j, i: (0, j + nf)),
                  pl.BlockSpec((1, tf), lambda j, i: (0, j)),
                  pl.BlockSpec((1, tf), lambda j, i: (0, j + nf)),
                  pl.BlockSpec(memory_space=pl.ANY)],
        out_specs=[pl.BlockSpec((tm, tf), lambda j, i: (i, j)),
                   pl.BlockSpec((1, 8, tf), lambda j, i: (i, 0, j)),
                   pl.BlockSpec((1, 8, tf), lambda j, i: (i, 0, j))],
        out_shape=[jax.ShapeDtypeStruct(act_prev.shape, BF16), tail, tail],
        input_output_aliases={7: 0},
        scratch_shapes=[pltpu.VMEM((2, k, tf), BF16), pltpu.VMEM((2, 8 + tm, tf), F32)],
        compiler_params=_cparams(("parallel", "arbitrary"), blk,
                                 2 * _nbytes((k, tf), BF16) + 2 * _nbytes((8 + tm, tf), F32)
                                 + 6 * _nbytes((row_chunk, tf), F32)),
        name="ffn_up_conv_prompt",
    )(h, w_up, w_up, cw, cw, cb.reshape(1, f2), cb.reshape(1, f2), act_prev)
    tiles_per_seq = seq // tm
    keep = slice(8 - (conv_w - 1), 8)
    tails = jnp.concatenate([tail_a[tiles_per_seq - 1::tiles_per_seq, keep], tail_g[tiles_per_seq - 1::tiles_per_seq, keep]],
                            axis=-1)
    return act, tails


def _ffn_up_sample_kernel(*refs, steps, conv_w):
    h_ref, wa_ref, wg_ref, cwa_ref, cwg_ref, cba_ref, cbg_ref = refs[:7]
    bufs_a = refs[7:7 + conv_w - 1]
    bufs_g = refs[7 + conv_w - 1:7 + 2 * (conv_w - 1)]
    act_ref, upa_ref, upg_ref = refs[7 + 2 * (conv_w - 1):]
    bs = h_ref.shape[0] // steps
    hb = h_ref[...]

    def branch(w_ref, bufs, cw_ref, cb_ref, up_ref):
        up = _dot(hb, w_ref[...].astype(BF16))
        up_ref[...] = up

        def at(t):
            return up[t * bs:(t + 1) * bs, :] if t >= 0 else bufs[conv_w - 1 + t][...]

        return [_causal_conv(lambda back, t=t: at(t - back), cw_ref, cb_ref, conv_w) for t in range(steps)]

    ya = branch(wa_ref, bufs_a, cwa_ref, cba_ref, upa_ref)
    yg = branch(wg_ref, bufs_g, cwg_ref, cbg_ref, upg_ref)
    for t in range(steps):
        act_ref[t * bs:(t + 1) * bs, :] = (ya[t] * (yg[t] * jax.nn.sigmoid(yg[t]))).astype(act_ref.dtype)


def _ffn_up_sample(h_t, w_up, layer, cw, cb, conv_state, *, steps):
    _, k, f2 = w_up.shape
    f = f2 // 2
    conv_w = cw.shape[0]
    ms = h_t.shape[0]
    bs = ms // steps
    assert steps >= conv_w - 1 and bs % 16 == 0 and conv_state.shape[1] == bs
    tf = _pick_tile(f, FFN_TF, LANE)
    nf = f // tf
    state_specs = [pl.BlockSpec((None, bs, tf), lambda j, r=r, off=off: (r, 0, off + j))
                   for off in (0, nf) for r in range(conv_w - 1)]
    blk = (_nbytes((ms, k), BF16) + 2 * _nbytes((k, tf), F32) + 2 * _nbytes((conv_w - 1, bs, tf), F32)
           + _nbytes((ms, tf), BF16) + 2 * _nbytes((ms, tf), F32) + 6 * _nbytes((8, tf), F32))
    up_shape = jax.ShapeDtypeStruct((ms, f), F32)
    act, up_a, up_g = pl.pallas_call(
        functools.partial(_ffn_up_sample_kernel, steps=steps, conv_w=conv_w),
        grid=(nf,),
        in_specs=[pl.BlockSpec((ms, k), lambda j: (0, 0)),
                  pl.BlockSpec((None, k, tf), lambda j: (layer, 0, j)),
                  pl.BlockSpec((None, k, tf), lambda j: (layer, 0, j + nf)),
                  pl.BlockSpec((conv_w, tf), lambda j: (0, j)),
                  pl.BlockSpec((conv_w, tf), lambda j: (0, j + nf)),
                  pl.BlockSpec((1, tf), lambda j: (0, j)),
                  pl.BlockSpec((1, tf), lambda j: (0, j + nf))] + state_specs,
        out_specs=[pl.BlockSpec((ms, tf), lambda j: (0, j))] * 3,
        out_shape=[jax.ShapeDtypeStruct((ms, f), BF16), up_shape, up_shape],
        compiler_params=_cparams(("parallel",), blk, 2 * _nbytes((k, tf), BF16) + 8 * _nbytes((ms, tf), F32)),
        name="ffn_up_conv_sample",
    )(h_t, w_up, w_up, cw, cw, cb.reshape(1, f2), cb.reshape(1, f2), *([conv_state] * (2 * (conv_w - 1))))
    keep = slice((steps - (conv_w - 1)) * bs, ms)
    up_tail = jnp.concatenate([up_a[keep], up_g[keep]], axis=-1).reshape(conv_w - 1, bs, f2).transpose(1, 0, 2)
    return act, up_tail


def _forward(x_prompt, x_sample, state_hgrn_S, state_mlstm_C, state_mlstm_n, state_mlstm_m, state_ffn_conv,
             norm_mix, norm_ffn, norm_final,
             hgrn_w_q, hgrn_w_f, hgrn_w_i, hgrn_w_g, hgrn_lb, hgrn_onorm, hgrn_w_o,
             mlstm_w_q, mlstm_w_k, mlstm_w_v, mlstm_w_og, mlstm_w_if, mlstm_b_if, mlstm_hnorm, mlstm_w_out,
             gmlp_w_in, gmlp_b_in, gmlp_vnorm_g, gmlp_vnorm_b, gmlp_w_s, gmlp_b_s, gmlp_w_out,
             ffn_w_up, ffn_conv_w, ffn_conv_b, ffn_w_down):
    bp, tp, d = x_prompt.shape
    bs, ts, _ = x_sample.shape
    mp, ms = bp * tp, bs * ts
    m = mp + ms
    depth = norm_mix.shape[0]
    hg_heads, hg_dk, hg_dv = state_hgrn_S.shape[2:]
    ml_heads, ml_dqk, ml_dv = state_mlstm_C.shape[2:]

    x = jnp.concatenate([x_prompt.reshape(mp, d), x_sample.reshape(ms, d)], axis=0)

    lb = jax.nn.softmax(hgrn_lb.astype(F32), axis=0)
    lbs = jnp.cumsum(lb, axis=0) - lb[0]

    w_down = ffn_w_down.astype(BF16)
    act = jnp.zeros((m, ffn_w_down.shape[1]), BF16)
    out_c, out_n, out_m, out_v, out_conv = [], [], [], [], []
    s_p_all = s_s_all = state_hgrn_S.shape[0]
    for i in range(depth):
        j = i // N_MIXERS
        kind = i % N_MIXERS
        h = _rmsnorm(x, norm_mix[i], BF16)
        if kind == 0:
            (q,) = _matmul(h, hgrn_w_q, _epi_silu, out_dtypes=(BF16,), name="hgrn_q", layer=j)
            if j == 0:
                lf, k = _matmul(h, hgrn_w_f, _epi_hgrn_forget_no_floor,
                                out_dtypes=(F32, BF16), name="hgrn_f", layer=j)
            else:
                lf, k = _matmul(h, hgrn_w_f, _epi_hgrn_forget, extras=[("row", lbs[j])],
                                out_dtypes=(F32, BF16), name="hgrn_f", layer=j)
            (v,) = _matmul(h, hgrn_w_i, _epi_id, out_dtypes=(BF16,), name="hgrn_i", layer=j)
            (g,) = _matmul(h, hgrn_w_g, _epi_sigmoid, out_dtypes=(BF16,), name="hgrn_g", layer=j)
            y, s_p_all = _hgrn_scan(q, lf, k, v, g, hgrn_onorm[j],
                                    jnp.zeros((1, bp, hg_heads, hg_dk, hg_dv), F32), j, s_p_all, h,
                                    row_start=0, batch=bp, seq=tp, chunk=min(HGRN_CHUNK, tp), group=1)
            group = max(1, SAMPLE_ROWS // ts)
            y, s_s_all = _hgrn_scan(q, lf, k, v, g, hgrn_onorm[j], state_hgrn_S.astype(F32), j, s_s_all, y,
                                    row_start=mp, batch=bs, seq=ts, chunk=group * ts, group=group)
            (x,) = _matmul(y, hgrn_w_o, _epi_residual, extras=[("tile", x)], name="hgrn_o", layer=j)
        elif kind == 1:
            (q,) = _matmul(h, mlstm_w_q, _epi_id, out_dtypes=(BF16,), name="mlstm_q", layer=j)
            (k,) = _matmul(h, mlstm_w_k, functools.partial(_epi_scale, scale=ml_dqk ** -0.5),
                           out_dtypes=(BF16,), name="mlstm_k", layer=j)
            (v,) = _matmul(h, mlstm_w_v, _epi_id, out_dtypes=(BF16,), name="mlstm_v", layer=j)
            (og,) = _matmul(h, mlstm_w_og, _epi_sigmoid, out_dtypes=(BF16,), name="mlstm_og", layer=j)
            w_if = jnp.pad(mlstm_w_if[j], ((0, 0), (0, LANE - 2 * ml_heads)))
            b_if = jnp.pad(mlstm_b_if[j], (0, LANE - 2 * ml_heads))
            (gates,) = _matmul(h, w_if, functools.partial(_epi_mlstm_gates, heads=ml_heads),
                               extras=[("row", b_if)], name="mlstm_if")
            y, c_p, n_p, m_p = _mlstm_scan(
                q, k, v, og, gates, mlstm_hnorm[j],
                jnp.zeros((bp, ml_heads, ml_dqk, ml_dv), F32), jnp.zeros((bp, ml_heads, ml_dqk), F32),
                jnp.zeros((bp, ml_heads), F32), h,
                row_start=0, batch=bp, seq=tp, chunk=min(MLSTM_CHUNK, tp), group=1)
            group = max(1, SAMPLE_ROWS // ts)
            y, c_s, n_s, m_s = _mlstm_scan(
                q, k, v, og, gates, mlstm_hnorm[j],
                state_mlstm_C[j].astype(F32), state_mlstm_n[j].astype(F32), state_mlstm_m[j].astype(F32), y,
                row_start=mp, batch=bs, seq=ts, chunk=group * ts, group=group)
            out_c.append((c_p, c_s))
            out_n.append((n_p, n_s))
            out_m.append((m_p, m_s))
            (x,) = _matmul(y, mlstm_w_out, _epi_residual, extras=[("tile", x)], name="mlstm_out", layer=j)
        else:
            (z,) = _matmul(h, gmlp_w_in, _epi_bias_gelu, extras=[("row", gmlp_b_in[j])], name="gmlp_in", layer=j)
            groups = gmlp_w_s.shape[1]
            lp = min(GMLP_CHUNK, tp)
            ls = min(GMLP_CHUNK, ts)
            reps = GMLP_CHUNK // ls
            w_s_blk = jnp.einsum("ab,gts->gatbs", jnp.eye(reps, dtype=F32), gmlp_w_s[j][:, :ls, :ls])
            w_s_blk = w_s_blk.reshape(groups, GMLP_CHUNK, GMLP_CHUNK)
            w_p = gmlp_w_s[j][:, :lp, :lp]
            w_stack = jnp.stack([w_p, w_s_blk])
            bs_stack = jnp.stack([gmlp_b_s[j][:, :lp].T, jnp.tile(gmlp_b_s[j][:, :ls].T, (reps, 1))])
            y, vn = _gmlp_gate(z, gmlp_vnorm_g[j], gmlp_vnorm_b[j], w_stack, bs_stack,
                               n_prompt_chunks=mp // GMLP_CHUNK)
            out_v.append(vn.reshape(bs, ts, d))
            (x,) = _matmul(y, gmlp_w_out, _epi_residual, extras=[("tile", x)], name="gmlp_out", layer=j)

        h = _rmsnorm(x, norm_ffn[i], BF16)
        act, tail_p = _ffn_up_prompt(h, ffn_w_up, i, ffn_conv_w[i], ffn_conv_b[i], act, rows=mp, seq=tp)
        h_t = h[mp:].reshape(bs, ts, d).transpose(1, 0, 2).reshape(ms, d)
        act_t, tail_s = _ffn_up_sample(h_t, ffn_w_up, i, ffn_conv_w[i], ffn_conv_b[i],
                                       state_ffn_conv[i].astype(F32).transpose(1, 0, 2), steps=ts)
        act_s = act_t.reshape(ts, bs, -1).transpose(1, 0, 2).reshape(ms, -1)
        act = lax.dynamic_update_slice(act, act_s, (mp, 0))
        out_conv.append((tail_p, tail_s))
        (x,) = _matmul(act, w_down, _epi_residual, extras=[("tile", x)], name="ffn_down", layer=i)

    y_p = _rmsnorm(x, norm_final, F32, row_start=0, n_rows=mp).reshape(bp, tp, d)
    y_s = _rmsnorm(x, norm_final, F32, row_start=mp, n_rows=ms).reshape(bs, ts, d)
    stack = lambda pairs, idx: jnp.stack([p[idx] for p in pairs])
    return (y_p, y_s,
            s_p_all, s_s_all, stack(out_c, 0), stack(out_c, 1),
            stack(out_n, 0), stack(out_n, 1), stack(out_m, 0), stack(out_m, 1),
            jnp.stack(out_v), stack(out_conv, 0), stack(out_conv, 1))


_forward_jit = jax.jit(_forward)


def kernel(x_prompt, x_sample, state_hgrn_S, state_mlstm_C, state_mlstm_n, state_mlstm_m, state_ffn_conv, norm_mix, norm_ffn, norm_final, hgrn_w_q, hgrn_w_f, hgrn_w_i, hgrn_w_g, hgrn_lb, hgrn_onorm, hgrn_w_o, mlstm_w_q, mlstm_w_k, mlstm_w_v, mlstm_w_og, mlstm_w_if, mlstm_b_if, mlstm_hnorm, mlstm_w_out, gmlp_w_in, gmlp_b_in, gmlp_vnorm_g, gmlp_vnorm_b, gmlp_w_s, gmlp_b_s, gmlp_w_out, ffn_w_up, ffn_conv_w, ffn_conv_b, ffn_w_down):
    return _forward_jit(x_prompt, x_sample, state_hgrn_S, state_mlstm_C, state_mlstm_n, state_mlstm_m, state_ffn_conv,
                        norm_mix, norm_ffn, norm_final,
                        hgrn_w_q, hgrn_w_f, hgrn_w_i, hgrn_w_g, hgrn_lb, hgrn_onorm, hgrn_w_o,
                        mlstm_w_q, mlstm_w_k, mlstm_w_v, mlstm_w_og, mlstm_w_if, mlstm_b_if, mlstm_hnorm, mlstm_w_out,
                        gmlp_w_in, gmlp_b_in, gmlp_vnorm_g, gmlp_vnorm_b, gmlp_w_s, gmlp_b_s, gmlp_w_out,
                        ffn_w_up, ffn_conv_w, ffn_conv_b, ffn_w_down)
```

```python
import functools
import math

import numpy as np
import jax
import jax.numpy as jnp
from jax import lax
from jax.experimental import pallas as pl
from jax.experimental.pallas import tpu as pltpu

F32 = jnp.float32
BF16 = jnp.bfloat16

EPS = 1e-6
GATE_CAP = 15.0
N_MIXERS = 3
HGRN_CHUNK = 128
MLSTM_CHUNK = 256
GMLP_CHUNK = 128
FFN_ROW_CHUNK = 256
FFN_TF = 512
SAMPLE_ROWS = 32

LANE = 128
VMEM_PHYSICAL_V7X = 64 * 1024 * 1024
VMEM_BUDGET = VMEM_PHYSICAL_V7X - 8 * 1024 * 1024

MM_TN_MAX = 1024
MM_W_TILE_BYTES = 12 << 20
MM_X_TILE_BYTES = 13 << 20
MM_VMEM_TARGET = 52 << 20


def _nbytes(shape, dtype):
    return int(np.prod(shape)) * jnp.dtype(dtype).itemsize


def _cparams(semantics, block_bytes, temp_bytes=0):
    need = 2 * block_bytes + temp_bytes + (4 << 20)
    return pltpu.CompilerParams(dimension_semantics=semantics,
                                vmem_limit_bytes=int(min(max(need, 16 << 20), VMEM_BUDGET)))


def _pick_tile(n, target, align):
    best = None
    for d in range(align, min(n, target) + 1, align):
        if n % d == 0:
            best = d
    return best if best is not None else n


def _dot(a, b):
    return jnp.dot(a, b, preferred_element_type=F32)


def _dot_nt(a, b):
    return lax.dot_general(a, b, (((1,), (1,)), ((), ())), preferred_element_type=F32)


def _dot_tn(a, b):
    return lax.dot_general(a, b, (((0,), (0,)), ((), ())), preferred_element_type=F32)


def _log_sigmoid(x):
    return -(jnp.maximum(-x, 0.0) + jnp.log1p(jnp.exp(-jnp.abs(x))))


def _split2(x):
    hi = x.astype(BF16)
    lo = (x - hi.astype(F32)).astype(BF16)
    return hi, lo


def _split3(x):
    hi = x.astype(BF16)
    r = x - hi.astype(F32)
    mid = r.astype(BF16)
    lo = (r - mid.astype(F32)).astype(BF16)
    return hi, mid, lo


def _rmsnorm_kernel(x_ref, g_ref, o_ref):
    x = x_ref[...]
    y = x * lax.rsqrt(jnp.mean(x * x, axis=-1, keepdims=True) + EPS)
    o_ref[...] = (y * g_ref[...]).astype(o_ref.dtype)


def _rmsnorm(x, g, out_dtype, row_start=0, n_rows=None):
    m_all, d = x.shape
    n_rows = m_all - row_start if n_rows is None else n_rows
    tm = _pick_tile(math.gcd(n_rows, row_start) if row_start else n_rows, 544, 16)
    off = row_start // tm
    blk = _nbytes((tm, d), F32) + _nbytes((tm, d), out_dtype)
    return pl.pallas_call(
        _rmsnorm_kernel,
        grid=(n_rows // tm,),
        in_specs=[pl.BlockSpec((tm, d), lambda i: (i + off, 0)),
                  pl.BlockSpec((1, d), lambda i: (0, 0))],
        out_specs=pl.BlockSpec((tm, d), lambda i: (i, 0)),
        out_shape=jax.ShapeDtypeStruct((n_rows, d), out_dtype),
        compiler_params=_cparams(("parallel",), blk, _nbytes((tm, d), F32)),
        name="rmsnorm",
    )(x, g.reshape(1, d).astype(F32))


def _mm_kernel(*refs, epi, n_extra, n_out, cast_w):
    x_ref, w_ref = refs[0], refs[1]
    extras = refs[2:2 + n_extra]
    outs = refs[2 + n_extra:2 + n_extra + n_out]
    if cast_w:
        wb_ref = refs[2 + n_extra + n_out]

        @pl.when(pl.program_id(1) == 0)
        def _():
            wb_ref[...] = w_ref[...].astype(BF16)
    else:
        wb_ref = w_ref

    acc = _dot(x_ref[...], wb_ref[...])
    vals = epi(acc, *(e[...] for e in extras))
    for o_ref, val in zip(outs, vals):
        o_ref[...] = val.astype(o_ref.dtype)


def _matmul(x, w, epi, extras=(), out_dtypes=(F32,), name="matmul", layer=None):
    m, k = x.shape
    n = w.shape[-1]
    cast_w = w.dtype != BF16
    w_item = jnp.dtype(w.dtype).itemsize
    tn = _pick_tile(n, max(LANE, min(MM_TN_MAX, MM_W_TILE_BYTES // (w_item * k) // LANE * LANE)), LANE)
    n_tile_io = len(out_dtypes) + sum(kind == "tile" for kind, _ in extras)

    def vmem_need(tm):
        blocks = tm * k * 2 + k * tn * w_item + n_tile_io * tm * tn * 4
        return 2 * blocks + (k * tn * 2 if cast_w else 0) + 3 * tm * tn * 4

    tm = _pick_tile(m, 16, 16)
    for cand in sorted((d for d in range(16, m + 1, 16) if m % d == 0), reverse=True):
        if cand * k * 2 <= MM_X_TILE_BYTES and vmem_need(cand) <= MM_VMEM_TARGET:
            tm = cand
            break
    w_spec = (pl.BlockSpec((k, tn), lambda j, i: (0, j)) if layer is None else
              pl.BlockSpec((None, k, tn), lambda j, i: (layer, 0, j)))
    in_specs = [pl.BlockSpec((tm, k), lambda j, i: (i, 0)), w_spec]
    args = [x, w]
    blk = _nbytes((tm, k), x.dtype) + _nbytes((k, tn), w.dtype)
    for kind, arr in extras:
        if kind == "row":
            in_specs.append(pl.BlockSpec((1, tn), lambda j, i: (0, j)))
            args.append(arr.reshape(1, n).astype(F32))
            blk += _nbytes((8, tn), F32)
        else:
            in_specs.append(pl.BlockSpec((tm, tn), lambda j, i: (i, j)))
            args.append(arr)
            blk += _nbytes((tm, tn), arr.dtype)
    out_specs = [pl.BlockSpec((tm, tn), lambda j, i: (i, j)) for _ in out_dtypes]
    out_shape = [jax.ShapeDtypeStruct((m, n), dt) for dt in out_dtypes]
    blk += sum(_nbytes((tm, tn), dt) for dt in out_dtypes)
    outs = pl.pallas_call(
        functools.partial(_mm_kernel, epi=epi, n_extra=len(extras), n_out=len(out_dtypes), cast_w=cast_w),
        grid=(n // tn, m // tm),
        in_specs=in_specs, out_specs=out_specs, out_shape=out_shape,
        scratch_shapes=[pltpu.VMEM((k, tn), BF16)] if cast_w else [],
        compiler_params=_cparams(("parallel", "arbitrary"), blk,
                                 (_nbytes((k, tn), BF16) if cast_w else 0) + 3 * _nbytes((tm, tn), F32)),
        name=name,
    )(*args)
    return outs


def _epi_id(acc):
    return (acc,)


def _epi_silu(acc):
    return (acc * jax.nn.sigmoid(acc),)


def _epi_sigmoid(acc):
    return (jax.nn.sigmoid(acc),)


def _epi_scale(acc, *, scale):
    return (acc * scale,)


def _epi_residual(acc, res):
    return (res + acc,)


def _epi_hgrn_forget(acc, lb):
    lbh = jnp.maximum(lb, 0.0)
    a = jnp.log(lbh)
    c = jnp.log1p(-lbh) + _log_sigmoid(acc)
    logf = jnp.maximum(a, c) + jnp.log1p(jnp.exp(-jnp.abs(a - c)))
    kk = (1.0 - lbh) * jax.nn.sigmoid(-acc)
    return logf, kk


def _epi_hgrn_forget_no_floor(acc):
    e = jnp.exp(-jnp.abs(acc))
    logf = -(jnp.maximum(-acc, 0.0) + jnp.log1p(e))
    kk = jnp.where(acc >= 0.0, e, 1.0) / (1.0 + e)
    return logf, kk


def _epi_bias_gelu(acc, b):
    x = acc + b
    return (0.5 * x * (1.0 + lax.erf(x * (2.0 ** -0.5))),)


def _epi_mlstm_gates(acc, b, *, heads):
    gates = GATE_CAP * jnp.tanh((acc + b) / GATE_CAP)
    col = lax.broadcasted_iota(jnp.int32, acc.shape, 1)
    return (jnp.where(col < heads, gates, _log_sigmoid(gates)),)


def _hgrn_levels(chunk, t_real):
    return tuple(m for m in (2 ** p for p in range(int(math.log2(chunk)) - 1, -1, -1)) if m < t_real)


def _hgrn_sum_matrix(chunk, levels, t_len):
    t = np.arange(chunk)[:, None]
    r = np.arange(chunk)[None, :]
    same = (t // t_len) == (r // t_len)
    mats = [(r <= t) & same, (r > t) & same]
    for m in levels:
        mid = (t // (2 * m)) * (2 * m) + m - 1
        second = (t % (2 * m)) >= m
        mats.append(np.where(second, (r > mid) & (r <= t), (r > t) & (r <= mid)))
    return np.concatenate(mats, axis=0).astype(np.float32)


def _hgrn_kernel(*refs, heads, chunk, levels, dk, group, n_aliased, slot):
    q_ref, lf_ref, k_ref, v_ref, g_ref, onorm_ref, d_ref, s0_ref = refs[:8]
    y_ref, s_ref, o_scr = refs[8 + n_aliased:]
    c = pl.program_id(1)
    t_len = chunk // group
    dv = s_ref.shape[-1]

    @pl.when(c == 0)
    def _():
        for other in range(s_ref.shape[0]):
            if other != slot:
                s_ref[other] = jnp.zeros(s_ref.shape[1:], F32)
        s_ref[slot] = s0_ref[0]

    row = lax.broadcasted_iota(jnp.int32, (chunk, 1), 0)
    in_seq = [None] if group == 1 else [(row >> int(math.log2(t_len))) == s for s in range(group)]
    seq_ones = [jnp.ones((chunk, dv), BF16) if mask is None else
                jnp.where(mask, jnp.ones((chunk, dv), F32), 0.0).astype(BF16) for mask in in_seq]
    r2 = lax.broadcasted_iota(jnp.int32, (chunk, chunk), 0)
    c2 = lax.broadcasted_iota(jnp.int32, (chunk, chunk), 1)

    lf_hi, lf_lo = _split2(lf_ref[...])
    xs = _dot(d_ref[...], jnp.concatenate([lf_hi, lf_lo], axis=0))
    q = q_ref[...].astype(F32)
    k = k_ref[...].astype(F32)
    vb = v_ref[...].astype(BF16)
    ex_b = jnp.exp(xs[0:chunk])
    q_in = q * ex_b
    k_out = k * jnp.exp(xs[chunk:2 * chunk])
    q_in = [(q_in if mask is None else jnp.where(mask, q_in, 0.0)).astype(BF16) for mask in in_seq]
    k_out = [(k_out if mask is None else jnp.where(mask, k_out, 0.0)).astype(BF16) for mask in in_seq]
    qs = [q_ref[...].astype(BF16)]
    ks = [k_ref[...].astype(BF16)]
    masks = [r2 == c2]
    for li, m in enumerate(levels):
        ex = jnp.exp(xs[(2 + li) * chunk:(3 + li) * chunk])
        second = (row & m) != 0
        qs.append(jnp.where(second, q * ex, 0.0).astype(BF16))
        ks.append(jnp.where(second, 0.0, k * ex).astype(BF16))
        shift = int(math.log2(2 * m))
        masks.append(None if 2 * m == chunk else (r2 >> shift) == (c2 >> shift))

    scores = []
    for h in range(heads):
        cols = slice(h * dk, (h + 1) * dk)
        a = None
        for qs_l, ks_l, mask in zip(qs, ks, masks):
            al = _dot_nt(qs_l[:, cols], ks_l[:, cols])
            if mask is not None:
                al = jnp.where(mask, al, 0.0)
            a = al if a is None else a + al
        scores.append(a.astype(BF16))

    for h in range(heads):
        cols = slice(h * dk, (h + 1) * dk)
        o = _dot(scores[h], vb[:, cols])
        for s in range(group):
            st = s_ref[slot, s, h]
            o = o + _dot(q_in[s][:, cols], st.astype(BF16))
            decay = jnp.exp(_dot_tn(lf_hi[:, cols], seq_ones[s]) + _dot_tn(lf_lo[:, cols], seq_ones[s]))
            s_ref[slot, s, h] = st * decay + _dot_tn(k_out[s][:, cols], vb[:, cols])
        o_scr[:, cols] = o

    o = o_scr[...]
    y = o * lax.rsqrt(jnp.mean(o * o, axis=-1, keepdims=True) + EPS) * onorm_ref[...]
    y_ref[...] = (y * g_ref[...]).astype(y_ref.dtype)


def _hgrn_scan(q, lf, k, v, g, onorm, s0_all, layer, s_out_prev, y_prev, *, row_start, batch, seq, chunk, group):
    d = q.shape[1]
    heads, dk, dv = s0_all.shape[2:]
    t_chunk = chunk // group
    n_chunks = seq // t_chunk
    assert group == 1 or n_chunks == 1
    assert row_start % chunk == 0 and batch % group == 0
    levels = _hgrn_levels(chunk, t_chunk)
    dmat = _hgrn_sum_matrix(chunk, levels, t_chunk)
    dmat = jnp.asarray(np.concatenate([dmat, dmat], axis=1), dtype=BF16)
    first_blk = row_start // chunk
    tok = pl.BlockSpec((chunk, d), lambda b, c: (first_blk + b * n_chunks + c, 0))
    in_layer = min(layer, s0_all.shape[0] - 1)
    st_in = pl.BlockSpec((1, group, heads, dk, dv), lambda b, c: (in_layer, b, 0, 0, 0))
    n_layers = s_out_prev if isinstance(s_out_prev, int) else s_out_prev.shape[0]
    fresh_state = isinstance(s_out_prev, int)
    slots = n_layers if fresh_state else 1
    st_out = pl.BlockSpec((slots, group, heads, dk, dv), lambda b, c: (0 if fresh_state else layer, b, 0, 0, 0))
    in_specs = [tok, tok, tok, tok, tok,
                pl.BlockSpec((1, d), lambda b, c: (0, 0)),
                pl.BlockSpec(dmat.shape, lambda b, c: (0, 0)),
                st_in]
    args = [q, lf, k, v, g, onorm.reshape(1, d).astype(F32), dmat, s0_all]
    aliases = {}
    for out_idx, prev in ((0, y_prev), (1, s_out_prev)):
        if not isinstance(prev, int):
            aliases[len(args)] = out_idx
            in_specs.append(pl.BlockSpec(memory_space=pl.ANY))
            args.append(prev)
    y_rows = y_prev if isinstance(y_prev, int) else y_prev.shape[0]
    blk = (sum(_nbytes((chunk, d), a.dtype) for a in (q, lf, k, v, g)) + _nbytes((chunk, d), BF16) + (1 + slots) * _nbytes((group, heads, dk, dv), F32)
           + _nbytes(dmat.shape, BF16))
    n_exp = 2 + len(levels)
    y, s = pl.pallas_call(
        functools.partial(_hgrn_kernel, heads=heads, chunk=chunk, levels=levels, dk=dk, group=group,
                          n_aliased=len(aliases), slot=layer if fresh_state else 0),
        grid=(batch // group, n_chunks),
        in_specs=in_specs,
        out_specs=[tok, st_out],
        out_shape=[jax.ShapeDtypeStruct((y_rows, d), BF16),
                   jax.ShapeDtypeStruct((n_layers, batch, heads, dk, dv), F32)],
        scratch_shapes=[pltpu.VMEM((chunk, d), F32)],
        input_output_aliases=aliases,
        compiler_params=_cparams(("parallel", "arbitrary"), blk, (3 * n_exp + 8) * _nbytes((chunk, d), F32)),
        name="hgrn_scan",
    )(*args)
    return y, s


def _mlstm_kernel(*refs, heads, chunk, dqk, dv, group, n_aliased):
    q_ref, k_ref, v_ref, og_ref, gt_ref, hn_ref, tri_ref, sel_ref, c0_ref, n0_ref, m0_ref = refs[:11]
    y_ref, c_ref, n_ref, m_ref = refs[11 + n_aliased:]
    c = pl.program_id(1)
    t_len = chunk // group

    @pl.when(c == 0)
    def _():
        c_ref[...] = c0_ref[...]
        n_ref[...] = n0_ref[...]
        m_ref[...] = m0_ref[...]

    gates = gt_ref[...]
    tri = tri_ref[...]
    sel = sel_ref[...]
    g3 = _split3(gates)
    cum = sum(_dot(tri, p) for p in g3)
    gates_t = sum(_dot_nt(sel, p) for p in g3)
    cum_t = sum(_dot_nt(sel, p) for p in _split3(cum))
    row = lax.broadcasted_iota(jnp.int32, (chunk, 1), 0)
    r2 = lax.broadcasted_iota(jnp.int32, (chunk, chunk), 0)
    c2 = lax.broadcasted_iota(jnp.int32, (chunk, chunk), 1)
    causal = c2 <= r2
    in_seq = [None]
    if group > 1:
        shift = int(math.log2(t_len))
        causal = causal & ((r2 >> shift) == (c2 >> shift))
        in_seq = [(row >> shift) == s for s in range(group)]

    def per_row(vals):
        if group == 1:
            return vals[0]
        out = jnp.where(in_seq[0], vals[0], 0.0)
        for s in range(1, group):
            out = jnp.where(in_seq[s], vals[s], out)
        return out

    lane = lax.broadcasted_iota(jnp.int32, (1, LANE), 1)
    m_rows = [m_ref[s] for s in range(group)]
    m_new = list(m_rows)
    for h in range(heads):
        qc = slice(h * dqk, (h + 1) * dqk)
        vc = slice(h * dv, (h + 1) * dv)
        qb = q_ref[:, qc].astype(BF16)
        kb = k_ref[:, qc].astype(BF16)
        qh = qb.astype(F32)
        kh = kb.astype(F32)
        vb = v_ref[:, vc].astype(BF16)
        b_c = cum[:, heads + h:heads + h + 1]
        ig_c = gates[:, h:h + 1]
        b_r = cum_t[heads + h:heads + h + 1, :]
        ig_r = gates_t[h:h + 1, :]
        m_h = [m_rows[s][:, h:h + 1] for s in range(group)]
        dlog = jnp.where(causal, b_c - b_r + ig_r, -jnp.inf)
        inter = b_c + per_row(m_h)
        mt = jnp.maximum(inter, jnp.max(dlog, axis=-1, keepdims=True))
        wts = jnp.exp(dlog - mt) * _dot_nt(qb, kb)
        sc = jnp.exp(inter - mt)
        c_h = [c_ref[s, h] for s in range(group)]
        n_h = [n_ref[s, h:h + 1, :] for s in range(group)]
        if group == 1:
            q_c = _dot(qb, c_h[0].astype(BF16))
        else:
            q_c = sum(_dot(jnp.where(in_seq[s], qh.astype(F32), 0.0).astype(BF16), c_h[s].astype(BF16))
                      for s in range(group))
        num = sc * q_c + _dot(wts.astype(BF16), vb)
        den = sc * jnp.sum(qh * per_row(n_h), axis=-1, keepdims=True) + jnp.sum(wts, axis=-1, keepdims=True)
        out = num / jnp.maximum(jnp.abs(den), jnp.exp(-mt))
        for s in range(group):
            last = (s + 1) * t_len - 1
            m_last = mt[last:last + 1, :]
            b_last = b_c[last:last + 1, :]
            sc_state = jnp.exp(b_last + m_h[s] - m_last)
            w_k = jnp.exp(b_last - b_c + ig_c - m_last)
            if group > 1:
                w_k = jnp.where(in_seq[s], w_k, 0.0)
            kw = w_k * kh
            c_ref[s, h] = sc_state * c_h[s] + _dot_tn(kw.astype(BF16), vb)
            n_ref[s, h:h + 1, :] = sc_state * n_h[s] + jnp.sum(kw, axis=0, keepdims=True)
            m_new[s] = jnp.where(lane == h, m_last, m_new[s])
        y = out * lax.rsqrt(jnp.mean(out * out, axis=-1, keepdims=True) + EPS) * hn_ref[:, vc]
        y_ref[:, vc] = (y * og_ref[:, vc]).astype(y_ref.dtype)
    for s in range(group):
        m_ref[s] = m_new[s]


def _mlstm_scan(q, k, v, og, gates, hnorm, c0, n0, m0, y_prev, *, row_start, batch, seq, chunk, group):
    heads, dqk, dv = c0.shape[1], c0.shape[2], c0.shape[3]
    dq_all, dv_all = q.shape[1], v.shape[1]
    t_chunk = chunk // group
    n_chunks = seq // t_chunk
    assert 2 * heads <= 16 and (group == 1 or n_chunks == 1)
    assert row_start % chunk == 0 and batch % group == 0
    idx = np.arange(chunk)
    same_seq = (idx[:, None] // t_chunk) == (idx[None, :] // t_chunk)
    tri = jnp.asarray(np.tril(np.ones((chunk, chunk), np.float32)) * same_seq, dtype=BF16)
    sel = jnp.asarray(np.eye(16, LANE, dtype=np.float32), dtype=BF16)
    m0p = jnp.pad(m0, ((0, 0), (0, LANE - heads))).reshape(batch, 1, LANE)
    first_blk = row_start // chunk
    tq = pl.BlockSpec((chunk, dq_all), lambda b, c: (first_blk + b * n_chunks + c, 0))
    tv = pl.BlockSpec((chunk, dv_all), lambda b, c: (first_blk + b * n_chunks + c, 0))
    tg = pl.BlockSpec((chunk, LANE), lambda b, c: (first_blk + b * n_chunks + c, 0))
    sc_ = pl.BlockSpec((group, heads, dqk, dv), lambda b, c: (b, 0, 0, 0))
    sn_ = pl.BlockSpec((group, heads, dqk), lambda b, c: (b, 0, 0))
    sm_ = pl.BlockSpec((group, 1, LANE), lambda b, c: (b, 0, 0))
    in_specs = [tq, tq, tv, tv, tg,
                pl.BlockSpec((1, dv_all), lambda b, c: (0, 0)),
                pl.BlockSpec(tri.shape, lambda b, c: (0, 0)),
                pl.BlockSpec(sel.shape, lambda b, c: (0, 0)),
                sc_, sn_, sm_]
    args = [q, k, v, og, gates, hnorm.reshape(1, dv_all).astype(F32), tri, sel, c0, n0, m0p]
    aliases = {}
    if not isinstance(y_prev, int):
        aliases[len(args)] = 0
        in_specs.append(pl.BlockSpec(memory_space=pl.ANY))
        args.append(y_prev)
    y_rows = y_prev if isinstance(y_prev, int) else y_prev.shape[0]
    blk = (_nbytes((chunk, dq_all), q.dtype) + _nbytes((chunk, dq_all), k.dtype) + _nbytes((chunk, dv_all), v.dtype)
           + _nbytes((chunk, dv_all), og.dtype) + _nbytes((chunk, LANE), F32)
           + _nbytes((chunk, dv_all), BF16) + 2 * _nbytes((group,) + c0.shape[1:], F32))
    y, c_out, n_out, m_out = pl.pallas_call(
        functools.partial(_mlstm_kernel, heads=heads, chunk=chunk, dqk=dqk, dv=dv, group=group,
                          n_aliased=len(aliases)),
        grid=(batch // group, n_chunks),
        in_specs=in_specs,
        out_specs=[tv, sc_, sn_, sm_],
        out_shape=[jax.ShapeDtypeStruct((y_rows, dv_all), BF16),
                   jax.ShapeDtypeStruct(c0.shape, F32),
                   jax.ShapeDtypeStruct(n0.shape, F32),
                   jax.ShapeDtypeStruct((batch, 1, LANE), F32)],
        input_output_aliases=aliases,
        compiler_params=_cparams(("parallel", "arbitrary"), blk, 8 * _nbytes((chunk, dv_all), F32)),
        name="mlstm_scan",
    )(*args)
    return y, c_out, n_out, m_out[:, 0, :heads]


def _gmlp_kernel(u_ref, v_ref, vg_ref, vb_ref, w_ref, bs_ref, o_ref, vn_ref, *, groups, gd):
    v = v_ref[...]
    mu = jnp.mean(v, axis=-1, keepdims=True)
    xc = v - mu
    vn = xc * lax.rsqrt(jnp.mean(xc * xc, axis=-1, keepdims=True) + EPS) * vg_ref[...] + vb_ref[...]
    vn_ref[...] = vn
    n = v.shape[0]
    causal = (lax.broadcasted_iota(jnp.int32, (n, n), 1) <= lax.broadcasted_iota(jnp.int32, (n, n), 0))
    bs = bs_ref[0]
    for g in range(groups):
        cols = slice(g * gd, (g + 1) * gd)
        wg = jnp.where(causal, w_ref[0, g], 0.0).astype(BF16)
        mix = _dot(wg, vn[:, cols].astype(BF16)) + bs[:, g:g + 1]
        o_ref[:, cols] = (u_ref[:, cols] * mix).astype(o_ref.dtype)


def _gmlp_gate(z, vg, vb, w_stack, bs_stack, *, n_prompt_chunks):
    m, d2 = z.shape
    d = d2 // 2
    groups, chunk = w_stack.shape[1], w_stack.shape[2]
    gd = d // groups
    assert m % chunk == 0

    def which(i):
        return jnp.minimum(i // n_prompt_chunks, 1)

    blk = (2 * _nbytes((chunk, d), F32) + _nbytes((chunk, d), BF16) + _nbytes((chunk, d), F32)
           + _nbytes((groups, chunk, chunk), F32))
    o, vn = pl.pallas_call(
        functools.partial(_gmlp_kernel, groups=groups, gd=gd),
        grid=(m // chunk,),
        in_specs=[pl.BlockSpec((chunk, d), lambda i: (i, 0)),
                  pl.BlockSpec((chunk, d), lambda i: (i, 1)),
                  pl.BlockSpec((1, d), lambda i: (0, 0)),
                  pl.BlockSpec((1, d), lambda i: (0, 0)),
                  pl.BlockSpec((1, groups, chunk, chunk), lambda i: (which(i), 0, 0, 0)),
                  pl.BlockSpec((1, chunk, groups), lambda i: (which(i), 0, 0))],
        out_specs=[pl.BlockSpec((chunk, d), lambda i: (i, 0)),
                   pl.BlockSpec((chunk, d), lambda i: (jnp.maximum(i - n_prompt_chunks, 0), 0))],
        out_shape=[jax.ShapeDtypeStruct((m, d), BF16),
                   jax.ShapeDtypeStruct((m - n_prompt_chunks * chunk, d), F32)],
        compiler_params=_cparams(("arbitrary",), blk, 4 * _nbytes((chunk, d), F32)),
        name="gmlp_gate",
    )(z, z, vg.reshape(1, d).astype(F32), vb.reshape(1, d).astype(F32), w_stack, bs_stack)
    return o, vn


def _causal_conv(tap, cw_ref, cb_ref, conv_w):
    y = cb_ref[...] + cw_ref[conv_w - 1:conv_w, :] * tap(0)
    for back in range(1, conv_w):
        y = y + cw_ref[conv_w - 1 - back:conv_w - back, :] * tap(back)
    return y


def _ffn_up_prompt_kernel(h_ref, wa_ref, wg_ref, cwa_ref, cwg_ref, cba_ref, cbg_ref,
                          _, act_ref, tail_a_ref, tail_g_ref, wb_scr, up_scr, *, tiles_per_seq, conv_w, row_chunk):
    i = pl.program_id(1)
    tm = h_ref.shape[0]
    n_chunks = tm // row_chunk

    @pl.when(i == 0)
    def _():
        wb_scr[0] = wa_ref[...].astype(BF16)
        wb_scr[1] = wg_ref[...].astype(BF16)

    @pl.when(lax.rem(i, tiles_per_seq) == 0)
    def _():
        up_scr[:, 0:8, :] = jnp.zeros((2, 8, up_scr.shape[2]), F32)

    def multiply(c):
        hb = h_ref[c * row_chunk:(c + 1) * row_chunk, :]
        for idx in range(2):
            up_scr[idx, 8 + c * row_chunk:8 + (c + 1) * row_chunk, :] = _dot(hb, wb_scr[idx])

    def finish(c):
        lo = 8 + c * row_chunk
        ya = _causal_conv(lambda back: up_scr[0, lo - back:lo - back + row_chunk, :], cwa_ref, cba_ref, conv_w)
        yg = _causal_conv(lambda back: up_scr[1, lo - back:lo - back + row_chunk, :], cwg_ref, cbg_ref, conv_w)
        act_ref[c * row_chunk:(c + 1) * row_chunk, :] = (ya * (yg * jax.nn.sigmoid(yg))).astype(act_ref.dtype)

    multiply(0)
    for c in range(1, n_chunks):
        multiply(c)
        finish(c - 1)
    finish(n_chunks - 1)
    for idx, tail_ref in ((0, tail_a_ref), (1, tail_g_ref)):
        last = up_scr[idx, tm:tm + 8, :]
        tail_ref[0] = last
        up_scr[idx, 0:8, :] = last


def _ffn_up_prompt(h, w_up, layer, cw, cb, act_prev, *, rows, seq):
    _, k, f2 = w_up.shape
    f = f2 // 2
    conv_w = cw.shape[0]
    tm = _pick_tile(seq, 1024, 16)
    tf = _pick_tile(f, FFN_TF, LANE)
    nf = f // tf
    n_tiles = rows // tm
    row_chunk = _pick_tile(tm, FFN_ROW_CHUNK, 16)
    blk = (_nbytes((tm, k), BF16) + 2 * _nbytes((k, tf), F32) + _nbytes((tm, tf), BF16)
           + 2 * _nbytes((8, tf), F32) + 6 * _nbytes((8, tf), F32))
    tail = jax.ShapeDtypeStruct((n_tiles, 8, f), F32)
    act, tail_a, tail_g = pl.pallas_call(
        functools.partial(_ffn_up_prompt_kernel, tiles_per_seq=seq // tm, conv_w=conv_w, row_chunk=row_chunk),
        grid=(nf, n_tiles),
        in_specs=[pl.BlockSpec((tm, k), lambda j, i: (i, 0)),
                  pl.BlockSpec((None, k, tf), lambda j, i: (layer, 0, j)),
                  pl.BlockSpec((None, k, tf), lambda j, i: (layer, 0, j + nf)),
                  pl.BlockSpec((conv_w, tf), lambda j, i: (0, j)),
                  pl.BlockSpec((conv_w, tf), lambda j, i: (0, j + nf)),
                  pl.BlockSpec((1, tf), lambda j, i: (0, j)),
                  pl.BlockSpec((1, tf), lambda j, i: (0, j + nf)),
                  pl.BlockSpec(memory_space=pl.ANY)],
        out_specs=[pl.BlockSpec((tm, tf), lambda j, i: (i, j)),
                   pl.BlockSpec((1, 8, tf), lambda j, i: (i, 0, j)),
                   pl.BlockSpec((1, 8, tf), lambda j, i: (i, 0, j))],
        out_shape=[jax.ShapeDtypeStruct(act_prev.shape, BF16), tail, tail],
        input_output_aliases={7: 0},
        scratch_shapes=[pltpu.VMEM((2, k, tf), BF16), pltpu.VMEM((2, 8 + tm, tf), F32)],
        compiler_params=_cparams(("parallel", "arbitrary"), blk,
                                 2 * _nbytes((k, tf), BF16) + 2 * _nbytes((8 + tm, tf), F32)
                                 + 6 * _nbytes((row_chunk, tf), F32)),
        name="ffn_up_conv_prompt",
    )(h, w_up, w_up, cw, cw, cb.reshape(1, f2), cb.reshape(1, f2), act_prev)
    tiles_per_seq = seq // tm
    keep = slice(8 - (conv_w - 1), 8)
    tails = jnp.concatenate([tail_a[tiles_per_seq - 1::tiles_per_seq, keep], tail_g[tiles_per_seq - 1::tiles_per_seq, keep]],
                            axis=-1)
    return act, tails


def _ffn_up_sample_kernel(*refs, steps, conv_w):
    h_ref, wa_ref, wg_ref, cwa_ref, cwg_ref, cba_ref, cbg_ref = refs[:7]
    bufs_a = refs[7:7 + conv_w - 1]
    bufs_g = refs[7 + conv_w - 1:7 + 2 * (conv_w - 1)]
    act_ref, upa_ref, upg_ref = refs[7 + 2 * (conv_w - 1):]
    bs = h_ref.shape[0] // steps
    hb = h_ref[...]

    def branch(w_ref, bufs, cw_ref, cb_ref, up_ref):
        up = _dot(hb, w_ref[...].astype(BF16))
        up_ref[...] = up

        def at(t):
            return up[t * bs:(t + 1) * bs, :] if t >= 0 else bufs[conv_w - 1 + t][...]

        return [_causal_conv(lambda back, t=t: at(t - back), cw_ref, cb_ref, conv_w) for t in range(steps)]

    ya = branch(wa_ref, bufs_a, cwa_ref, cba_ref, upa_ref)
    yg = branch(wg_ref, bufs_g, cwg_ref, cbg_ref, upg_ref)
    for t in range(steps):
        act_ref[t * bs:(t + 1) * bs, :] = (ya[t] * (yg[t] * jax.nn.sigmoid(yg[t]))).astype(act_ref.dtype)


def _ffn_up_sample(h_t, w_up, layer, cw, cb, conv_state, *, steps):
    _, k, f2 = w_up.shape
    f = f2 // 2
    conv_w = cw.shape[0]
    ms = h_t.shape[0]
    bs = ms // steps
    assert steps >= conv_w - 1 and bs % 16 == 0 and conv_state.shape[1] == bs
    tf = _pick_tile(f, FFN_TF, LANE)
    nf = f // tf
    state_specs = [pl.BlockSpec((None, bs, tf), lambda j, r=r, off=off: (r, 0, off + j))
                   for off in (0, nf) for r in range(conv_w - 1)]
    blk = (_nbytes((ms, k), BF16) + 2 * _nbytes((k, tf), F32) + 2 * _nbytes((conv_w - 1, bs, tf), F32)
           + _nbytes((ms, tf), BF16) + 2 * _nbytes((ms, tf), F32) + 6 * _nbytes((8, tf), F32))
    up_shape = jax.ShapeDtypeStruct((ms, f), F32)
    act, up_a, up_g = pl.pallas_call(
        functools.partial(_ffn_up_sample_kernel, steps=steps, conv_w=conv_w),
        grid=(nf,),
        in_specs=[pl.BlockSpec((ms, k), lambda j: (0, 0)),
                  pl.BlockSpec((None, k, tf), lambda j: (layer, 0, j)),
                  pl.BlockSpec((None, k, tf), lambda j: (layer, 0, j + nf)),
                  pl.BlockSpec((conv_w, tf), lambda j: (0, j)),
                  pl.BlockSpec((conv_w, tf), lambda j: (0, j + nf)),
                  pl.BlockSpec((1, tf), lambda j: (0, j)),
                  pl.BlockSpec((1, tf), lambda j: (0, j + nf))] + state_specs,
        out_specs=[pl.BlockSpec((ms, tf), lambda j: (0, j))] * 3,
        out_shape=[jax.ShapeDtypeStruct((ms, f), BF16), up_shape, up_shape],
        compiler_params=_cparams(("parallel",), blk, 2 * _nbytes((k, tf), BF16) + 8 * _nbytes((ms, tf), F32)),
        name="ffn_up_conv_sample",
    )(h_t, w_up, w_up, cw, cw, cb.reshape(1, f2), cb.reshape(1, f2), *([conv_state] * (2 * (conv_w - 1))))
    keep = slice((steps - (conv_w - 1)) * bs, ms)
    up_tail = jnp.concatenate([up_a[keep], up_g[keep]], axis=-1).reshape(conv_w - 1, bs, f2).transpose(1, 0, 2)
    return act, up_tail


def _forward(x_prompt, x_sample, state_hgrn_S, state_mlstm_C, state_mlstm_n, state_mlstm_m, state_ffn_conv,
             norm_mix, norm_ffn, norm_final,
             hgrn_w_q, hgrn_w_f, hgrn_w_i, hgrn_w_g, hgrn_lb, hgrn_onorm, hgrn_w_o,
             mlstm_w_q, mlstm_w_k, mlstm_w_v, mlstm_w_og, mlstm_w_if, mlstm_b_if, mlstm_hnorm, mlstm_w_out,
             gmlp_w_in, gmlp_b_in, gmlp_vnorm_g, gmlp_vnorm_b, gmlp_w_s, gmlp_b_s, gmlp_w_out,
             ffn_w_up, ffn_conv_w, ffn_conv_b, ffn_w_down):
    bp, tp, d = x_prompt.shape
    bs, ts, _ = x_sample.shape
    mp, ms = bp * tp, bs * ts
    m = mp + ms
    depth = norm_mix.shape[0]
    hg_heads, hg_dk, hg_dv = state_hgrn_S.shape[2:]
    ml_heads, ml_dqk, ml_dv = state_mlstm_C.shape[2:]

    x = jnp.concatenate([x_prompt.reshape(mp, d), x_sample.reshape(ms, d)], axis=0)

    lb = jax.nn.softmax(hgrn_lb.astype(F32), axis=0)
    lbs = jnp.cumsum(lb, axis=0) - lb[0]

    w_down = ffn_w_down.astype(BF16)
    act = jnp.zeros((m, ffn_w_down.shape[1]), BF16)
    out_c, out_n, out_m, out_v, out_conv = [], [], [], [], []
    s_p_all = s_s_all = state_hgrn_S.shape[0]
    for i in range(depth):
        j = i // N_MIXERS
        kind = i % N_MIXERS
        h = _rmsnorm(x, norm_mix[i], BF16)
        if kind == 0:
            (q,) = _matmul(h, hgrn_w_q, _epi_silu, out_dtypes=(BF16,), name="hgrn_q", layer=j)
            if j == 0:
                lf, k = _matmul(h, hgrn_w_f, _epi_hgrn_forget_no_floor,
                                out_dtypes=(F32, BF16), name="hgrn_f", layer=j)
            else:
                lf, k = _matmul(h, hgrn_w_f, _epi_hgrn_forget, extras=[("row", lbs[j])],
                                out_dtypes=(F32, BF16), name="hgrn_f", layer=j)
            (v,) = _matmul(h, hgrn_w_i, _epi_id, out_dtypes=(BF16,), name="hgrn_i", layer=j)
            (g,) = _matmul(h, hgrn_w_g, _epi_sigmoid, out_dtypes=(BF16,), name="hgrn_g", layer=j)
            y, s_p_all = _hgrn_scan(q, lf, k, v, g, hgrn_onorm[j],
                                    jnp.zeros((1, bp, hg_heads, hg_dk, hg_dv), F32), j, s_p_all, h,
                                    row_start=0, batch=bp, seq=tp, chunk=min(HGRN_CHUNK, tp), group=1)
            group = max(1, SAMPLE_ROWS // ts)
            y, s_s_all = _hgrn_scan(q, lf, k, v, g, hgrn_onorm[j], state_hgrn_S.astype(F32), j, s_s_all, y,
                                    row_start=mp, batch=bs, seq=ts, chunk=group * ts, group=group)
            (x,) = _matmul(y, hgrn_w_o, _epi_residual, extras=[("tile", x)], name="hgrn_o", layer=j)
        elif kind == 1:
            (q,) = _matmul(h, mlstm_w_q, _epi_id, out_dtypes=(BF16,), name="mlstm_q", layer=j)
            (k,) = _matmul(h, mlstm_w_k, functools.partial(_epi_scale, scale=ml_dqk ** -0.5),
                           out_dtypes=(BF16,), name="mlstm_k", layer=j)
            (v,) = _matmul(h, mlstm_w_v, _epi_id, out_dtypes=(BF16,), name="mlstm_v", layer=j)
            (og,) = _matmul(h, mlstm_w_og, _epi_sigmoid, out_dtypes=(BF16,), name="mlstm_og", layer=j)
            w_if = jnp.pad(mlstm_w_if[j], ((0, 0), (0, LANE - 2 * ml_heads)))
            b_if = jnp.pad(mlstm_b_if[j], (0, LANE - 2 * ml_heads))
            (gates,) = _matmul(h, w_if, functools.partial(_epi_mlstm_gates, heads=ml_heads),
                               extras=[("row", b_if)], name="mlstm_if")
            y, c_p, n_p, m_p = _mlstm_scan(
                q, k, v, og, gates, mlstm_hnorm[j],
                jnp.zeros((bp, ml_heads, ml_dqk, ml_dv), F32), jnp.zeros((bp, ml_heads, ml_dqk), F32),
                jnp.zeros((bp, ml_heads), F32), h,
                row_start=0, batch=bp, seq=tp, chunk=min(MLSTM_CHUNK, tp), group=1)
            group = max(1, SAMPLE_ROWS // ts)
            y, c_s, n_s, m_s = _mlstm_scan(
                q, k, v, og, gates, mlstm_hnorm[j],
                state_mlstm_C[j].astype(F32), state_mlstm_n[j].astype(F32), state_mlstm_m[j].astype(F32), y,
                row_start=mp, batch=bs, seq=ts, chunk=group * ts, group=group)
            out_c.append((c_p, c_s))
            out_n.append((n_p, n_s))
            out_m.append((m_p, m_s))
            (x,) = _matmul(y, mlstm_w_out, _epi_residual, extras=[("tile", x)], name="mlstm_out", layer=j)
        else:
            (z,) = _matmul(h, gmlp_w_in, _epi_bias_gelu, extras=[("row", gmlp_b_in[j])], name="gmlp_in", layer=j)
            groups = gmlp_w_s.shape[1]
            lp = min(GMLP_CHUNK, tp)
            ls = min(GMLP_CHUNK, ts)
            reps = GMLP_CHUNK // ls
            w_s_blk = jnp.einsum("ab,gts->gatbs", jnp.eye(reps, dtype=F32), gmlp_w_s[j][:, :ls, :ls])
            w_s_blk = w_s_blk.reshape(groups, GMLP_CHUNK, GMLP_CHUNK)
            w_p = gmlp_w_s[j][:, :lp, :lp]
            w_stack = jnp.stack([w_p, w_s_blk])
            bs_stack = jnp.stack([gmlp_b_s[j][:, :lp].T, jnp.tile(gmlp_b_s[j][:, :ls].T, (reps, 1))])
            y, vn = _gmlp_gate(z, gmlp_vnorm_g[j], gmlp_vnorm_b[j], w_stack, bs_stack,
                               n_prompt_chunks=mp // GMLP_CHUNK)
            out_v.append(vn.reshape(bs, ts, d))
            (x,) = _matmul(y, gmlp_w_out, _epi_residual, extras=[("tile", x)], name="gmlp_out", layer=j)

        h = _rmsnorm(x, norm_ffn[i], BF16)
        act, tail_p = _ffn_up_prompt(h, ffn_w_up, i, ffn_conv_w[i], ffn_conv_b[i], act, rows=mp, seq=tp)
        h_t = h[mp:].reshape(bs, ts, d).transpose(1, 0, 2).reshape(ms, d)
        act_t, tail_s = _ffn_up_sample(h_t, ffn_w_up, i, ffn_conv_w[i], ffn_conv_b[i],
                                       state_ffn_conv[i].astype(F32).transpose(1, 0, 2), steps=ts)
        act_s = act_t.reshape(ts, bs, -1).transpose(1, 0, 2).reshape(ms, -1)
        act = lax.dynamic_update_slice(act, act_s, (mp, 0))
        out_conv.append((tail_p, tail_s))
        (x,) = _matmul(act, w_down, _epi_residual, extras=[("tile", x)], name="ffn_down", layer=i)

    y_p = _rmsnorm(x, norm_final, F32, row_start=0, n_rows=mp).reshape(bp, tp, d)
    y_s = _rmsnorm(x, norm_final, F32, row_start=mp, n_rows=ms).reshape(bs, ts, d)
    stack = lambda pairs, idx: jnp.stack([p[idx] for p in pairs])
    return (y_p, y_s,
            s_p_all, s_s_all, stack(out_c, 0), stack(out_c, 1),
            stack(out_n, 0), stack(out_n, 1), stack(out_m, 0), stack(out_m, 1),
            jnp.stack(out_v), stack(out_conv, 0), stack(out_conv, 1))


_forward_jit = jax.jit(_forward)


def kernel(x_prompt, x_sample, state_hgrn_S, state_mlstm_C, state_mlstm_n, state_mlstm_m, state_ffn_conv, norm_mix, norm_ffn, norm_final, hgrn_w_q, hgrn_w_f, hgrn_w_i, hgrn_w_g, hgrn_lb, hgrn_onorm, hgrn_w_o, mlstm_w_q, mlstm_w_k, mlstm_w_v, mlstm_w_og, mlstm_w_if, mlstm_b_if, mlstm_hnorm, mlstm_w_out, gmlp_w_in, gmlp_b_in, gmlp_vnorm_g, gmlp_vnorm_b, gmlp_w_s, gmlp_b_s, gmlp_w_out, ffn_w_up, ffn_conv_w, ffn_conv_b, ffn_w_down):
    return _forward_jit(x_prompt, x_sample, state_hgrn_S, state_mlstm_C, state_mlstm_n, state_mlstm_m, state_ffn_conv,
                        norm_mix, norm_ffn, norm_final,
                        hgrn_w_q, hgrn_w_f, hgrn_w_i, hgrn_w_g, hgrn_lb, hgrn_onorm, hgrn_w_o,
                        mlstm_w_q, mlstm_w_k, mlstm_w_v, mlstm_w_og, mlstm_w_if, mlstm_b_if, mlstm_hnorm, mlstm_w_out,
                        gmlp_w_in, gmlp_b_in, gmlp_vnorm_g, gmlp_vnorm_b, gmlp_w_s, gmlp_b_s, gmlp_w_out,
                        ffn_w_up, ffn_conv_w, ffn_conv_b, ffn_w_down)
```

```python
import functools
import math

import numpy as np
import jax
import jax.numpy as jnp
from jax import lax
from jax.experimental import pallas as pl
from jax.experimental.pallas import tpu as pltpu

F32 = jnp.float32
BF16 = jnp.bfloat16

EPS = 1e-6
GATE_CAP = 15.0
N_MIXERS = 3
HGRN_CHUNK = 128
MLSTM_CHUNK = 256
GMLP_CHUNK = 128
FFN_ROW_CHUNK = 256
FFN_TF = 512
SAMPLE_ROWS = 32

LANE = 128
VMEM_PHYSICAL_V7X = 64 * 1024 * 1024
VMEM_BUDGET = VMEM_PHYSICAL_V7X - 8 * 1024 * 1024

MM_TN_MAX = 1024
MM_W_TILE_BYTES = 12 << 20
MM_X_TILE_BYTES = 13 << 20
MM_VMEM_TARGET = 52 << 20


def _nbytes(shape, dtype):
    return int(np.prod(shape)) * jnp.dtype(dtype).itemsize


def _cparams(semantics, block_bytes, temp_bytes=0):
    need = 2 * block_bytes + temp_bytes + (4 << 20)
    return pltpu.CompilerParams(dimension_semantics=semantics,
                                vmem_limit_bytes=int(min(max(need, 16 << 20), VMEM_BUDGET)))


def _pick_tile(n, target, align):
    best = None
    for d in range(align, min(n, target) + 1, align):
        if n % d == 0:
            best = d
    return best if best is not None else n


def _dot(a, b):
    return jnp.dot(a, b, preferred_element_type=F32)


def _dot_nt(a, b):
    return lax.dot_general(a, b, (((1,), (1,)), ((), ())), preferred_element_type=F32)


def _dot_tn(a, b):
    return lax.dot_general(a, b, (((0,), (0,)), ((), ())), preferred_element_type=F32)


def _log_sigmoid(x):
    return -(jnp.maximum(-x, 0.0) + jnp.log1p(jnp.exp(-jnp.abs(x))))


def _split2(x):
    hi = x.astype(BF16)
    lo = (x - hi.astype(F32)).astype(BF16)
    return hi, lo


def _split3(x):
    hi = x.astype(BF16)
    r = x - hi.astype(F32)
    mid = r.astype(BF16)
    lo = (r - mid.astype(F32)).astype(BF16)
    return hi, mid, lo


def _rmsnorm_kernel(x_ref, g_ref, o_ref):
    x = x_ref[...]
    y = x * lax.rsqrt(jnp.mean(x * x, axis=-1, keepdims=True) + EPS)
    o_ref[...] = (y * g_ref[...]).astype(o_ref.dtype)


def _rmsnorm(x, g, out_dtype, row_start=0, n_rows=None):
    m_all, d = x.shape
    n_rows = m_all - row_start if n_rows is None else n_rows
    tm = _pick_tile(math.gcd(n_rows, row_start) if row_start else n_rows, 544, 16)
    off = row_start // tm
    blk = _nbytes((tm, d), F32) + _nbytes((tm, d), out_dtype)
    return pl.pallas_call(
        _rmsnorm_kernel,
        grid=(n_rows // tm,),
        in_specs=[pl.BlockSpec((tm, d), lambda i: (i + off, 0)),
                  pl.BlockSpec((1, d), lambda i: (0, 0))],
        out_specs=pl.BlockSpec((tm, d), lambda i: (i, 0)),
        out_shape=jax.ShapeDtypeStruct((n_rows, d), out_dtype),
        compiler_params=_cparams(("parallel",), blk, _nbytes((tm, d), F32)),
        name="rmsnorm",
    )(x, g.reshape(1, d).astype(F32))


def _mm_kernel(*refs, epi, n_extra, n_out, cast_w):
    x_ref, w_ref = refs[0], refs[1]
    extras = refs[2:2 + n_extra]
    outs = refs[2 + n_extra:2 + n_extra + n_out]
    if cast_w:
        wb_ref = refs[2 + n_extra + n_out]

        @pl.when(pl.program_id(1) == 0)
        def _():
            wb_ref[...] = w_ref[...].astype(BF16)
    else:
        wb_ref = w_ref

    acc = _dot(x_ref[...], wb_ref[...])
    vals = epi(acc, *(e[...] for e in extras))
    for o_ref, val in zip(outs, vals):
        o_ref[...] = val.astype(o_ref.dtype)


def _matmul(x, w, epi, extras=(), out_dtypes=(F32,), name="matmul", layer=None):
    m, k = x.shape
    n = w.shape[-1]
    cast_w = w.dtype != BF16
    w_item = jnp.dtype(w.dtype).itemsize
    tn = _pick_tile(n, max(LANE, min(MM_TN_MAX, MM_W_TILE_BYTES // (w_item * k) // LANE * LANE)), LANE)
    n_tile_io = len(out_dtypes) + sum(kind == "tile" for kind, _ in extras)

    def vmem_need(tm):
        blocks = tm * k * 2 + n_tile_io * tm * tn * 4
        return 2 * blocks + k * tn * w_item + (k * tn * 2 if cast_w else 0) + 3 * tm * tn * 4

    tm = _pick_tile(m, 16, 16)
    for cand in sorted((d for d in range(16, m + 1, 16) if m % d == 0), reverse=True):
        if cand * k * 2 <= MM_X_TILE_BYTES and vmem_need(cand) <= MM_VMEM_TARGET:
            tm = cand
            break
    w_spec = (pl.BlockSpec((k, tn), lambda j, i: (0, j), pipeline_mode=pl.Buffered(1)) if layer is None else
              pl.BlockSpec((None, k, tn), lambda j, i: (layer, 0, j), pipeline_mode=pl.Buffered(1)))
    in_specs = [pl.BlockSpec((tm, k), lambda j, i: (i, 0)), w_spec]
    args = [x, w]
    blk = _nbytes((tm, k), x.dtype) + _nbytes((k, tn), w.dtype)
    for kind, arr in extras:
        if kind == "row":
            in_specs.append(pl.BlockSpec((1, tn), lambda j, i: (0, j)))
            args.append(arr.reshape(1, n).astype(F32))
            blk += _nbytes((8, tn), F32)
        else:
            in_specs.append(pl.BlockSpec((tm, tn), lambda j, i: (i, j)))
            args.append(arr)
            blk += _nbytes((tm, tn), arr.dtype)
    out_specs = [pl.BlockSpec((tm, tn), lambda j, i: (i, j)) for _ in out_dtypes]
    out_shape = [jax.ShapeDtypeStruct((m, n), dt) for dt in out_dtypes]
    blk += sum(_nbytes((tm, tn), dt) for dt in out_dtypes)
    outs = pl.pallas_call(
        functools.partial(_mm_kernel, epi=epi, n_extra=len(extras), n_out=len(out_dtypes), cast_w=cast_w),
        grid=(n // tn, m // tm),
        in_specs=in_specs, out_specs=out_specs, out_shape=out_shape,
        scratch_shapes=[pltpu.VMEM((k, tn), BF16)] if cast_w else [],
        compiler_params=_cparams(("parallel", "arbitrary"), blk,
                                 (_nbytes((k, tn), BF16) if cast_w else 0) + 3 * _nbytes((tm, tn), F32)),
        name=name,
    )(*args)
    return outs


def _epi_id(acc):
    return (acc,)


def _epi_silu(acc):
    return (acc * jax.nn.sigmoid(acc),)


def _epi_sigmoid(acc):
    return (jax.nn.sigmoid(acc),)


def _epi_scale(acc, *, scale):
    return (acc * scale,)


def _epi_residual(acc, res):
    return (res + acc,)


def _epi_hgrn_forget(acc, lb):
    lbh = jnp.maximum(lb, 0.0)
    a = jnp.log(lbh)
    c = jnp.log1p(-lbh) + _log_sigmoid(acc)
    logf = jnp.maximum(a, c) + jnp.log1p(jnp.exp(-jnp.abs(a - c)))
    kk = (1.0 - lbh) * jax.nn.sigmoid(-acc)
    return logf, kk


def _epi_hgrn_forget_no_floor(acc):
    e = jnp.exp(-jnp.abs(acc))
    logf = -(jnp.maximum(-acc, 0.0) + jnp.log1p(e))
    kk = jnp.where(acc >= 0.0, e, 1.0) / (1.0 + e)
    return logf, kk


def _epi_bias_gelu(acc, b):
    x = acc + b
    return (0.5 * x * (1.0 + lax.erf(x * (2.0 ** -0.5))),)


def _epi_mlstm_gates(acc, b, *, heads):
    gates = GATE_CAP * jnp.tanh((acc + b) / GATE_CAP)
    col = lax.broadcasted_iota(jnp.int32, acc.shape, 1)
    return (jnp.where(col < heads, gates, _log_sigmoid(gates)),)


def _hgrn_levels(chunk, t_real):
    return tuple(m for m in (2 ** p for p in range(int(math.log2(chunk)) - 1, -1, -1)) if m < t_real)


def _hgrn_sum_matrix(chunk, levels, t_len):
    t = np.arange(chunk)[:, None]
    r = np.arange(chunk)[None, :]
    same = (t // t_len) == (r // t_len)
    mats = [(r <= t) & same, (r > t) & same]
    for m in levels:
        mid = (t // (2 * m)) * (2 * m) + m - 1
        second = (t % (2 * m)) >= m
        mats.append(np.where(second, (r > mid) & (r <= t), (r > t) & (r <= mid)))
    return np.concatenate(mats, axis=0).astype(np.float32)


def _hgrn_kernel(*refs, heads, chunk, levels, dk, group, n_aliased, slot):
    q_ref, lf_ref, k_ref, v_ref, g_ref, onorm_ref, d_ref, s0_ref = refs[:8]
    y_ref, s_ref, o_scr = refs[8 + n_aliased:]
    c = pl.program_id(1)
    t_len = chunk // group
    dv = s_ref.shape[-1]

    @pl.when(c == 0)
    def _():
        for other in range(s_ref.shape[0]):
            if other != slot:
                s_ref[other] = jnp.zeros(s_ref.shape[1:], F32)
        s_ref[slot] = s0_ref[0]

    row = lax.broadcasted_iota(jnp.int32, (chunk, 1), 0)
    in_seq = [None] if group == 1 else [(row >> int(math.log2(t_len))) == s for s in range(group)]
    seq_ones = [jnp.ones((chunk, dv), BF16) if mask is None else
                jnp.where(mask, jnp.ones((chunk, dv), F32), 0.0).astype(BF16) for mask in in_seq]
    r2 = lax.broadcasted_iota(jnp.int32, (chunk, chunk), 0)
    c2 = lax.broadcasted_iota(jnp.int32, (chunk, chunk), 1)

    lf_hi, lf_lo = _split2(lf_ref[...])
    xs = _dot(d_ref[...], jnp.concatenate([lf_hi, lf_lo], axis=0))
    q = q_ref[...].astype(F32)
    k = k_ref[...].astype(F32)
    vb = v_ref[...].astype(BF16)
    ex_b = jnp.exp(xs[0:chunk])
    q_in = q * ex_b
    k_out = k * jnp.exp(xs[chunk:2 * chunk])
    q_in = [(q_in if mask is None else jnp.where(mask, q_in, 0.0)).astype(BF16) for mask in in_seq]
    k_out = [(k_out if mask is None else jnp.where(mask, k_out, 0.0)).astype(BF16) for mask in in_seq]
    qs = [q_ref[...].astype(BF16)]
    ks = [k_ref[...].astype(BF16)]
    masks = [r2 == c2]
    for li, m in enumerate(levels):
        ex = jnp.exp(xs[(2 + li) * chunk:(3 + li) * chunk])
        second = (row & m) != 0
        qs.append(jnp.where(second, q * ex, 0.0).astype(BF16))
        ks.append(jnp.where(second, 0.0, k * ex).astype(BF16))
        shift = int(math.log2(2 * m))
        masks.append(None if 2 * m == chunk else (r2 >> shift) == (c2 >> shift))

    scores = []
    for h in range(heads):
        cols = slice(h * dk, (h + 1) * dk)
        a = None
        for qs_l, ks_l, mask in zip(qs, ks, masks):
            al = _dot_nt(qs_l[:, cols], ks_l[:, cols])
            if mask is not None:
                al = jnp.where(mask, al, 0.0)
            a = al if a is None else a + al
        scores.append(a.astype(BF16))

    for h in range(heads):
        cols = slice(h * dk, (h + 1) * dk)
        o = _dot(scores[h], vb[:, cols])
        for s in range(group):
            st = s_ref[slot, s, h]
            o = o + _dot(q_in[s][:, cols], st.astype(BF16))
            decay = jnp.exp(_dot_tn(lf_hi[:, cols], seq_ones[s]) + _dot_tn(lf_lo[:, cols], seq_ones[s]))
            s_ref[slot, s, h] = st * decay + _dot_tn(k_out[s][:, cols], vb[:, cols])
        o_scr[:, cols] = o

    o = o_scr[...]
    y = o * lax.rsqrt(jnp.mean(o * o, axis=-1, keepdims=True) + EPS) * onorm_ref[...]
    y_ref[...] = (y * g_ref[...]).astype(y_ref.dtype)


def _hgrn_scan(q, lf, k, v, g, onorm, s0_all, layer, s_out_prev, y_prev, *, row_start, batch, seq, chunk, group):
    d = q.shape[1]
    heads, dk, dv = s0_all.shape[2:]
    t_chunk = chunk // group
    n_chunks = seq // t_chunk
    assert group == 1 or n_chunks == 1
    assert row_start % chunk == 0 and batch % group == 0
    levels = _hgrn_levels(chunk, t_chunk)
    dmat = _hgrn_sum_matrix(chunk, levels, t_chunk)
    dmat = jnp.asarray(np.concatenate([dmat, dmat], axis=1), dtype=BF16)
    first_blk = row_start // chunk
    tok = pl.BlockSpec((chunk, d), lambda b, c: (first_blk + b * n_chunks + c, 0))
    in_layer = min(layer, s0_all.shape[0] - 1)
    st_in = pl.BlockSpec((1, group, heads, dk, dv), lambda b, c: (in_layer, b, 0, 0, 0))
    n_layers = s_out_prev if isinstance(s_out_prev, int) else s_out_prev.shape[0]
    fresh_state = isinstance(s_out_prev, int)
    slots = n_layers if fresh_state else 1
    st_out = pl.BlockSpec((slots, group, heads, dk, dv), lambda b, c: (0 if fresh_state else layer, b, 0, 0, 0))
    in_specs = [tok, tok, tok, tok, tok,
                pl.BlockSpec((1, d), lambda b, c: (0, 0)),
                pl.BlockSpec(dmat.shape, lambda b, c: (0, 0)),
                st_in]
    args = [q, lf, k, v, g, onorm.reshape(1, d).astype(F32), dmat, s0_all]
    aliases = {}
    for out_idx, prev in ((0, y_prev), (1, s_out_prev)):
        if not isinstance(prev, int):
            aliases[len(args)] = out_idx
            in_specs.append(pl.BlockSpec(memory_space=pl.ANY))
            args.append(prev)
    y_rows = y_prev if isinstance(y_prev, int) else y_prev.shape[0]
    blk = (sum(_nbytes((chunk, d), a.dtype) for a in (q, lf, k, v, g)) + _nbytes((chunk, d), BF16) + (1 + slots) * _nbytes((group, heads, dk, dv), F32)
           + _nbytes(dmat.shape, BF16))
    n_exp = 2 + len(levels)
    y, s = pl.pallas_call(
        functools.partial(_hgrn_kernel, heads=heads, chunk=chunk, levels=levels, dk=dk, group=group,
                          n_aliased=len(aliases), slot=layer if fresh_state else 0),
        grid=(batch // group, n_chunks),
        in_specs=in_specs,
        out_specs=[tok, st_out],
        out_shape=[jax.ShapeDtypeStruct((y_rows, d), BF16),
                   jax.ShapeDtypeStruct((n_layers, batch, heads, dk, dv), F32)],
        scratch_shapes=[pltpu.VMEM((chunk, d), F32)],
        input_output_aliases=aliases,
        compiler_params=_cparams(("parallel", "arbitrary"), blk, (3 * n_exp + 8) * _nbytes((chunk, d), F32)),
        name="hgrn_scan",
    )(*args)
    return y, s


def _mlstm_kernel(*refs, heads, chunk, dqk, dv, group, n_aliased):
    q_ref, k_ref, v_ref, og_ref, gt_ref, hn_ref, tri_ref, sel_ref, c0_ref, n0_ref, m0_ref = refs[:11]
    y_ref, c_ref, n_ref, m_ref = refs[11 + n_aliased:]
    c = pl.program_id(1)
    t_len = chunk // group

    @pl.when(c == 0)
    def _():
        c_ref[...] = c0_ref[...]
        n_ref[...] = n0_ref[...]
        m_ref[...] = m0_ref[...]

    gates = gt_ref[...]
    tri = tri_ref[...]
    sel = sel_ref[...]
    g3 = _split3(gates)
    cum = sum(_dot(tri, p) for p in g3)
    gates_t = sum(_dot_nt(sel, p) for p in g3)
    cum_t = sum(_dot_nt(sel, p) for p in _split3(cum))
    row = lax.broadcasted_iota(jnp.int32, (chunk, 1), 0)
    r2 = lax.broadcasted_iota(jnp.int32, (chunk, chunk), 0)
    c2 = lax.broadcasted_iota(jnp.int32, (chunk, chunk), 1)
    causal = c2 <= r2
    in_seq = [None]
    if group > 1:
        shift = int(math.log2(t_len))
        causal = causal & ((r2 >> shift) == (c2 >> shift))
        in_seq = [(row >> shift) == s for s in range(group)]

    def per_row(vals):
        if group == 1:
            return vals[0]
        out = jnp.where(in_seq[0], vals[0], 0.0)
        for s in range(1, group):
            out = jnp.where(in_seq[s], vals[s], out)
        return out

    lane = lax.broadcasted_iota(jnp.int32, (1, LANE), 1)
    m_rows = [m_ref[s] for s in range(group)]
    m_new = list(m_rows)
    for h in range(heads):
        qc = slice(h * dqk, (h + 1) * dqk)
        vc = slice(h * dv, (h + 1) * dv)
        qb = q_ref[:, qc].astype(BF16)
        kb = k_ref[:, qc].astype(BF16)
        qh = qb.astype(F32)
        kh = kb.astype(F32)
        vb = v_ref[:, vc].astype(BF16)
        b_c = cum[:, heads + h:heads + h + 1]
        ig_c = gates[:, h:h + 1]
        b_r = cum_t[heads + h:heads + h + 1, :]
        ig_r = gates_t[h:h + 1, :]
        m_h = [m_rows[s][:, h:h + 1] for s in range(group)]
        dlog = jnp.where(causal, b_c - b_r + ig_r, -jnp.inf)
        inter = b_c + per_row(m_h)
        mt = jnp.maximum(inter, jnp.max(dlog, axis=-1, keepdims=True))
        wts = jnp.exp(dlog - mt) * _dot_nt(qb, kb)
        sc = jnp.exp(inter - mt)
        c_h = [c_ref[s, h] for s in range(group)]
        n_h = [n_ref[s, h:h + 1, :] for s in range(group)]
        if group == 1:
            q_c = _dot(qb, c_h[0].astype(BF16))
        else:
            q_c = sum(_dot(jnp.where(in_seq[s], qh.astype(F32), 0.0).astype(BF16), c_h[s].astype(BF16))
                      for s in range(group))
        num = sc * q_c + _dot(wts.astype(BF16), vb)
        den = sc * jnp.sum(qh * per_row(n_h), axis=-1, keepdims=True) + jnp.sum(wts, axis=-1, keepdims=True)
        out = num / jnp.maximum(jnp.abs(den), jnp.exp(-mt))
        for s in range(group):
            last = (s + 1) * t_len - 1
            m_last = mt[last:last + 1, :]
            b_last = b_c[last:last + 1, :]
            sc_state = jnp.exp(b_last + m_h[s] - m_last)
            w_k = jnp.exp(b_last - b_c + ig_c - m_last)
            if group > 1:
                w_k = jnp.where(in_seq[s], w_k, 0.0)
            kw = w_k * kh
            c_ref[s, h] = sc_state * c_h[s] + _dot_tn(kw.astype(BF16), vb)
            n_ref[s, h:h + 1, :] = sc_state * n_h[s] + jnp.sum(kw, axis=0, keepdims=True)
            m_new[s] = jnp.where(lane == h, m_last, m_new[s])
        y = out * lax.rsqrt(jnp.mean(out * out, axis=-1, keepdims=True) + EPS) * hn_ref[:, vc]
        y_ref[:, vc] = (y * og_ref[:, vc]).astype(y_ref.dtype)
    for s in range(group):
        m_ref[s] = m_new[s]


def _mlstm_scan(q, k, v, og, gates, hnorm, c0, n0, m0, y_prev, *, row_start, batch, seq, chunk, group):
    heads, dqk, dv = c0.shape[1], c0.shape[2], c0.shape[3]
    dq_all, dv_all = q.shape[1], v.shape[1]
    t_chunk = chunk // group
    n_chunks = seq // t_chunk
    assert 2 * heads <= 16 and (group == 1 or n_chunks == 1)
    assert row_start % chunk == 0 and batch % group == 0
    idx = np.arange(chunk)
    same_seq = (idx[:, None] // t_chunk) == (idx[None, :] // t_chunk)
    tri = jnp.asarray(np.tril(np.ones((chunk, chunk), np.float32)) * same_seq, dtype=BF16)
    sel = jnp.asarray(np.eye(16, LANE, dtype=np.float32), dtype=BF16)
    m0p = jnp.pad(m0, ((0, 0), (0, LANE - heads))).reshape(batch, 1, LANE)
    first_blk = row_start // chunk
    tq = pl.BlockSpec((chunk, dq_all), lambda b, c: (first_blk + b * n_chunks + c, 0))
    tv = pl.BlockSpec((chunk, dv_all), lambda b, c: (first_blk + b * n_chunks + c, 0))
    tg = pl.BlockSpec((chunk, LANE), lambda b, c: (first_blk + b * n_chunks + c, 0))
    sc_ = pl.BlockSpec((group, heads, dqk, dv), lambda b, c: (b, 0, 0, 0))
    sn_ = pl.BlockSpec((group, heads, dqk), lambda b, c: (b, 0, 0))
    sm_ = pl.BlockSpec((group, 1, LANE), lambda b, c: (b, 0, 0))
    in_specs = [tq, tq, tv, tv, tg,
                pl.BlockSpec((1, dv_all), lambda b, c: (0, 0)),
                pl.BlockSpec(tri.shape, lambda b, c: (0, 0)),
                pl.BlockSpec(sel.shape, lambda b, c: (0, 0)),
                sc_, sn_, sm_]
    args = [q, k, v, og, gates, hnorm.reshape(1, dv_all).astype(F32), tri, sel, c0, n0, m0p]
    aliases = {}
    if not isinstance(y_prev, int):
        aliases[len(args)] = 0
        in_specs.append(pl.BlockSpec(memory_space=pl.ANY))
        args.append(y_prev)
    y_rows = y_prev if isinstance(y_prev, int) else y_prev.shape[0]
    blk = (_nbytes((chunk, dq_all), q.dtype) + _nbytes((chunk, dq_all), k.dtype) + _nbytes((chunk, dv_all), v.dtype)
           + _nbytes((chunk, dv_all), og.dtype) + _nbytes((chunk, LANE), F32)
           + _nbytes((chunk, dv_all), BF16) + 2 * _nbytes((group,) + c0.shape[1:], F32))
    y, c_out, n_out, m_out = pl.pallas_call(
        functools.partial(_mlstm_kernel, heads=heads, chunk=chunk, dqk=dqk, dv=dv, group=group,
                          n_aliased=len(aliases)),
        grid=(batch // group, n_chunks),
        in_specs=in_specs,
        out_specs=[tv, sc_, sn_, sm_],
        out_shape=[jax.ShapeDtypeStruct((y_rows, dv_all), BF16),
                   jax.ShapeDtypeStruct(c0.shape, F32),
                   jax.ShapeDtypeStruct(n0.shape, F32),
                   jax.ShapeDtypeStruct((batch, 1, LANE), F32)],
        input_output_aliases=aliases,
        compiler_params=_cparams(("parallel", "arbitrary"), blk, 8 * _nbytes((chunk, dv_all), F32)),
        name="mlstm_scan",
    )(*args)
    return y, c_out, n_out, m_out[:, 0, :heads]


def _gmlp_kernel(u_ref, v_ref, vg_ref, vb_ref, w_ref, bs_ref, o_ref, vn_ref, *, groups, gd):
    v = v_ref[...]
    mu = jnp.mean(v, axis=-1, keepdims=True)
    xc = v - mu
    vn = xc * lax.rsqrt(jnp.mean(xc * xc, axis=-1, keepdims=True) + EPS) * vg_ref[...] + vb_ref[...]
    vn_ref[...] = vn
    n = v.shape[0]
    causal = (lax.broadcasted_iota(jnp.int32, (n, n), 1) <= lax.broadcasted_iota(jnp.int32, (n, n), 0))
    bs = bs_ref[0]
    for g in range(groups):
        cols = slice(g * gd, (g + 1) * gd)
        wg = jnp.where(causal, w_ref[0, g], 0.0).astype(BF16)
        mix = _dot(wg, vn[:, cols].astype(BF16)) + bs[:, g:g + 1]
        o_ref[:, cols] = (u_ref[:, cols] * mix).astype(o_ref.dtype)


def _gmlp_gate(z, vg, vb, w_stack, bs_stack, *, n_prompt_chunks):
    m, d2 = z.shape
    d = d2 // 2
    groups, chunk = w_stack.shape[1], w_stack.shape[2]
    gd = d // groups
    assert m % chunk == 0

    def which(i):
        return jnp.minimum(i // n_prompt_chunks, 1)

    blk = (2 * _nbytes((chunk, d), F32) + _nbytes((chunk, d), BF16) + _nbytes((chunk, d), F32)
           + _nbytes((groups, chunk, chunk), F32))
    o, vn = pl.pallas_call(
        functools.partial(_gmlp_kernel, groups=groups, gd=gd),
        grid=(m // chunk,),
        in_specs=[pl.BlockSpec((chunk, d), lambda i: (i, 0)),
                  pl.BlockSpec((chunk, d), lambda i: (i, 1)),
                  pl.BlockSpec((1, d), lambda i: (0, 0)),
                  pl.BlockSpec((1, d), lambda i: (0, 0)),
                  pl.BlockSpec((1, groups, chunk, chunk), lambda i: (which(i), 0, 0, 0)),
                  pl.BlockSpec((1, chunk, groups), lambda i: (which(i), 0, 0))],
        out_specs=[pl.BlockSpec((chunk, d), lambda i: (i, 0)),
                   pl.BlockSpec((chunk, d), lambda i: (jnp.maximum(i - n_prompt_chunks, 0), 0))],
        out_shape=[jax.ShapeDtypeStruct((m, d), BF16),
                   jax.ShapeDtypeStruct((m - n_prompt_chunks * chunk, d), F32)],
        compiler_params=_cparams(("arbitrary",), blk, 4 * _nbytes((chunk, d), F32)),
        name="gmlp_gate",
    )(z, z, vg.reshape(1, d).astype(F32), vb.reshape(1, d).astype(F32), w_stack, bs_stack)
    return o, vn


def _causal_conv(tap, cw_ref, cb_ref, conv_w):
    y = cb_ref[...] + cw_ref[conv_w - 1:conv_w, :] * tap(0)
    for back in range(1, conv_w):
        y = y + cw_ref[conv_w - 1 - back:conv_w - back, :] * tap(back)
    return y


def _ffn_up_prompt_kernel(h_ref, wa_ref, wg_ref, cwa_ref, cwg_ref, cba_ref, cbg_ref,
                          _, act_ref, tail_a_ref, tail_g_ref, wb_scr, up_scr, *, tiles_per_seq, conv_w, row_chunk):
    i = pl.program_id(1)
    tm = h_ref.shape[0]
    n_chunks = tm // row_chunk

    @pl.when(i == 0)
    def _():
        wb_scr[0] = wa_ref[...].astype(BF16)
        wb_scr[1] = wg_ref[...].astype(BF16)

    @pl.when(lax.rem(i, tiles_per_seq) == 0)
    def _():
        up_scr[:, 0:8, :] = jnp.zeros((2, 8, up_scr.shape[2]), F32)

    def multiply(c):
        hb = h_ref[c * row_chunk:(c + 1) * row_chunk, :]
        for idx in range(2):
            up_scr[idx, 8 + c * row_chunk:8 + (c + 1) * row_chunk, :] = _dot(hb, wb_scr[idx])

    def finish(c):
        lo = 8 + c * row_chunk
        ya = _causal_conv(lambda back: up_scr[0, lo - back:lo - back + row_chunk, :], cwa_ref, cba_ref, conv_w)
        yg = _causal_conv(lambda back: up_scr[1, lo - back:lo - back + row_chunk, :], cwg_ref, cbg_ref, conv_w)
        act_ref[c * row_chunk:(c + 1) * row_chunk, :] = (ya * (yg * jax.nn.sigmoid(yg))).astype(act_ref.dtype)

    multiply(0)
    for c in range(1, n_chunks):
        multiply(c)
        finish(c - 1)
    finish(n_chunks - 1)
    for idx, tail_ref in ((0, tail_a_ref), (1, tail_g_ref)):
        last = up_scr[idx, tm:tm + 8, :]
        tail_ref[0] = last
        up_scr[idx, 0:8, :] = last


def _ffn_up_prompt(h, w_up, layer, cw, cb, act_prev, *, rows, seq):
    _, k, f2 = w_up.shape
    f = f2 // 2
    conv_w = cw.shape[0]
    tm = _pick_tile(seq, 1024, 16)
    tf = _pick_tile(f, FFN_TF, LANE)
    nf = f // tf
    n_tiles = rows // tm
    row_chunk = _pick_tile(tm, FFN_ROW_CHUNK, 16)
    blk = (_nbytes((tm, k), BF16) + 2 * _nbytes((k, tf), F32) + _nbytes((tm, tf), BF16)
           + 2 * _nbytes((8, tf), F32) + 6 * _nbytes((8, tf), F32))
    tail = jax.ShapeDtypeStruct((n_tiles, 8, f), F32)
    act, tail_a, tail_g = pl.pallas_call(
        functools.partial(_ffn_up_prompt_kernel, tiles_per_seq=seq // tm, conv_w=conv_w, row_chunk=row_chunk),
        grid=(nf, n_tiles),
        in_specs=[pl.BlockSpec((tm, k), lambda j, i: (i, 0)),
                  pl.BlockSpec((None, k, tf), lambda j, i: (layer, 0, j)),
                  pl.BlockSpec((None, k, tf), lambda j, i: (layer, 0, j + nf)),
                  pl.BlockSpec((conv_w, tf), lambda j, i: (0, j)),
                  pl.BlockSpec((conv_w, tf), lambda j, i: (0, j + nf)),
                  pl.BlockSpec((1, tf), lambda j, i: (0, j)),
                  pl.BlockSpec((1, tf), lambda j, i: (0, j + nf)),
                  pl.BlockSpec(memory_space=pl.ANY)],
        out_specs=[pl.BlockSpec((tm, tf), lambda j, i: (i, j)),
                   pl.BlockSpec((1, 8, tf), lambda j, i: (i, 0, j)),
                   pl.BlockSpec((1, 8, tf), lambda j, i: (i, 0, j))],
        out_shape=[jax.ShapeDtypeStruct(act_prev.shape, BF16), tail, tail],
        input_output_aliases={7: 0},
        scratch_shapes=[pltpu.VMEM((2, k, tf), BF16), pltpu.VMEM((2, 8 + tm, tf), F32)],
        compiler_params=_cparams(("parallel", "arbitrary"), blk,
                                 2 * _nbytes((k, tf), BF16) + 2 * _nbytes((8 + tm, tf), F32)
                                 + 6 * _nbytes((row_chunk, tf), F32)),
        name="ffn_up_conv_prompt",
    )(h, w_up, w_up, cw, cw, cb.reshape(1, f2), cb.reshape(1, f2), act_prev)
    tiles_per_seq = seq // tm
    keep = slice(8 - (conv_w - 1), 8)
    tails = jnp.concatenate([tail_a[tiles_per_seq - 1::tiles_per_seq, keep], tail_g[tiles_per_seq - 1::tiles_per_seq, keep]],
                            axis=-1)
    return act, tails


def _ffn_up_sample_kernel(*refs, steps, conv_w):
    h_ref, wa_ref, wg_ref, cwa_ref, cwg_ref, cba_ref, cbg_ref = refs[:7]
    bufs_a = refs[7:7 + conv_w - 1]
    bufs_g = refs[7 + conv_w - 1:7 + 2 * (conv_w - 1)]
    act_ref, upa_ref, upg_ref = refs[7 + 2 * (conv_w - 1):]
    bs = h_ref.shape[0] // steps
    hb = h_ref[...]

    def branch(w_ref, bufs, cw_ref, cb_ref, up_ref):
        up = _dot(hb, w_ref[...].astype(BF16))
        up_ref[...] = up

        def at(t):
            return up[t * bs:(t + 1) * bs, :] if t >= 0 else bufs[conv_w - 1 + t][...]

        return [_causal_conv(lambda back, t=t: at(t - back), cw_ref, cb_ref, conv_w) for t in range(steps)]

    ya = branch(wa_ref, bufs_a, cwa_ref, cba_ref, upa_ref)
    yg = branch(wg_ref, bufs_g, cwg_ref, cbg_ref, upg_ref)
    for t in range(steps):
        act_ref[t * bs:(t + 1) * bs, :] = (ya[t] * (yg[t] * jax.nn.sigmoid(yg[t]))).astype(act_ref.dtype)


def _ffn_up_sample(h_t, w_up, layer, cw, cb, conv_state, *, steps):
    _, k, f2 = w_up.shape
    f = f2 // 2
    conv_w = cw.shape[0]
    ms = h_t.shape[0]
    bs = ms // steps
    assert steps >= conv_w - 1 and bs % 16 == 0 and conv_state.shape[1] == bs
    tf = _pick_tile(f, FFN_TF, LANE)
    nf = f // tf
    state_specs = [pl.BlockSpec((None, bs, tf), lambda j, r=r, off=off: (r, 0, off + j))
                   for off in (0, nf) for r in range(conv_w - 1)]
    blk = (_nbytes((ms, k), BF16) + 2 * _nbytes((k, tf), F32) + 2 * _nbytes((conv_w - 1, bs, tf), F32)
           + _nbytes((ms, tf), BF16) + 2 * _nbytes((ms, tf), F32) + 6 * _nbytes((8, tf), F32))
    up_shape = jax.ShapeDtypeStruct((ms, f), F32)
    act, up_a, up_g = pl.pallas_call(
        functools.partial(_ffn_up_sample_kernel, steps=steps, conv_w=conv_w),
        grid=(nf,),
        in_specs=[pl.BlockSpec((ms, k), lambda j: (0, 0)),
                  pl.BlockSpec((None, k, tf), lambda j: (layer, 0, j)),
                  pl.BlockSpec((None, k, tf), lambda j: (layer, 0, j + nf)),
                  pl.BlockSpec((conv_w, tf), lambda j: (0, j)),
                  pl.BlockSpec((conv_w, tf), lambda j: (0, j + nf)),
                  pl.BlockSpec((1, tf), lambda j: (0, j)),
                  pl.BlockSpec((1, tf), lambda j: (0, j + nf))] + state_specs,
        out_specs=[pl.BlockSpec((ms, tf), lambda j: (0, j))] * 3,
        out_shape=[jax.ShapeDtypeStruct((ms, f), BF16), up_shape, up_shape],
        compiler_params=_cparams(("parallel",), blk, 2 * _nbytes((k, tf), BF16) + 8 * _nbytes((ms, tf), F32)),
        name="ffn_up_conv_sample",
    )(h_t, w_up, w_up, cw, cw, cb.reshape(1, f2), cb.reshape(1, f2), *([conv_state] * (2 * (conv_w - 1))))
    keep = slice((steps - (conv_w - 1)) * bs, ms)
    up_tail = jnp.concatenate([up_a[keep], up_g[keep]], axis=-1).reshape(conv_w - 1, bs, f2).transpose(1, 0, 2)
    return act, up_tail


def _forward(x_prompt, x_sample, state_hgrn_S, state_mlstm_C, state_mlstm_n, state_mlstm_m, state_ffn_conv,
             norm_mix, norm_ffn, norm_final,
             hgrn_w_q, hgrn_w_f, hgrn_w_i, hgrn_w_g, hgrn_lb, hgrn_onorm, hgrn_w_o,
             mlstm_w_q, mlstm_w_k, mlstm_w_v, mlstm_w_og, mlstm_w_if, mlstm_b_if, mlstm_hnorm, mlstm_w_out,
             gmlp_w_in, gmlp_b_in, gmlp_vnorm_g, gmlp_vnorm_b, gmlp_w_s, gmlp_b_s, gmlp_w_out,
             ffn_w_up, ffn_conv_w, ffn_conv_b, ffn_w_down):
    bp, tp, d = x_prompt.shape
    bs, ts, _ = x_sample.shape
    mp, ms = bp * tp, bs * ts
    m = mp + ms
    depth = norm_mix.shape[0]
    hg_heads, hg_dk, hg_dv = state_hgrn_S.shape[2:]
    ml_heads, ml_dqk, ml_dv = state_mlstm_C.shape[2:]

    x = jnp.concatenate([x_prompt.reshape(mp, d), x_sample.reshape(ms, d)], axis=0)

    lb = jax.nn.softmax(hgrn_lb.astype(F32), axis=0)
    lbs = jnp.cumsum(lb, axis=0) - lb[0]

    w_down = ffn_w_down.astype(BF16)
    act = jnp.zeros((m, ffn_w_down.shape[1]), BF16)
    out_c, out_n, out_m, out_v, out_conv = [], [], [], [], []
    s_p_all = s_s_all = state_hgrn_S.shape[0]
    for i in range(depth):
        j = i // N_MIXERS
        kind = i % N_MIXERS
        h = _rmsnorm(x, norm_mix[i], BF16)
        if kind == 0:
            (q,) = _matmul(h, hgrn_w_q, _epi_silu, out_dtypes=(BF16,), name="hgrn_q", layer=j)
            if j == 0:
                lf, k = _matmul(h, hgrn_w_f, _epi_hgrn_forget_no_floor,
                                out_dtypes=(F32, BF16), name="hgrn_f", layer=j)
            else:
                lf, k = _matmul(h, hgrn_w_f, _epi_hgrn_forget, extras=[("row", lbs[j])],
                                out_dtypes=(F32, BF16), name="hgrn_f", layer=j)
            (v,) = _matmul(h, hgrn_w_i, _epi_id, out_dtypes=(BF16,), name="hgrn_i", layer=j)
            (g,) = _matmul(h, hgrn_w_g, _epi_sigmoid, out_dtypes=(BF16,), name="hgrn_g", layer=j)
            y, s_p_all = _hgrn_scan(q, lf, k, v, g, hgrn_onorm[j],
                                    jnp.zeros((1, bp, hg_heads, hg_dk, hg_dv), F32), j, s_p_all, h,
                                    row_start=0, batch=bp, seq=tp, chunk=min(HGRN_CHUNK, tp), group=1)
            group = max(1, SAMPLE_ROWS // ts)
            y, s_s_all = _hgrn_scan(q, lf, k, v, g, hgrn_onorm[j], state_hgrn_S.astype(F32), j, s_s_all, y,
                                    row_start=mp, batch=bs, seq=ts, chunk=group * ts, group=group)
            (x,) = _matmul(y, hgrn_w_o, _epi_residual, extras=[("tile", x)], name="hgrn_o", layer=j)
        elif kind == 1:
            (q,) = _matmul(h, mlstm_w_q, _epi_id, out_dtypes=(BF16,), name="mlstm_q", layer=j)
            (k,) = _matmul(h, mlstm_w_k, functools.partial(_epi_scale, scale=ml_dqk ** -0.5),
                           out_dtypes=(BF16,), name="mlstm_k", layer=j)
            (v,) = _matmul(h, mlstm_w_v, _epi_id, out_dtypes=(BF16,), name="mlstm_v", layer=j)
            (og,) = _matmul(h, mlstm_w_og, _epi_sigmoid, out_dtypes=(BF16,), name="mlstm_og", layer=j)
            w_if = jnp.pad(mlstm_w_if[j], ((0, 0), (0, LANE - 2 * ml_heads)))
            b_if = jnp.pad(mlstm_b_if[j], (0, LANE - 2 * ml_heads))
            (gates,) = _matmul(h, w_if, functools.partial(_epi_mlstm_gates, heads=ml_heads),
                               extras=[("row", b_if)], name="mlstm_if")
            y, c_p, n_p, m_p = _mlstm_scan(
                q, k, v, og, gates, mlstm_hnorm[j],
                jnp.zeros((bp, ml_heads, ml_dqk, ml_dv), F32), jnp.zeros((bp, ml_heads, ml_dqk), F32),
                jnp.zeros((bp, ml_heads), F32), h,
                row_start=0, batch=bp, seq=tp, chunk=min(MLSTM_CHUNK, tp), group=1)
            group = max(1, SAMPLE_ROWS // ts)
            y, c_s, n_s, m_s = _mlstm_scan(
                q, k, v, og, gates, mlstm_hnorm[j],
                state_mlstm_C[j].astype(F32), state_mlstm_n[j].astype(F32), state_mlstm_m[j].astype(F32), y,
                row_start=mp, batch=bs, seq=ts, chunk=group * ts, group=group)
            out_c.append((c_p, c_s))
            out_n.append((n_p, n_s))
            out_m.append((m_p, m_s))
            (x,) = _matmul(y, mlstm_w_out, _epi_residual, extras=[("tile", x)], name="mlstm_out", layer=j)
        else:
            (z,) = _matmul(h, gmlp_w_in, _epi_bias_gelu, extras=[("row", gmlp_b_in[j])], name="gmlp_in", layer=j)
            groups = gmlp_w_s.shape[1]
            lp = min(GMLP_CHUNK, tp)
            ls = min(GMLP_CHUNK, ts)
            reps = GMLP_CHUNK // ls
            w_s_blk = jnp.einsum("ab,gts->gatbs", jnp.eye(reps, dtype=F32), gmlp_w_s[j][:, :ls, :ls])
            w_s_blk = w_s_blk.reshape(groups, GMLP_CHUNK, GMLP_CHUNK)
            w_p = gmlp_w_s[j][:, :lp, :lp]
            w_stack = jnp.stack([w_p, w_s_blk])
            bs_stack = jnp.stack([gmlp_b_s[j][:, :lp].T, jnp.tile(gmlp_b_s[j][:, :ls].T, (reps, 1))])
            y, vn = _gmlp_gate(z, gmlp_vnorm_g[j], gmlp_vnorm_b[j], w_stack, bs_stack,
                               n_prompt_chunks=mp // GMLP_CHUNK)
            out_v.append(vn.reshape(bs, ts, d))
            (x,) = _matmul(y, gmlp_w_out, _epi_residual, extras=[("tile", x)], name="gmlp_out", layer=j)

        h = _rmsnorm(x, norm_ffn[i], BF16)
        act, tail_p = _ffn_up_prompt(h, ffn_w_up, i, ffn_conv_w[i], ffn_conv_b[i], act, rows=mp, seq=tp)
        h_t = h[mp:].reshape(bs, ts, d).transpose(1, 0, 2).reshape(ms, d)
        act_t, tail_s = _ffn_up_sample(h_t, ffn_w_up, i, ffn_conv_w[i], ffn_conv_b[i],
                                       state_ffn_conv[i].astype(F32).transpose(1, 0, 2), steps=ts)
        act_s = act_t.reshape(ts, bs, -1).transpose(1, 0, 2).reshape(ms, -1)
        act = lax.dynamic_update_slice(act, act_s, (mp, 0))
        out_conv.append((tail_p, tail_s))
        (x,) = _matmul(act, w_down, _epi_residual, extras=[("tile", x)], name="ffn_down", layer=i)

    y_p = _rmsnorm(x, norm_final, F32, row_start=0, n_rows=mp).reshape(bp, tp, d)
    y_s = _rmsnorm(x, norm_final, F32, row_start=mp, n_rows=ms).reshape(bs, ts, d)
    stack = lambda pairs, idx: jnp.stack([p[idx] for p in pairs])
    return (y_p, y_s,
            s_p_all, s_s_all, stack(out_c, 0), stack(out_c, 1),
            stack(out_n, 0), stack(out_n, 1), stack(out_m, 0), stack(out_m, 1),
            jnp.stack(out_v), stack(out_conv, 0), stack(out_conv, 1))


_forward_jit = jax.jit(_forward)


def kernel(x_prompt, x_sample, state_hgrn_S, state_mlstm_C, state_mlstm_n, state_mlstm_m, state_ffn_conv, norm_mix, norm_ffn, norm_final, hgrn_w_q, hgrn_w_f, hgrn_w_i, hgrn_w_g, hgrn_lb, hgrn_onorm, hgrn_w_o, mlstm_w_q, mlstm_w_k, mlstm_w_v, mlstm_w_og, mlstm_w_if, mlstm_b_if, mlstm_hnorm, mlstm_w_out, gmlp_w_in, gmlp_b_in, gmlp_vnorm_g, gmlp_vnorm_b, gmlp_w_s, gmlp_b_s, gmlp_w_out, ffn_w_up, ffn_conv_w, ffn_conv_b, ffn_w_down):
    return _forward_jit(x_prompt, x_sample, state_hgrn_S, state_mlstm_C, state_mlstm_n, state_mlstm_m, state_ffn_conv,
                        norm_mix, norm_ffn, norm_final,
                        hgrn_w_q, hgrn_w_f, hgrn_w_i, hgrn_w_g, hgrn_lb, hgrn_onorm, hgrn_w_o,
                        mlstm_w_q, mlstm_w_k, mlstm_w_v, mlstm_w_og, mlstm_w_if, mlstm_b_if, mlstm_hnorm, mlstm_w_out,
                        gmlp_w_in, gmlp_b_in, gmlp_vnorm_g, gmlp_vnorm_b, gmlp_w_s, gmlp_b_s, gmlp_w_out,
                        ffn_w_up, ffn_conv_w, ffn_conv_b, ffn_w_down)
```
